```python
import jax, jax.numpy as jnp
from jax import lax
import numpy as np

D_MODEL = 2048
BATCH = 4
SEQ = 2048
DEPTH = 2
DEC_BATCH = 128
DEC_SEQ = 4
PAST_LEN = 16384
PAGE_SIZE = 128

GLA_HEADS = 4
GLA_DK = D_MODEL // 2 // GLA_HEADS
GLA_DV = D_MODEL // GLA_HEADS
GLA_QK = GLA_HEADS * GLA_DK
GLA_V = GLA_HEADS * GLA_DV
GLA_LOWRANK = 16
GLA_TAU = 16.0
GLA_CHUNK = 16
CM_GROUPS = 8
CM_WIDTH = D_MODEL
CM_GROUP_DIM = CM_WIDTH // CM_GROUPS
CM_CHUNK = 128
SSD_INNER = 2 * D_MODEL
SSD_HEADDIM = 64
SSD_HEADS = SSD_INNER // SSD_HEADDIM
SSD_GROUPS = 8
SSD_STATE = 128
SSD_CONV = 4
SSD_CHUNK = 64
SSD_CONV_DIM = SSD_INNER + 2 * SSD_GROUPS * SSD_STATE
D_FF = -(-8 * D_MODEL // (3 * 256)) * 256
PLE_DIM = 256
IN_WIDTH = 2 * GLA_QK + 2 * GLA_V + GLA_LOWRANK + 2 * CM_WIDTH + SSD_INNER + SSD_CONV_DIM + SSD_HEADS + 3 * D_MODEL
EPS = 1e-6

kernel_name = 'hybrid_gla_gmlp_ssd_step'


def _rmsnorm(x, g):
    xf = x.astype(jnp.float32)
    y = xf * lax.rsqrt(jnp.mean(xf * xf, axis=-1, keepdims=True) + EPS)
    return (y * g.astype(jnp.float32)).astype(x.dtype)


def _group_rmsnorm(x, g, groups):
    shp = x.shape
    xf = x.astype(jnp.float32).reshape(*shp[:-1], groups, shp[-1] // groups)
    y = xf * lax.rsqrt(jnp.mean(xf * xf, axis=-1, keepdims=True) + EPS)
    return (y.reshape(shp) * g.astype(jnp.float32)).astype(x.dtype)


def _layernorm(x, g, b):
    xf = x.astype(jnp.float32)
    mu = jnp.mean(xf, axis=-1, keepdims=True)
    xc = xf - mu
    var = jnp.mean(xc * xc, axis=-1, keepdims=True)
    return (xc * lax.rsqrt(var + EPS) * g.astype(jnp.float32) + b.astype(jnp.float32)).astype(x.dtype)


def _to_chunks(t, c):
    bsz, length = t.shape[:2]
    n = -(-length // c)
    t = jnp.pad(t, [(0, 0), (0, n * c - length)] + [(0, 0)] * (t.ndim - 2))
    return jnp.swapaxes(t.reshape(bsz, n, c, *t.shape[2:]), 0, 1)


def _from_chunks(t, length):
    n, bsz, c = t.shape[:3]
    return jnp.swapaxes(t, 0, 1).reshape(bsz, n * c, *t.shape[3:])[:, :length]


def _gla_scan(q, k, v, log_a, s0):
    length = q.shape[1]
    c = min(GLA_CHUNK, length)
    f32 = jnp.float32
    qc, kc, vc, ac = (_to_chunks(t.astype(f32), c) for t in (q, k, v, log_a))
    causal = jnp.tril(jnp.ones((c, c), dtype=bool))[None, :, :, None, None]

    def step(s, inp):
        qi, ki, vi, ai = inp
        b = jnp.cumsum(ai, axis=1)
        o_inter = jnp.einsum('bthk,bhkv->bthv', qi * jnp.exp(b), s)
        rel = jnp.where(causal, b[:, :, None] - b[:, None], -jnp.inf)
        att = jnp.einsum('bthk,bshk,btshk->bhts', qi, ki, jnp.exp(rel))
        o_intra = jnp.einsum('bhts,bshv->bthv', att, vi)
        b_end = b[:, -1]
        k_dec = ki * jnp.exp(b_end[:, None] - b)
        s = s * jnp.exp(b_end)[..., None] + jnp.einsum('bshk,bshv->bhkv', k_dec, vi)
        return s, o_inter + o_intra

    s, o = lax.scan(step, s0.astype(f32), (qc, kc, vc, ac))
    return _from_chunks(o, length), s


def _ssd_scan(x, dt, a_neg, b_in, c_in, h0):
    bsz, length = x.shape[:2]
    g, r = SSD_GROUPS, SSD_HEADS // SSD_GROUPS
    c = min(SSD_CHUNK, length)
    f32 = jnp.float32
    xg = x.astype(f32).reshape(bsz, length, g, r, SSD_HEADDIM)
    dtg = dt.astype(f32).reshape(bsz, length, g, r)
    ag = dtg * a_neg.reshape(g, r)
    xc, dtc, acc, bc, cc = (_to_chunks(t, c) for t in (xg, dtg, ag, b_in.astype(f32), c_in.astype(f32)))
    causal = jnp.tril(jnp.ones((c, c), dtype=bool))[None, :, :, None, None]

    def step(h, inp):
        xi, dti, ai, bi, ci = inp
        cs = jnp.cumsum(ai, axis=1)
        decay = jnp.exp(jnp.where(causal, cs[:, :, None] - cs[:, None], -jnp.inf))
        cb = jnp.einsum('btgn,bsgn->btsg', ci, bi)
        w = cb[..., None] * decay * dti[:, None]
        y_intra = jnp.einsum('btsgr,bsgrp->btgrp', w, xi)
        y_inter = jnp.einsum('btgn,bgrpn->btgrp', ci, h) * jnp.exp(cs)[..., None]
        cs_end = cs[:, -1]
        wx = (jnp.exp(cs_end[:, None] - cs) * dti)[..., None] * xi
        h = h * jnp.exp(cs_end)[..., None, None] + jnp.einsum('bsgrp,bsgn->bgrpn', wx, bi)
        return h, y_intra + y_inter

    h_init = h0.astype(f32).reshape(bsz, g, r, SSD_HEADDIM, SSD_STATE)
    h, y = lax.scan(step, h_init, (xc, dtc, acc, bc, cc))
    y = _from_chunks(y, length).reshape(bsz, length, SSD_HEADS, SSD_HEADDIM)
    return y, h.reshape(bsz, SSD_HEADS, SSD_HEADDIM, SSD_STATE)


def _causal_conv(xbc, buf, w, b):
    length = xbc.shape[1]
    xp = jnp.concatenate([buf.astype(xbc.dtype), xbc], axis=1)
    out = b + sum(xp[:, j:j + length] * w[j] for j in range(SSD_CONV))
    return jax.nn.silu(out), xp[:, -(SSD_CONV - 1):]


def _chunk_mlp(uv, ln_g, ln_b, w_s, b_s):
    bsz, length, _ = uv.shape
    u, v = jnp.split(jax.nn.gelu(uv), 2, axis=-1)
    v = _layernorm(v, ln_g, ln_b)
    lc = min(CM_CHUNK, length)
    ws = jnp.where(jnp.tril(jnp.ones((lc, lc), dtype=bool)), w_s[:, :lc, :lc], 0)
    vc = _to_chunks(v.reshape(bsz, length, CM_GROUPS, CM_GROUP_DIM), lc)
    mixed = jnp.einsum('gts,nbsgd->nbtgd', ws, vc) + b_s[:, :lc].T[:, :, None]
    y = u * _from_chunks(mixed, length).reshape(bsz, length, CM_WIDTH)
    return y, v


def _layer(x, p, s_gla, s_ssm, s_conv, g_mix, w_in, w_gla_lr, b_gla_lr, g_gla_norm,
           cm_ln_g, cm_ln_b, cm_ws, cm_bs, ssd_conv_w, ssd_conv_b, ssd_dt_bias, ssd_a_log,
           ssd_d, g_ssd_norm, w_br_gla, w_br_cm, w_br_ssd, w_o, g_ffn, w_ffn_in, w_ffn_out,
           g_ple, w_ple_gate, w_ple_proj):
    bsz, length, _ = x.shape
    f32 = jnp.float32
    n = _rmsnorm(x, g_mix)
    proj = n @ w_in
    sizes = [GLA_QK, GLA_QK, GLA_V, GLA_V, GLA_LOWRANK, 2 * CM_WIDTH, SSD_INNER, SSD_CONV_DIM, SSD_HEADS, 3 * D_MODEL]
    q, k, v, g_out, lr, uv, z, xbc, dt_raw, gates = jnp.split(proj, np.cumsum(sizes)[:-1].tolist(), axis=-1)

    q = q.reshape(bsz, length, GLA_HEADS, GLA_DK) * (GLA_DK ** -0.5)
    k = k.reshape(bsz, length, GLA_HEADS, GLA_DK)
    v = v.reshape(bsz, length, GLA_HEADS, GLA_DV)
    log_a = jax.nn.log_sigmoid((lr @ w_gla_lr + b_gla_lr).astype(f32)) / GLA_TAU
    o, s_gla_new = _gla_scan(q, k, v, log_a.reshape(bsz, length, GLA_HEADS, GLA_DK), s_gla)
    y_a = _group_rmsnorm(o.reshape(bsz, length, GLA_V), g_gla_norm, GLA_HEADS).astype(x.dtype) * jax.nn.silu(g_out)

    y_b, v_rows = _chunk_mlp(uv, cm_ln_g, cm_ln_b, cm_ws, cm_bs)

    xbc, s_conv_new = _causal_conv(xbc, s_conv, ssd_conv_w, ssd_conv_b)
    xs, b_ssm, c_ssm = jnp.split(xbc, [SSD_INNER, SSD_INNER + SSD_GROUPS * SSD_STATE], axis=-1)
    dt = jax.nn.softplus((dt_raw + ssd_dt_bias).astype(f32))
    a_neg = -jnp.exp(ssd_a_log.astype(f32))
    xs = xs.reshape(bsz, length, SSD_HEADS, SSD_HEADDIM)
    y, s_ssm_new = _ssd_scan(xs, dt, a_neg,
                             b_ssm.reshape(bsz, length, SSD_GROUPS, SSD_STATE),
                             c_ssm.reshape(bsz, length, SSD_GROUPS, SSD_STATE), s_ssm)
    y = y + ssd_d.astype(f32)[:, None] * xs.astype(f32)
    y_c = _group_rmsnorm(y.reshape(bsz, length, SSD_INNER) * jax.nn.silu(z.astype(f32)),
                         g_ssd_norm, SSD_GROUPS).astype(x.dtype)

    g_a, g_b, g_c = jnp.split(jax.nn.sigmoid(gates), 3, axis=-1)
    mix = g_a * (y_a @ w_br_gla) + g_b * (y_b @ w_br_cm) + g_c * (y_c @ w_br_ssd)
    x = x + mix @ w_o

    gate, up = jnp.split(_rmsnorm(x, g_ffn) @ w_ffn_in, 2, axis=-1)
    x = x + (jax.nn.silu(gate) * up) @ w_ffn_out

    x = x + jax.nn.sigmoid(_rmsnorm(x, g_ple) @ w_ple_gate) * (p.astype(x.dtype) @ w_ple_proj)
    return x, s_gla_new.astype(x.dtype), s_ssm_new.astype(x.dtype), s_conv_new.astype(x.dtype), v_rows


def _trunk(x, p, s_gla, s_ssm, s_conv, weights, g_final):
    gla_out, ssm_out, conv_out, v_out = [], [], [], []
    for i in range(DEPTH):
        layer_w = [w[i] for w in weights]
        x, sg, ss, sc, vr = _layer(x, p[i], s_gla[i], s_ssm[i], s_conv[i], *layer_w)
        gla_out.append(sg)
        ssm_out.append(ss)
        conv_out.append(sc)
        v_out.append(vr)
    return (_rmsnorm(x, g_final), jnp.stack(gla_out), jnp.stack(ssm_out),
            jnp.stack(conv_out), jnp.stack(v_out))


def setup_inputs(seed: int = 0) -> dict:
    key = jax.random.key(seed)
    ks = iter(list(jax.random.split(key, 48)))
    f32 = jnp.float32

    def nrm(shape, scale):
        return jax.random.normal(next(ks), shape, f32) * scale

    def gain(shape):
        return 1.0 + nrm(shape, 0.02)

    u_dt = jax.random.uniform(next(ks), (DEPTH, SSD_HEADS), f32)
    dt0 = jnp.exp(u_dt * (jnp.log(0.1) - jnp.log(0.001)) + jnp.log(0.001))
    a_init = jax.random.uniform(next(ks), (DEPTH, SSD_HEADS), f32, 1.0, 16.0)
    return {
        'x_prompt': nrm((BATCH, SEQ, D_MODEL), 1.0),
        'x_sample': nrm((DEC_BATCH, DEC_SEQ, D_MODEL), 1.0),
        'state_gla': nrm((DEPTH, DEC_BATCH, GLA_HEADS, GLA_DK, GLA_DV), 0.5),
        'state_ssm': nrm((DEPTH, DEC_BATCH, SSD_HEADS, SSD_HEADDIM, SSD_STATE), 0.5),
        'state_conv': nrm((DEPTH, DEC_BATCH, SSD_CONV - 1, SSD_CONV_DIM), 1.0),
        'p_prompt': nrm((DEPTH, BATCH, SEQ, PLE_DIM), 1.0),
        'p_sample': nrm((DEPTH, DEC_BATCH, DEC_SEQ, PLE_DIM), 1.0),
        'g_mix': gain((DEPTH, D_MODEL)),
        'w_in': nrm((DEPTH, D_MODEL, IN_WIDTH), D_MODEL ** -0.5),
        'w_gla_lr': nrm((DEPTH, GLA_LOWRANK, GLA_QK), GLA_LOWRANK ** -0.5),
        'b_gla_lr': nrm((DEPTH, GLA_QK), 0.1),
        'g_gla_norm': gain((DEPTH, GLA_V)),
        'cm_ln_g': gain((DEPTH, CM_WIDTH)),
        'cm_ln_b': nrm((DEPTH, CM_WIDTH), 0.02),
        'cm_ws': nrm((DEPTH, CM_GROUPS, CM_CHUNK, CM_CHUNK), CM_CHUNK ** -0.5),
        'cm_bs': gain((DEPTH, CM_GROUPS, CM_CHUNK)),
        'ssd_conv_w': nrm((DEPTH, SSD_CONV, SSD_CONV_DIM), SSD_CONV ** -0.5),
        'ssd_conv_b': nrm((DEPTH, SSD_CONV_DIM), 0.02),
        'ssd_dt_bias': dt0 + jnp.log(-jnp.expm1(-dt0)),
        'ssd_a_log': jnp.log(a_init),
        'ssd_d': gain((DEPTH, SSD_HEADS)),
        'g_ssd_norm': gain((DEPTH, SSD_INNER)),
        'w_br_gla': nrm((DEPTH, GLA_V, D_MODEL), GLA_V ** -0.5),
        'w_br_cm': nrm((DEPTH, CM_WIDTH, D_MODEL), CM_WIDTH ** -0.5),
        'w_br_ssd': nrm((DEPTH, SSD_INNER, D_MODEL), SSD_INNER ** -0.5),
        'w_o': nrm((DEPTH, D_MODEL, D_MODEL), D_MODEL ** -0.5),
        'g_ffn': gain((DEPTH, D_MODEL)),
        'w_ffn_in': nrm((DEPTH, D_MODEL, 2 * D_FF), D_MODEL ** -0.5),
        'w_ffn_out': nrm((DEPTH, D_FF, D_MODEL), D_FF ** -0.5),
        'g_ple': gain((DEPTH, D_MODEL)),
        'w_ple_gate': nrm((DEPTH, D_MODEL, D_MODEL), D_MODEL ** -0.5),
        'w_ple_proj': nrm((DEPTH, PLE_DIM, D_MODEL), PLE_DIM ** -0.5),
        'g_final': gain((D_MODEL,)),
    }


def reference(x_prompt, x_sample, state_gla, state_ssm, state_conv, p_prompt, p_sample,
              g_mix, w_in, w_gla_lr, b_gla_lr, g_gla_norm, cm_ln_g, cm_ln_b, cm_ws, cm_bs,
              ssd_conv_w, ssd_conv_b, ssd_dt_bias, ssd_a_log, ssd_d, g_ssd_norm,
              w_br_gla, w_br_cm, w_br_ssd, w_o, g_ffn, w_ffn_in, w_ffn_out,
              g_ple, w_ple_gate, w_ple_proj, g_final):
    weights = (g_mix, w_in, w_gla_lr, b_gla_lr, g_gla_norm, cm_ln_g, cm_ln_b, cm_ws, cm_bs,
               ssd_conv_w, ssd_conv_b, ssd_dt_bias, ssd_a_log, ssd_d, g_ssd_norm,
               w_br_gla, w_br_cm, w_br_ssd, w_o, g_ffn, w_ffn_in, w_ffn_out,
               g_ple, w_ple_gate, w_ple_proj)
    bp = x_prompt.shape[0]
    dt_p = x_prompt.dtype
    zero_gla = jnp.zeros((DEPTH, bp, GLA_HEADS, GLA_DK, GLA_DV), dt_p)
    zero_ssm = jnp.zeros((DEPTH, bp, SSD_HEADS, SSD_HEADDIM, SSD_STATE), dt_p)
    zero_conv = jnp.zeros((DEPTH, bp, SSD_CONV - 1, SSD_CONV_DIM), dt_p)
    y_prompt, gla_p, ssm_p, conv_p, _ = _trunk(x_prompt, p_prompt, zero_gla, zero_ssm, zero_conv, weights, g_final)
    y_sample, gla_s, ssm_s, conv_s, v_s = _trunk(x_sample, p_sample, state_gla, state_ssm, state_conv, weights, g_final)
    return (y_prompt, y_sample, gla_p, gla_s, ssm_p, ssm_s, conv_p, conv_s, v_s)
```

```python
import functools

import numpy as np
import jax
import jax.numpy as jnp
from jax import lax
from jax.experimental import pallas as pl
from jax.experimental.pallas import tpu as pltpu

F32 = jnp.float32
BF16 = jnp.bfloat16

D_MODEL = 2048
DEPTH = 2
GLA_HEADS = 4
GLA_DK = 256
GLA_DV = 512
GLA_QK = 1024
GLA_V = 2048
GLA_LOWRANK = 16
GLA_TAU = 16.0
CM_GROUPS = 8
CM_WIDTH = 2048
CM_GROUP_DIM = 256
CM_CHUNK = 128
SSD_INNER = 4096
SSD_HEADDIM = 64
SSD_HEADS = 64
SSD_GROUPS = 8
SSD_STATE = 128
SSD_CONV = 4
SSD_CONV_DIM = 6144
SSD_GROUP_WIDTH = SSD_INNER // SSD_GROUPS
D_FF = 5632
PLE_DIM = 256
EPS = 1e-6

LANES = 128
SUBLANES = 8
VMEM_LIMIT = 52 * 1024 * 1024

C_Q = 0
C_K = 1024
C_V = 2048
C_G = 4096
C_UV = 6144
C_Z = 10240
C_XBC = 14336
C_GATE = 20480
C_LR = 26624
C_DT = 26752
N_PROJ = 26880

CHUNK = 128
SAMPLE_ROWS = 8
SAMPLE_LEN = 4
SAMPLE_SHIFT = 2
HEADDIM_SHIFT = 6


def _params(sem):
    return pltpu.CompilerParams(dimension_semantics=sem, vmem_limit_bytes=VMEM_LIMIT)


def _nt(a, b):
    return lax.dot_general(a, b, (((1,), (1,)), ((), ())), preferred_element_type=F32)


def _tn(a, b):
    return lax.dot_general(a, b, (((0,), (0,)), ((), ())), preferred_element_type=F32)


def _dot(a, b):
    return jnp.dot(a, b, preferred_element_type=F32)


def _split2(x):
    hi = x.astype(BF16)
    lo = (x - hi.astype(F32)).astype(BF16)
    return hi, lo


def _softplus(x):
    return jnp.maximum(x, 0.0) + jnp.log1p(jnp.exp(-jnp.abs(x)))


def _log_sigmoid(x):
    return jnp.minimum(x, 0.0) - jnp.log1p(jnp.exp(-jnp.abs(x)))


def _silu(x):
    return x * jax.nn.sigmoid(x)


def _norm_mm_body(x_ref, g_ref, w_ref, o_ref, n_scr):
    @pl.when(pl.program_id(1) == 0)
    def _():
        x = x_ref[...]
        ms = jnp.mean(x * x, axis=-1, keepdims=True)
        n_scr[...] = (x * lax.rsqrt(ms + EPS) * g_ref[...]).astype(BF16)

    o_ref[...] = _dot(n_scr[...], w_ref[...])


def _norm_matmul(x, g, w, tm, tn):
    t, k = x.shape
    n = w.shape[1]
    return pl.pallas_call(
        _norm_mm_body,
        grid=(t // tm, n // tn),
        in_specs=[
            pl.BlockSpec((tm, k), lambda i, j: (i, 0)),
            pl.BlockSpec((1, k), lambda i, j: (0, 0)),
            pl.BlockSpec((k, tn), lambda i, j: (0, j)),
        ],
        out_specs=pl.BlockSpec((tm, tn), lambda i, j: (i, j)),
        out_shape=jax.ShapeDtypeStruct((t, n), F32),
        scratch_shapes=[pltpu.VMEM((tm, k), BF16)],
        compiler_params=_params(("parallel", "arbitrary")),
        name="norm_matmul",
    )(x, g, w)


def _ffn_in_body(x_ref, g_ref, wg_ref, wu_ref, o_ref, n_scr):
    @pl.when(pl.program_id(1) == 0)
    def _():
        x = x_ref[...]
        ms = jnp.mean(x * x, axis=-1, keepdims=True)
        n_scr[...] = (x * lax.rsqrt(ms + EPS) * g_ref[...]).astype(BF16)

    n = n_scr[...]
    gate = _dot(n, wg_ref[...])
    up = _dot(n, wu_ref[...])
    o_ref[...] = (_silu(gate) * up).astype(o_ref.dtype)


def _ffn_in(x, g, w, tm, tn):
    t, k = x.shape
    nblk = D_FF // tn
    return pl.pallas_call(
        _ffn_in_body,
        grid=(t // tm, nblk),
        in_specs=[
            pl.BlockSpec((tm, k), lambda i, j: (i, 0)),
            pl.BlockSpec((1, k), lambda i, j: (0, 0)),
            pl.BlockSpec((k, tn), lambda i, j: (0, j)),
            pl.BlockSpec((k, tn), lambda i, j: (0, j + nblk)),
        ],
        out_specs=pl.BlockSpec((tm, tn), lambda i, j: (i, j)),
        out_shape=jax.ShapeDtypeStruct((t, D_FF), BF16),
        scratch_shapes=[pltpu.VMEM((tm, k), BF16)],
        compiler_params=_params(("parallel", "arbitrary")),
        name="ffn_in",
    )(x, g, w, w)


def _mm_res_body(a_ref, w_ref, r_ref, o_ref):
    o_ref[...] = r_ref[...] + _dot(a_ref[...].astype(BF16), w_ref[...])


def _matmul_residual(a, w, res, tm, tn):
    t, k = a.shape
    n = w.shape[1]
    return pl.pallas_call(
        _mm_res_body,
        grid=(t // tm, n // tn),
        in_specs=[
            pl.BlockSpec((tm, k), lambda i, j: (i, 0)),
            pl.BlockSpec((k, tn), lambda i, j: (0, j)),
            pl.BlockSpec((tm, tn), lambda i, j: (i, j)),
        ],
        out_specs=pl.BlockSpec((tm, tn), lambda i, j: (i, j)),
        out_shape=jax.ShapeDtypeStruct((t, n), F32),
        compiler_params=_params(("parallel", "parallel")),
        name="matmul_residual",
    )(a, w, res)


def _merge_body(ya_ref, yb_ref, yc_ref, ga_ref, gb_ref, gc_ref, wa_ref, wb_ref, wc_ref, o_ref):
    acc = jax.nn.sigmoid(ga_ref[...]) * _dot(ya_ref[...].astype(BF16), wa_ref[...])
    acc += jax.nn.sigmoid(gb_ref[...]) * _dot(yb_ref[...].astype(BF16), wb_ref[...])
    acc += jax.nn.sigmoid(gc_ref[...]) * _dot(yc_ref[...].astype(BF16), wc_ref[...])
    o_ref[...] = acc.astype(o_ref.dtype)


def _merge(ya, yb, yc, proj, wa, wb, wc, tm, tn):
    t = ya.shape[0]
    nblk = D_MODEL // tn
    g0 = C_GATE // tn
    return pl.pallas_call(
        _merge_body,
        grid=(t // tm, nblk),
        in_specs=[
            pl.BlockSpec((tm, GLA_V), lambda i, j: (i, 0)),
            pl.BlockSpec((tm, CM_WIDTH), lambda i, j: (i, 0)),
            pl.BlockSpec((tm, SSD_INNER), lambda i, j: (i, 0)),
            pl.BlockSpec((tm, tn), lambda i, j: (i, g0 + j)),
            pl.BlockSpec((tm, tn), lambda i, j: (i, g0 + nblk + j)),
            pl.BlockSpec((tm, tn), lambda i, j: (i, g0 + 2 * nblk + j)),
            pl.BlockSpec((GLA_V, tn), lambda i, j: (0, j)),
            pl.BlockSpec((CM_WIDTH, tn), lambda i, j: (0, j)),
            pl.BlockSpec((SSD_INNER, tn), lambda i, j: (0, j)),
        ],
        out_specs=pl.BlockSpec((tm, tn), lambda i, j: (i, j)),
        out_shape=jax.ShapeDtypeStruct((t, D_MODEL), BF16),
        compiler_params=_params(("parallel", "parallel")),
        name="merge",
    )(ya, yb, yc, proj, proj, proj, wa, wb, wc)


def _ple_body(x_ref, xr_ref, g_ref, p_ref, wg_ref, wp_ref, o_ref, n_scr):
    @pl.when(pl.program_id(1) == 0)
    def _():
        x = x_ref[...]
        ms = jnp.mean(x * x, axis=-1, keepdims=True)
        n_scr[...] = (x * lax.rsqrt(ms + EPS) * g_ref[...]).astype(BF16)

    gate = jax.nn.sigmoid(_dot(n_scr[...], wg_ref[...]))
    emb = _dot(p_ref[...].astype(BF16), wp_ref[...])
    o_ref[...] = xr_ref[...] + gate * emb


def _ple(x, g, p, wg, wp, tm, tn):
    t = x.shape[0]
    return pl.pallas_call(
        _ple_body,
        grid=(t // tm, D_MODEL // tn),
        in_specs=[
            pl.BlockSpec((tm, D_MODEL), lambda i, j: (i, 0)),
            pl.BlockSpec((tm, tn), lambda i, j: (i, j)),
            pl.BlockSpec((1, D_MODEL), lambda i, j: (0, 0)),
            pl.BlockSpec((tm, PLE_DIM), lambda i, j: (i, 0)),
            pl.BlockSpec((D_MODEL, tn), lambda i, j: (0, j)),
            pl.BlockSpec((PLE_DIM, tn), lambda i, j: (0, j)),
        ],
        out_specs=pl.BlockSpec((tm, tn), lambda i, j: (i, j)),
        out_shape=jax.ShapeDtypeStruct((t, D_MODEL), F32),
        scratch_shapes=[pltpu.VMEM((tm, D_MODEL), BF16)],
        compiler_params=_params(("parallel", "arbitrary")),
        name="ple",
    )(x, x, g, p, wg, wp)


def _rmsnorm_body(x_ref, g_ref, o_ref):
    x = x_ref[...]
    ms = jnp.mean(x * x, axis=-1, keepdims=True)
    o_ref[...] = x * lax.rsqrt(ms + EPS) * g_ref[...]


def _rmsnorm(x, g, tm):
    t, k = x.shape
    return pl.pallas_call(
        _rmsnorm_body,
        grid=(t // tm,),
        in_specs=[pl.BlockSpec((tm, k), lambda i: (i, 0)), pl.BlockSpec((1, k), lambda i: (0, 0))],
        out_specs=pl.BlockSpec((tm, k), lambda i: (i, 0)),
        out_shape=jax.ShapeDtypeStruct((t, k), F32),
        compiler_params=_params(("parallel",)),
        name="final_rmsnorm",
    )(x, g)


def _cmlp_body(u_ref, v_ref, lng_ref, lnb_ref, ws_ref, bst_ref, yb_ref, *rest, seq_len):
    u = jax.nn.gelu(u_ref[...])
    v = jax.nn.gelu(v_ref[...])
    mu = jnp.mean(v, axis=-1, keepdims=True)
    vc = v - mu
    var = jnp.mean(vc * vc, axis=-1, keepdims=True)
    vn = vc * lax.rsqrt(var + EPS) * lng_ref[...] + lnb_ref[...]
    if rest:
        rest[0][...] = vn
    r = lax.broadcasted_iota(jnp.int32, (CM_CHUNK, CM_CHUNK), 0)
    c = lax.broadcasted_iota(jnp.int32, (CM_CHUNK, CM_CHUNK), 1)
    if seq_len >= CM_CHUNK:
        keep = r >= c
    else:
        sh = seq_len.bit_length() - 1
        keep = ((r >> sh) == (c >> sh)) & ((r & (seq_len - 1)) >= (c & (seq_len - 1)))
    for g in range(CM_GROUPS):
        sl = slice(g * CM_GROUP_DIM, (g + 1) * CM_GROUP_DIM)
        w = jnp.where(keep, ws_ref[g], 0.0).astype(BF16)
        mixed = _dot(w, vn[:, sl].astype(BF16)) + bst_ref[:, g:g + 1]
        yb_ref[:, sl] = (u[:, sl] * mixed).astype(yb_ref.dtype)


def _cmlp(proj, ln_g, ln_b, ws_tiled, bs_t, seq_len, emit_v, out_dtype):
    t = proj.shape[0]
    out_shape = [jax.ShapeDtypeStruct((t, CM_WIDTH), out_dtype)]
    out_specs = [pl.BlockSpec((CM_CHUNK, CM_WIDTH), lambda i: (i, 0))]
    if emit_v:
        out_shape.append(jax.ShapeDtypeStruct((t, CM_WIDTH), F32))
        out_specs.append(pl.BlockSpec((CM_CHUNK, CM_WIDTH), lambda i: (i, 0)))
    cu = C_UV // CM_WIDTH
    res = pl.pallas_call(
        functools.partial(_cmlp_body, seq_len=seq_len),
        grid=(t // CM_CHUNK,),
        in_specs=[
            pl.BlockSpec((CM_CHUNK, CM_WIDTH), lambda i: (i, cu)),
            pl.BlockSpec((CM_CHUNK, CM_WIDTH), lambda i: (i, cu + 1)),
            pl.BlockSpec((1, CM_WIDTH), lambda i: (0, 0)),
            pl.BlockSpec((1, CM_WIDTH), lambda i: (0, 0)),
            pl.BlockSpec((CM_GROUPS, CM_CHUNK, CM_CHUNK), lambda i: (0, 0, 0)),
            pl.BlockSpec((CM_CHUNK, CM_GROUPS), lambda i: (0, 0)),
        ],
        out_specs=out_specs,
        out_shape=out_shape,
        compiler_params=_params(("parallel",)),
        name="chunk_mlp",
    )(proj, proj, ln_g, ln_b, ws_tiled, bs_t)
    return (res[0], res[1]) if emit_v else (res[0], None)


def _gla_level_constants(n):
    levels = n.bit_length() - 1
    t = np.arange(n)[:, None]
    j = np.arange(n)[None, :]
    sums = [j <= t, j > t]
    masks = [t == j]
    for l in range(1, levels + 1):
        w, half = 1 << l, 1 << (l - 1)
        start = (t >> l) << l
        upper = ((t >> (l - 1)) & 1) == 1
        a_up = (j >= start + half) & (j <= t)
        a_lo = (j > t) & (j < start + half)
        sums.append(np.where(upper, a_up, a_lo))
        jj = j
        upper_t = ((t >> (l - 1)) & 1) == 1
        lower_s = ((jj >> (l - 1)) & 1) == 0
        masks.append(((t >> l) == (jj >> l)) & upper_t & lower_s)
    return (jnp.asarray(np.stack(sums), BF16), jnp.asarray(np.stack(masks), F32), levels)


def _gla_log_decay(lr_ref, wlr_ref, blr_ref):
    x = _dot(lr_ref[...].astype(BF16), wlr_ref[...]) + blr_ref[...]
    return _log_sigmoid(x) * (1.0 / GLA_TAU)


def _gla_finish(o, g, gn):
    ms = jnp.mean(o * o, axis=-1, keepdims=True)
    return o * lax.rsqrt(ms + EPS) * gn * _silu(g)


def _gla_prompt_body(q_ref, k_ref, v_ref, g_ref, lr_ref, wlr_ref, blr_ref, gn_ref, a_ref, m_ref,
                     ya_ref, s_ref, *, levels):
    rows = q_ref.shape[0]

    @pl.when(pl.program_id(1) == 0)
    def _():
        s_ref[...] = jnp.zeros(s_ref.shape, F32)

    q = q_ref[...] * (GLA_DK ** -0.5)
    k = k_ref[...]
    la = _gla_log_decay(lr_ref, wlr_ref, blr_ref)
    la2 = jnp.concatenate(_split2(la), axis=1)

    def decay_sum(i):
        d = _dot(a_ref[i], la2)
        return d[:, :GLA_QK] + d[:, GLA_QK:]

    b = decay_sum(0)
    qe = (q * jnp.exp(b)).astype(BF16)
    kd = (k * jnp.exp(decay_sum(1))).astype(BF16)
    e_end_t = jnp.exp(jnp.broadcast_to(b[rows - 1:rows, :], (SUBLANES, GLA_QK))).T
    row = lax.broadcasted_iota(jnp.int32, (rows, GLA_QK), 0)
    xs = []
    for l in range(1, levels + 1):
        e = jnp.exp(decay_sum(l + 1))
        upper = ((row >> (l - 1)) & 1) == 1
        xs.append((jnp.where(upper, q, k) * e).astype(BF16))
    qb = q.astype(BF16)
    kb = k.astype(BF16)
    for h in range(GLA_HEADS):
        ks = slice(h * GLA_DK, (h + 1) * GLA_DK)
        vs = slice(h * GLA_DV, (h + 1) * GLA_DV)
        att = m_ref[0] * _nt(qb[:, ks], kb[:, ks])
        for l in range(1, levels + 1):
            x = xs[l - 1][:, ks]
            att += m_ref[l] * _nt(x, x)
        vh = v_ref[:, vs].astype(BF16)
        s = s_ref[0, h]
        o = _dot(qe[:, ks], s.astype(BF16)) + _dot(att.astype(BF16), vh)
        s_ref[0, h] = s * e_end_t[ks, 0:1] + _tn(kd[:, ks], vh)
        ya_ref[:, vs] = _gla_finish(o, g_ref[:, vs], gn_ref[:, vs]).astype(ya_ref.dtype)


def _gla_prompt(proj, bsz, length, w_lr, b_lr, gn):
    nchunk = length // CHUNK
    a_mats, masks, levels = _gla_level_constants(CHUNK)
    row = lambda b, c: b * nchunk + c
    return pl.pallas_call(
        functools.partial(_gla_prompt_body, levels=levels),
        grid=(bsz, nchunk),
        in_specs=[
            pl.BlockSpec((CHUNK, GLA_QK), lambda b, c: (row(b, c), C_Q // GLA_QK)),
            pl.BlockSpec((CHUNK, GLA_QK), lambda b, c: (row(b, c), C_K // GLA_QK)),
            pl.BlockSpec((CHUNK, GLA_V), lambda b, c: (row(b, c), C_V // GLA_V)),
            pl.BlockSpec((CHUNK, GLA_V), lambda b, c: (row(b, c), C_G // GLA_V)),
            pl.BlockSpec((CHUNK, LANES), lambda b, c: (row(b, c), C_LR // LANES)),
            pl.BlockSpec((LANES, GLA_QK), lambda b, c: (0, 0)),
            pl.BlockSpec((1, GLA_QK), lambda b, c: (0, 0)),
            pl.BlockSpec((1, GLA_V), lambda b, c: (0, 0)),
            pl.BlockSpec(a_mats.shape, lambda b, c: (0, 0, 0)),
            pl.BlockSpec(masks.shape, lambda b, c: (0, 0, 0)),
        ],
        out_specs=[
            pl.BlockSpec((CHUNK, GLA_V), lambda b, c: (row(b, c), 0)),
            pl.BlockSpec((1, GLA_HEADS, GLA_DK, GLA_DV), lambda b, c: (b, 0, 0, 0)),
        ],
        out_shape=[
            jax.ShapeDtypeStruct((bsz * length, GLA_V), BF16),
            jax.ShapeDtypeStruct((bsz, GLA_HEADS, GLA_DK, GLA_DV), F32),
        ],
        compiler_params=_params(("parallel", "arbitrary")),
        name="gla_prompt",
    )(proj, proj, proj, proj, proj, w_lr, b_lr, gn, a_mats, masks)


def _roll_rows(x, d):
    return pltpu.roll(x, d, 0) if d else x


def _gla_sample_body(q_ref, k_ref, v_ref, g_ref, lr_ref, wlr_ref, blr_ref, gn_ref, s0_ref,
                     ya_ref, s_ref):
    n = SAMPLE_LEN
    pos_k = lax.broadcasted_iota(jnp.int32, (SAMPLE_ROWS, GLA_QK), 0) & (n - 1)
    row_v = lax.broadcasted_iota(jnp.int32, (SAMPLE_ROWS, GLA_DV), 0)
    q = q_ref[...] * (GLA_DK ** -0.5)
    k = k_ref[...]
    v = v_ref[...]
    la = _gla_log_decay(lr_ref, wlr_ref, blr_ref)
    back = [_roll_rows(la, d) for d in range(n)]
    win = [None, la]
    for d in range(2, n):
        win.append(win[-1] + back[d - 1])
    b = la
    suffix = jnp.zeros_like(la)
    for d in range(1, n):
        b = b + jnp.where(pos_k >= d, back[d], 0.0)
        suffix = suffix + jnp.where(pos_k < n - d, pltpu.roll(la, SAMPLE_ROWS - d, 0), 0.0)
    eb = jnp.exp(b)
    qe = (q * eb).astype(BF16)
    kd = k * jnp.exp(suffix)
    eb_t = eb.T
    nseq = SAMPLE_ROWS // n
    seq_k = lax.broadcasted_iota(jnp.int32, (SAMPLE_ROWS, GLA_QK), 0) >> SAMPLE_SHIFT
    kd_seq = [jnp.where(seq_k == j, kd, 0.0).astype(BF16) for j in range(nseq)]
    pair = []
    for d in range(n):
        p = q * _roll_rows(k, d)
        if d:
            p = p * jnp.exp(win[d])
        pair.append(p)
    vback = [_roll_rows(v, d) for d in range(n)]
    for h in range(GLA_HEADS):
        ks = slice(h * GLA_DK, (h + 1) * GLA_DK)
        vs = slice(h * GLA_DV, (h + 1) * GLA_DV)
        o = jnp.zeros((SAMPLE_ROWS, GLA_DV), F32)
        for d in range(n):
            score = jnp.sum(pair[d][:, ks], axis=-1, keepdims=True)
            o = o + jnp.where((row_v & (n - 1)) >= d, score * vback[d][:, vs], 0.0)
        vh = v[:, vs].astype(BF16)
        for j in range(nseq):
            s = s0_ref[j, h]
            o = o + jnp.where((row_v >> SAMPLE_SHIFT) == j, _dot(qe[:, ks], s.astype(BF16)), 0.0)
            col = n * j + n - 1
            s_ref[j, h] = s * eb_t[ks, col:col + 1] + _tn(kd_seq[j][:, ks], vh)
        ya_ref[:, vs] = _gla_finish(o, g_ref[:, vs], gn_ref[:, vs]).astype(ya_ref.dtype)


def _gla_sample(proj, state, w_lr, b_lr, gn):
    t = proj.shape[0]
    nseq = SAMPLE_ROWS // SAMPLE_LEN
    return pl.pallas_call(
        _gla_sample_body,
        grid=(t // SAMPLE_ROWS,),
        in_specs=[
            pl.BlockSpec((SAMPLE_ROWS, GLA_QK), lambda i: (i, C_Q // GLA_QK)),
            pl.BlockSpec((SAMPLE_ROWS, GLA_QK), lambda i: (i, C_K // GLA_QK)),
            pl.BlockSpec((SAMPLE_ROWS, GLA_V), lambda i: (i, C_V // GLA_V)),
            pl.BlockSpec((SAMPLE_ROWS, GLA_V), lambda i: (i, C_G // GLA_V)),
            pl.BlockSpec((SAMPLE_ROWS, LANES), lambda i: (i, C_LR // LANES)),
            pl.BlockSpec((LANES, GLA_QK), lambda i: (0, 0)),
            pl.BlockSpec((1, GLA_QK), lambda i: (0, 0)),
            pl.BlockSpec((1, GLA_V), lambda i: (0, 0)),
            pl.BlockSpec((nseq, GLA_HEADS, GLA_DK, GLA_DV), lambda i: (i, 0, 0, 0)),
        ],
        out_specs=[
            pl.BlockSpec((SAMPLE_ROWS, GLA_V), lambda i: (i, 0)),
            pl.BlockSpec((nseq, GLA_HEADS, GLA_DK, GLA_DV), lambda i: (i, 0, 0, 0)),
        ],
        out_shape=[
            jax.ShapeDtypeStruct((t, GLA_V), F32),
            jax.ShapeDtypeStruct(state.shape, F32),
        ],
        compiler_params=_params(("parallel",)),
        name="gla_sample",
    )(proj, proj, proj, proj, proj, w_lr, b_lr, gn, state)


XBC_PART = 2048


def _ssd_gate_norm(y, z, gn):
    yz = y * _silu(z)
    ms = jnp.mean(yz * yz, axis=-1, keepdims=True)
    return yz * lax.rsqrt(ms + EPS) * gn


def _ssd_prompt_body(x1_ref, x2_ref, bc_ref, z1_ref, z2_ref, dt_ref, cw_ref, cb_ref, dtb_ref, alog_ref,
                     dx_ref, gn_ref, e_ref, a_ref, yc_ref, h_ref, tail_scr, ht_scr):
    rows = x1_ref.shape[0]
    step = pl.program_id(1)

    @pl.when(step == 0)
    def _():
        tail_scr[...] = jnp.zeros(tail_scr.shape, F32)
        ht_scr[...] = jnp.zeros(ht_scr.shape, F32)

    row8 = lax.broadcasted_iota(jnp.int32, (SUBLANES, XBC_PART), 0)

    def conv(x_ref, part):
        cs = slice(part * XBC_PART, (part + 1) * XBC_PART)
        x = x_ref[...]
        prev = tail_scr[:, cs]
        acc = cb_ref[:, cs] + x * cw_ref[SSD_CONV - 1:SSD_CONV, cs]
        for d in range(1, SSD_CONV):
            xr = pltpu.roll(x, d, 0)
            first = jnp.where(row8 < d, pltpu.roll(prev, d, 0), xr[:SUBLANES])
            xd = jnp.concatenate([first, xr[SUBLANES:]], axis=0)
            acc = acc + xd * cw_ref[SSD_CONV - 1 - d:SSD_CONV - d, cs]
        tail_scr[:, cs] = x[rows - SUBLANES:, :]
        return _silu(acc)

    xs_halves = [conv(x1_ref, 0), conv(x2_ref, 1)]
    bc = conv(bc_ref, 2)
    half_w = SSD_GROUPS * SSD_STATE
    bmat = bc[:, :half_w]
    cmat = bc[:, half_w:]
    z_halves = [z1_ref, z2_ref]

    dt = _softplus(dt_ref[...] + dtb_ref[...])
    a = dt * (-jnp.exp(alog_ref[...]))
    a2 = jnp.concatenate(_split2(a), axis=1)

    def head_sum(i):
        d = _dot(a_ref[i], a2)
        return d[:, :LANES] + d[:, LANES:]

    cs_in = head_sum(0)
    cs_suf = head_sum(1)
    cs_t = cs_in.T
    dt_t = dt.T
    cs_hi, cs_lo = _split2(cs_in)
    suf_hi, suf_lo = _split2(cs_suf)
    dt_hi, dt_lo = _split2(dt)
    r_i = lax.broadcasted_iota(jnp.int32, (rows, rows), 0)
    c_i = lax.broadcasted_iota(jnp.int32, (rows, rows), 1)
    causal = r_i >= c_i
    lane_head = lax.broadcasted_iota(jnp.int32, (rows, SSD_GROUP_WIDTH), 1) >> HEADDIM_SHIFT
    heads_per_group = SSD_HEADS // SSD_GROUPS
    per_half = SSD_GROUPS // 2
    for g in range(SSD_GROUPS):
        gs = slice(g * SSD_GROUP_WIDTH, (g + 1) * SSD_GROUP_WIDTH)
        ls = slice((g % per_half) * SSD_GROUP_WIDTH, (g % per_half + 1) * SSD_GROUP_WIDTH)
        ns = slice(g * SSD_STATE, (g + 1) * SSD_STATE)
        xg = xs_halves[g // per_half][:, ls]
        eg = e_ref[:, gs]
        csx = _dot(cs_hi, eg) + _dot(cs_lo, eg)
        sufx = _dot(suf_hi, eg) + _dot(suf_lo, eg)
        dtx = _dot(dt_hi, eg) + _dot(dt_lo, eg)
        bg = bmat[:, ns].astype(BF16)
        cg = cmat[:, ns].astype(BF16)
        cb = _nt(cg, bg)
        ws = []
        xb = []
        for r in range(heads_per_group):
            h = g * heads_per_group + r
            dm = cs_in[:, h:h + 1] - cs_t[h:h + 1, :]
            dec = jnp.where(causal, jnp.exp(jnp.minimum(dm, 0.0)), 0.0)
            ws.append((cb * dec * dt_t[h:h + 1, :]).astype(BF16))
            xb.append(jnp.where(lane_head == r, xg, 0.0).astype(BF16))
        y = _dot(jnp.concatenate(ws, axis=1), jnp.concatenate(xb, axis=0))
        ht = ht_scr[g]
        y = y + _dot(cg, ht.astype(BF16)) * jnp.exp(csx)
        y = y + dx_ref[:, gs] * xg
        wx = (jnp.exp(sufx) * dtx * xg).astype(BF16)
        ht_scr[g] = ht * jnp.exp(csx[rows - 1:rows, :]) + _tn(bg, wx)
        z = z_halves[g // per_half][:, ls]
        yc_ref[:, gs] = _ssd_gate_norm(y, z, gn_ref[:, gs]).astype(yc_ref.dtype)

    @pl.when(step == pl.num_programs(1) - 1)
    def _():
        for g in range(SSD_GROUPS):
            h_ref[0, g] = ht_scr[g].T


def _tri_constants(n):
    t = np.arange(n)[:, None]
    j = np.arange(n)[None, :]
    return jnp.asarray(np.stack([j <= t, j > t]), BF16)


def _head_expand_matrix():
    h = np.arange(LANES)[:, None]
    lane = np.arange(SSD_INNER)[None, :]
    return jnp.asarray(h == lane // SSD_HEADDIM, BF16)


def _ssd_prompt(proj, bsz, length, conv_w, conv_b, dt_bias, a_log, d_x, gn):
    nchunk = length // CHUNK
    row = lambda b, c: b * nchunk + c
    cx = C_XBC // XBC_PART
    cz = C_Z // XBC_PART
    tri = _tri_constants(CHUNK)
    e = _head_expand_matrix()
    full = lambda shape: pl.BlockSpec(shape, lambda b, c: (0,) * len(shape))
    return pl.pallas_call(
        _ssd_prompt_body,
        grid=(bsz, nchunk),
        in_specs=[
            pl.BlockSpec((CHUNK, XBC_PART), lambda b, c: (row(b, c), cx)),
            pl.BlockSpec((CHUNK, XBC_PART), lambda b, c: (row(b, c), cx + 1)),
            pl.BlockSpec((CHUNK, XBC_PART), lambda b, c: (row(b, c), cx + 2)),
            pl.BlockSpec((CHUNK, XBC_PART), lambda b, c: (row(b, c), cz)),
            pl.BlockSpec((CHUNK, XBC_PART), lambda b, c: (row(b, c), cz + 1)),
            pl.BlockSpec((CHUNK, LANES), lambda b, c: (row(b, c), C_DT // LANES)),
            full((SSD_CONV, SSD_CONV_DIM)),
            full((1, SSD_CONV_DIM)),
            full((1, LANES)),
            full((1, LANES)),
            full((1, SSD_INNER)),
            full((1, SSD_INNER)),
            full((LANES, SSD_INNER)),
            full(tri.shape),
        ],
        out_specs=[
            pl.BlockSpec((CHUNK, SSD_INNER), lambda b, c: (row(b, c), 0)),
            pl.BlockSpec((1, SSD_GROUPS, SSD_GROUP_WIDTH, SSD_STATE), lambda b, c: (b, 0, 0, 0)),
        ],
        out_shape=[
            jax.ShapeDtypeStruct((bsz * length, SSD_INNER), BF16),
            jax.ShapeDtypeStruct((bsz, SSD_GROUPS, SSD_GROUP_WIDTH, SSD_STATE), F32),
        ],
        scratch_shapes=[
            pltpu.VMEM((SUBLANES, SSD_CONV_DIM), F32),
            pltpu.VMEM((SSD_GROUPS, SSD_STATE, SSD_GROUP_WIDTH), F32),
        ],
        compiler_params=_params(("parallel", "arbitrary")),
        name="ssd_prompt",
    )(proj, proj, proj, proj, proj, proj, conv_w, conv_b, dt_bias, a_log, d_x, gn, e, tri)


def _ssd_sample_body(x1_ref, x2_ref, bc_ref, z1_ref, z2_ref, dt_ref, cw_ref, cb_ref, dtb_ref, alog_ref,
                     dx_ref, gn_ref, e_ref, prev_ref, h0_ref, yc_ref, h_ref):
    n = SAMPLE_LEN
    nseq = SAMPLE_ROWS // n
    pos_p = lax.broadcasted_iota(jnp.int32, (SAMPLE_ROWS, XBC_PART), 0) & (n - 1)

    def conv(x_ref, part):
        cs = slice(part * XBC_PART, (part + 1) * XBC_PART)
        x = x_ref[...]
        prev = prev_ref[0][:, cs]
        acc = cb_ref[:, cs] + x * cw_ref[SSD_CONV - 1:SSD_CONV, cs]
        for d in range(1, SSD_CONV):
            xd = jnp.where(pos_p < d, pltpu.roll(prev, d, 0), pltpu.roll(x, d, 0))
            acc = acc + xd * cw_ref[SSD_CONV - 1 - d:SSD_CONV - d, cs]
        return _silu(acc)

    xs = jnp.concatenate([conv(x1_ref, 0), conv(x2_ref, 1)], axis=1)
    bc = conv(bc_ref, 2)
    half_w = SSD_GROUPS * SSD_STATE
    bmat = bc[:, :half_w]
    cmat = bc[:, half_w:]
    z = jnp.concatenate([z1_ref[...], z2_ref[...]], axis=1)

    dt = _softplus(dt_ref[...] + dtb_ref[...])
    dt_hi, dt_lo = _split2(dt)
    dtx = _dot(dt_hi, e_ref[...]) + _dot(dt_lo, e_ref[...])
    a_hi, a_lo = _split2(-jnp.exp(alog_ref[...]))
    anegx = _dot(jnp.broadcast_to(a_hi, (SUBLANES, LANES)), e_ref[...]) + \
        _dot(jnp.broadcast_to(a_lo, (SUBLANES, LANES)), e_ref[...])
    ax = dtx * anegx
    pos = lax.broadcasted_iota(jnp.int32, (SAMPLE_ROWS, SSD_INNER), 0) & (n - 1)
    seq_g = lax.broadcasted_iota(jnp.int32, (SAMPLE_ROWS, SSD_GROUP_WIDTH), 0) >> SAMPLE_SHIFT
    back = [_roll_rows(ax, d) for d in range(n)]
    win = [None, ax]
    for d in range(2, n):
        win.append(win[-1] + back[d - 1])
    csx = ax
    sufx = jnp.zeros_like(ax)
    for d in range(1, n):
        csx = csx + jnp.where(pos >= d, back[d], 0.0)
        sufx = sufx + jnp.where(pos < n - d, pltpu.roll(ax, SAMPLE_ROWS - d, 0), 0.0)
    ecs = jnp.exp(csx)
    ecs_t = ecs.T
    wx = jnp.exp(sufx) * dtx * xs

    y = dx_ref[...] * xs
    for d in range(n):
        prod = cmat * _roll_rows(bmat, d)
        cbx = jnp.concatenate(
            [jnp.broadcast_to(jnp.sum(prod[:, g * SSD_STATE:(g + 1) * SSD_STATE], axis=-1, keepdims=True),
                              (SAMPLE_ROWS, SSD_GROUP_WIDTH)) for g in range(SSD_GROUPS)], axis=1)
        term = cbx * _roll_rows(dtx, d) * _roll_rows(xs, d)
        if d:
            term = term * jnp.exp(win[d])
        y = y + jnp.where(pos >= d, term, 0.0)

    y_inter = []
    for g in range(SSD_GROUPS):
        gs = slice(g * SSD_GROUP_WIDTH, (g + 1) * SSD_GROUP_WIDTH)
        ns = slice(g * SSD_STATE, (g + 1) * SSD_STATE)
        bg = bmat[:, ns].astype(BF16)
        cg = cmat[:, ns].astype(BF16)
        acc = jnp.zeros((SAMPLE_ROWS, SSD_GROUP_WIDTH), F32)
        for j in range(nseq):
            h0 = h0_ref[j, g]
            acc = acc + jnp.where(seq_g == j, _nt(cg, h0.astype(BF16)), 0.0)
            wxj = jnp.where(seq_g == j, wx[:, gs], 0.0).astype(BF16)
            col = n * j + n - 1
            h_ref[j, g] = h0 * ecs_t[gs, col:col + 1] + _tn(wxj, bg)
        y_inter.append(acc)
    y = y + jnp.concatenate(y_inter, axis=1) * ecs
    for g in range(SSD_GROUPS):
        gs = slice(g * SSD_GROUP_WIDTH, (g + 1) * SSD_GROUP_WIDTH)
        yc_ref[:, gs] = _ssd_gate_norm(y[:, gs], z[:, gs], gn_ref[:, gs]).astype(yc_ref.dtype)


def _ssd_sample(proj, conv_prev, state, conv_w, conv_b, dt_bias, a_log, d_x, gn):
    t = proj.shape[0]
    nseq = SAMPLE_ROWS // SAMPLE_LEN
    cx = C_XBC // XBC_PART
    cz = C_Z // XBC_PART
    e = _head_expand_matrix()
    full = lambda shape: pl.BlockSpec(shape, lambda i: (0,) * len(shape))
    return pl.pallas_call(
        _ssd_sample_body,
        grid=(t // SAMPLE_ROWS,),
        in_specs=[
            pl.BlockSpec((SAMPLE_ROWS, XBC_PART), lambda i: (i, cx)),
            pl.BlockSpec((SAMPLE_ROWS, XBC_PART), lambda i: (i, cx + 1)),
            pl.BlockSpec((SAMPLE_ROWS, XBC_PART), lambda i: (i, cx + 2)),
            pl.BlockSpec((SAMPLE_ROWS, XBC_PART), lambda i: (i, cz)),
            pl.BlockSpec((SAMPLE_ROWS, XBC_PART), lambda i: (i, cz + 1)),
            pl.BlockSpec((SAMPLE_ROWS, LANES), lambda i: (i, C_DT // LANES)),
            full((SSD_CONV, SSD_CONV_DIM)),
            full((1, SSD_CONV_DIM)),
            full((1, LANES)),
            full((1, LANES)),
            full((1, SSD_INNER)),
            full((1, SSD_INNER)),
            full((LANES, SSD_INNER)),
            pl.BlockSpec((1, SAMPLE_ROWS, SSD_CONV_DIM), lambda i: (i, 0, 0)),
            pl.BlockSpec((nseq, SSD_GROUPS, SSD_GROUP_WIDTH, SSD_STATE), lambda i: (i, 0, 0, 0)),
        ],
        out_specs=[
            pl.BlockSpec((SAMPLE_ROWS, SSD_INNER), lambda i: (i, 0)),
            pl.BlockSpec((nseq, SSD_GROUPS, SSD_GROUP_WIDTH, SSD_STATE), lambda i: (i, 0, 0, 0)),
        ],
        out_shape=[
            jax.ShapeDtypeStruct((t, SSD_INNER), F32),
            jax.ShapeDtypeStruct(state.shape, F32),
        ],
        compiler_params=_params(("parallel",)),
        name="ssd_sample",
    )(proj, proj, proj, proj, proj, proj, conv_w, conv_b, dt_bias, a_log, d_x, gn, e, conv_prev, state)


def _pack_w_in(w):
    sizes = [GLA_QK, GLA_QK, GLA_V, GLA_V, GLA_LOWRANK, 2 * CM_WIDTH, SSD_INNER, SSD_CONV_DIM, SSD_HEADS,
             3 * D_MODEL]
    q, k, v, g, lr, uv, z, xbc, dt, gates = jnp.split(w, np.cumsum(sizes)[:-1].tolist(), axis=-1)
    pad = lambda n: jnp.zeros((w.shape[0], n), w.dtype)
    cat = [q, k, v, g, uv, z, xbc, gates, lr, pad(LANES - GLA_LOWRANK), dt, pad(LANES - SSD_HEADS)]
    return jnp.concatenate(cat, axis=-1).astype(BF16)


def _pad_lanes(v, n):
    return jnp.pad(v, (0, n - v.shape[0])).reshape(1, n)


def _layer_weights(i, g_mix, w_in, w_gla_lr, b_gla_lr, g_gla_norm, cm_ln_g, cm_ln_b, cm_ws, cm_bs,
                   ssd_conv_w, ssd_conv_b, ssd_dt_bias, ssd_a_log, ssd_d, g_ssd_norm,
                   w_br_gla, w_br_cm, w_br_ssd, w_o, g_ffn, w_ffn_in, w_ffn_out,
                   g_ple, w_ple_gate, w_ple_proj):
    row = lambda v: v[i].reshape(1, -1)
    n_tile = CM_CHUNK // SAMPLE_LEN
    return dict(
        g_mix=row(g_mix),
        w_in=_pack_w_in(w_in[i]),
        w_lr=jnp.pad(w_gla_lr[i], ((0, LANES - GLA_LOWRANK), (0, 0))).astype(BF16),
        b_lr=row(b_gla_lr),
        g_gla=row(g_gla_norm),
        ln_g=row(cm_ln_g),
        ln_b=row(cm_ln_b),
        ws_prompt=cm_ws[i],
        bs_prompt=cm_bs[i].T,
        ws_sample=jnp.tile(cm_ws[i][:, :SAMPLE_LEN, :SAMPLE_LEN], (1, n_tile, n_tile)),
        bs_sample=jnp.tile(cm_bs[i][:, :SAMPLE_LEN].T, (n_tile, 1)),
        conv_w=ssd_conv_w[i],
        conv_b=row(ssd_conv_b),
        dt_bias=_pad_lanes(ssd_dt_bias[i], LANES),
        a_log=_pad_lanes(ssd_a_log[i], LANES),
        d_x=jnp.repeat(ssd_d[i], SSD_HEADDIM).reshape(1, SSD_INNER),
        g_ssd=row(g_ssd_norm),
        w_br_gla=w_br_gla[i].astype(BF16),
        w_br_cm=w_br_cm[i].astype(BF16),
        w_br_ssd=w_br_ssd[i].astype(BF16),
        w_o=w_o[i].astype(BF16),
        g_ffn=row(g_ffn),
        w_ffn_in=w_ffn_in[i].astype(BF16),
        w_ffn_out=w_ffn_out[i].astype(BF16),
        g_ple=row(g_ple),
        w_ple_gate=w_ple_gate[i].astype(BF16),
        w_ple_proj=w_ple_proj[i].astype(BF16),
    )


def _dense_tail(x, p, proj, ya, yb, yc, w, tm):
    mix = _merge(ya, yb, yc, proj, w["w_br_gla"], w["w_br_cm"], w["w_br_ssd"], tm, 512)
    x = _matmul_residual(mix, w["w_o"], x, tm, 512)
    h = _ffn_in(x, w["g_ffn"], w["w_ffn_in"], tm, 512)
    x = _matmul_residual(h, w["w_ffn_out"], x, tm, 512)
    return _ple(x, w["g_ple"], p, w["w_ple_gate"], w["w_ple_proj"], tm, 512)


def _prompt_layer(x, p, bsz, length, w):
    proj = _norm_matmul(x, w["g_mix"], w["w_in"], 512, 1280)
    ya, s_gla = _gla_prompt(proj, bsz, length, w["w_lr"], w["b_lr"], w["g_gla"])
    yb, _ = _cmlp(proj, w["ln_g"], w["ln_b"], w["ws_prompt"], w["bs_prompt"], CM_CHUNK, False, BF16)
    yc, s_ssm = _ssd_prompt(proj, bsz, length, w["conv_w"], w["conv_b"], w["dt_bias"], w["a_log"],
                            w["d_x"], w["g_ssd"])
    xbc = proj[:, C_XBC:C_XBC + SSD_CONV_DIM].reshape(bsz, length, SSD_CONV_DIM)
    s_conv = xbc[:, length - (SSD_CONV - 1):]
    x = _dense_tail(x, p, proj, ya, yb, yc, w, 512)
    s_ssm = s_ssm.reshape(bsz, SSD_HEADS, SSD_HEADDIM, SSD_STATE)
    return x, s_gla, s_ssm, s_conv


def _sample_layer(x, p, bsz, s_gla, s_ssm, s_conv, w):
    n = SAMPLE_LEN
    proj = _norm_matmul(x, w["g_mix"], w["w_in"], 512, 1280)
    ya, s_gla_new = _gla_sample(proj, s_gla, w["w_lr"], w["b_lr"], w["g_gla"])
    yb, v_rows = _cmlp(proj, w["ln_g"], w["ln_b"], w["ws_sample"], w["bs_sample"], n, True, F32)
    sc = s_conv.reshape(bsz // 2, 2, SSD_CONV - 1, SSD_CONV_DIM)
    zrow = jnp.zeros((bsz // 2, 1, SSD_CONV_DIM), F32)
    conv_prev = jnp.concatenate([zrow, sc[:, 1], zrow, sc[:, 0]], axis=1)
    h0 = s_ssm.reshape(bsz, SSD_GROUPS, SSD_GROUP_WIDTH, SSD_STATE)
    yc, s_ssm_new = _ssd_sample(proj, conv_prev, h0, w["conv_w"], w["conv_b"], w["dt_bias"], w["a_log"],
                                w["d_x"], w["g_ssd"])
    xbc = proj[:, C_XBC:C_XBC + SSD_CONV_DIM].reshape(bsz, n, SSD_CONV_DIM)
    s_conv_new = xbc[:, n - (SSD_CONV - 1):]
    x = _dense_tail(x, p, proj, ya, yb, yc, w, 512)
    s_ssm_new = s_ssm_new.reshape(bsz, SSD_HEADS, SSD_HEADDIM, SSD_STATE)
    return x, s_gla_new, s_ssm_new, s_conv_new, v_rows.reshape(bsz, n, CM_WIDTH)


def kernel(x_prompt, x_sample, state_gla, state_ssm, state_conv, p_prompt, p_sample, g_mix, w_in, w_gla_lr, b_gla_lr, g_gla_norm, cm_ln_g, cm_ln_b, cm_ws, cm_bs, ssd_conv_w, ssd_conv_b, ssd_dt_bias, ssd_a_log, ssd_d, g_ssd_norm, w_br_gla, w_br_cm, w_br_ssd, w_o, g_ffn, w_ffn_in, w_ffn_out, g_ple, w_ple_gate, w_ple_proj, g_final):
    weights = (g_mix, w_in, w_gla_lr, b_gla_lr, g_gla_norm, cm_ln_g, cm_ln_b, cm_ws, cm_bs,
               ssd_conv_w, ssd_conv_b, ssd_dt_bias, ssd_a_log, ssd_d, g_ssd_norm,
               w_br_gla, w_br_cm, w_br_ssd, w_o, g_ffn, w_ffn_in, w_ffn_out,
               g_ple, w_ple_gate, w_ple_proj)
    bp, lp, _ = x_prompt.shape
    bs, ls, _ = x_sample.shape
    assert ls == SAMPLE_LEN and lp % CHUNK == 0 and bs % 2 == 0
    xp = x_prompt.reshape(bp * lp, D_MODEL)
    xs = x_sample.reshape(bs * ls, D_MODEL)
    gla_p, ssm_p, conv_p = [], [], []
    gla_s, ssm_s, conv_s, v_s = [], [], [], []
    for i in range(DEPTH):
        w = _layer_weights(i, *weights)
        xp, sg, sm, sc = _prompt_layer(xp, p_prompt[i].reshape(bp * lp, PLE_DIM), bp, lp, w)
        gla_p.append(sg)
        ssm_p.append(sm)
        conv_p.append(sc)
        xs, sg, sm, sc, vr = _sample_layer(xs, p_sample[i].reshape(bs * ls, PLE_DIM), bs,
                                           state_gla[i], state_ssm[i], state_conv[i], w)
        gla_s.append(sg)
        ssm_s.append(sm)
        conv_s.append(sc)
        v_s.append(vr)
    gf = g_final.reshape(1, D_MODEL)
    y_prompt = _rmsnorm(xp, gf, 512).reshape(bp, lp, D_MODEL)
    y_sample = _rmsnorm(xs, gf, 512).reshape(bs, ls, D_MODEL)
    return (y_prompt, y_sample, jnp.stack(gla_p), jnp.stack(gla_s), jnp.stack(ssm_p), jnp.stack(ssm_s),
            jnp.stack(conv_p), jnp.stack(conv_s), jnp.stack(v_s))
```

```python
import functools

import numpy as np
import jax
import jax.numpy as jnp
from jax import lax
from jax.experimental import pallas as pl
from jax.experimental.pallas import tpu as pltpu

F32 = jnp.float32
BF16 = jnp.bfloat16

D_MODEL = 2048
DEPTH = 2
GLA_HEADS = 4
GLA_DK = 256
GLA_DV = 512
GLA_QK = 1024
GLA_V = 2048
GLA_LOWRANK = 16
GLA_TAU = 16.0
CM_GROUPS = 8
CM_WIDTH = 2048
CM_GROUP_DIM = 256
CM_CHUNK = 128
SSD_INNER = 4096
SSD_HEADDIM = 64
SSD_HEADS = 64
SSD_GROUPS = 8
SSD_STATE = 128
SSD_CONV = 4
SSD_CONV_DIM = 6144
SSD_GROUP_WIDTH = SSD_INNER // SSD_GROUPS
D_FF = 5632
PLE_DIM = 256
EPS = 1e-6

LANES = 128
SUBLANES = 8
VMEM_LIMIT = 52 * 1024 * 1024

C_Q = 0
C_K = 1024
C_V = 2048
C_G = 4096
C_UV = 6144
C_Z = 10240
C_XBC = 14336
C_GATE = 20480
C_LR = 26624
SMALL_LR = 0
SMALL_DT = 1

CHUNK = 128
SAMPLE_ROWS = 8
SAMPLE_LEN = 4
SAMPLE_SHIFT = 2
HEADDIM_SHIFT = 6


def _params(sem):
    return pltpu.CompilerParams(dimension_semantics=sem, vmem_limit_bytes=VMEM_LIMIT)


def _nt(a, b):
    return lax.dot_general(a, b, (((1,), (1,)), ((), ())), preferred_element_type=F32)


def _tn(a, b):
    return lax.dot_general(a, b, (((0,), (0,)), ((), ())), preferred_element_type=F32)


def _dot(a, b):
    return jnp.dot(a, b, preferred_element_type=F32)


def _split2(x):
    hi = x.astype(BF16)
    lo = (x - hi.astype(F32)).astype(BF16)
    return hi, lo


def _softplus(x):
    return jnp.maximum(x, 0.0) + jnp.log1p(jnp.exp(-jnp.abs(x)))


def _log_sigmoid(x):
    return jnp.minimum(x, 0.0) - jnp.log1p(jnp.exp(-jnp.abs(x)))


def _silu(x):
    return x * jax.nn.sigmoid(x)


def _rms_cast_body(x_ref, g_ref, o_ref):
    x = x_ref[...]
    ms = jnp.mean(x * x, axis=-1, keepdims=True)
    o_ref[...] = (x * lax.rsqrt(ms + EPS) * g_ref[...]).astype(o_ref.dtype)


def _rms_cast(x, g, tm, dtype):
    t, k = x.shape
    return pl.pallas_call(
        _rms_cast_body,
        grid=(t // tm,),
        in_specs=[pl.BlockSpec((tm, k), lambda i: (i, 0)), pl.BlockSpec((1, k), lambda i: (0, 0))],
        out_specs=pl.BlockSpec((tm, k), lambda i: (i, 0)),
        out_shape=jax.ShapeDtypeStruct((t, k), dtype),
        compiler_params=_params(("parallel",)),
        name="rms_cast",
    )(x, g)


NATIVE_UV = 2 * GLA_QK + 2 * GLA_V + GLA_LOWRANK
NATIVE_DT = NATIVE_UV + 2 * CM_WIDTH + SSD_INNER + SSD_CONV_DIM
NATIVE_GATE = NATIVE_DT + SSD_HEADS
PROJ_TN = 1024
W_ROW_CHUNK = 256


def _proj_body(n_ref, w_ref, wn_ref, o_ref, w_scr):
    j = pl.program_id(0)
    k = w_ref.shape[0]

    def load_weights(shift):
        for r in range(0, k, W_ROW_CHUNK):
            rs = slice(r, r + W_ROW_CHUNK)
            if shift:
                w = jnp.concatenate([w_ref[rs, :], wn_ref[rs, :]], axis=1)[:, shift:shift + PROJ_TN]
            else:
                w = w_ref[rs, :]
            w_scr[rs, :] = w.astype(BF16)

    @pl.when(pl.program_id(1) == 0)
    def _():
        j_uv = C_UV // PROJ_TN
        j_gate = C_GATE // PROJ_TN
        pl.when(j < j_uv)(lambda: load_weights(0))
        pl.when((j >= j_uv) & (j < j_gate))(lambda: load_weights(NATIVE_UV - C_UV))
        pl.when(j >= j_gate)(lambda: load_weights(NATIVE_GATE - C_GATE))

    o_ref[...] = _dot(n_ref[...], w_scr[...])


def _proj_matmul(n, w_in, layer, tm):
    t, k = n.shape
    tn = PROJ_TN
    return pl.pallas_call(
        _proj_body,
        grid=(C_LR // tn, t // tm),
        in_specs=[
            pl.BlockSpec((tm, k), lambda j, i: (i, 0)),
            pl.BlockSpec((None, k, tn), lambda j, i: (layer, 0, j)),
            pl.BlockSpec((None, k, LANES), lambda j, i: (layer, 0, (j + 1) * (tn // LANES))),
        ],
        out_specs=pl.BlockSpec((tm, tn), lambda j, i: (i, j)),
        out_shape=jax.ShapeDtypeStruct((t, C_LR), F32),
        scratch_shapes=[pltpu.VMEM((k, tn), BF16)],
        compiler_params=_params(("parallel", "arbitrary")),
        name="proj_matmul",
    )(n, w_in, w_in)


def _mm_body(a_ref, w_ref, o_ref):
    o_ref[...] = _dot(a_ref[...], w_ref[...])


def _matmul_small(a, w, tm):
    t, k = a.shape
    n = w.shape[1]
    return pl.pallas_call(
        _mm_body,
        grid=(t // tm,),
        in_specs=[pl.BlockSpec((tm, k), lambda i: (i, 0)), pl.BlockSpec((k, n), lambda i: (0, 0))],
        out_specs=pl.BlockSpec((tm, n), lambda i: (i, 0)),
        out_shape=jax.ShapeDtypeStruct((t, n), F32),
        compiler_params=_params(("parallel",)),
        name="matmul_small",
    )(a, w)


def _ffn_in_body(n_ref, wg_ref, wu_ref, o_ref, wg_scr, wu_scr):
    @pl.when(pl.program_id(1) == 0)
    def _():
        for r in range(0, wg_ref.shape[0], W_ROW_CHUNK):
            rs = slice(r, r + W_ROW_CHUNK)
            wg_scr[rs, :] = wg_ref[rs, :].astype(BF16)
            wu_scr[rs, :] = wu_ref[rs, :].astype(BF16)

    n = n_ref[...]
    gate = _dot(n, wg_scr[...])
    up = _dot(n, wu_scr[...])
    o_ref[...] = (_silu(gate) * up).astype(o_ref.dtype)


def _ffn_in(n, w, layer, tm, tn):
    t, k = n.shape
    nblk = D_FF // tn
    return pl.pallas_call(
        _ffn_in_body,
        grid=(nblk, t // tm),
        in_specs=[
            pl.BlockSpec((tm, k), lambda j, i: (i, 0)),
            pl.BlockSpec((None, k, tn), lambda j, i: (layer, 0, j)),
            pl.BlockSpec((None, k, tn), lambda j, i: (layer, 0, j + nblk)),
        ],
        out_specs=pl.BlockSpec((tm, tn), lambda j, i: (i, j)),
        out_shape=jax.ShapeDtypeStruct((t, D_FF), BF16),
        scratch_shapes=[pltpu.VMEM((k, tn), BF16), pltpu.VMEM((k, tn), BF16)],
        compiler_params=_params(("parallel", "arbitrary")),
        name="ffn_in",
    )(n, w, w)


def _mm_res_body(a_ref, w_ref, r_ref, o_ref):
    o_ref[...] = r_ref[...] + _dot(a_ref[...].astype(BF16), w_ref[...])


def _matmul_residual(a, w, res, tm, tn):
    t, k = a.shape
    n = w.shape[1]
    return pl.pallas_call(
        _mm_res_body,
        grid=(t // tm, n // tn),
        in_specs=[
            pl.BlockSpec((tm, k), lambda i, j: (i, 0)),
            pl.BlockSpec((k, tn), lambda i, j: (0, j)),
            pl.BlockSpec((tm, tn), lambda i, j: (i, j)),
        ],
        out_specs=pl.BlockSpec((tm, tn), lambda i, j: (i, j)),
        out_shape=jax.ShapeDtypeStruct((t, n), F32),
        compiler_params=_params(("parallel", "parallel")),
        name="matmul_residual",
    )(a, w, res)


def _merge_body(ya_ref, yb_ref, yc_ref, ga_ref, gb_ref, gc_ref, wa_ref, wb_ref, wc_ref, o_ref):
    acc = jax.nn.sigmoid(ga_ref[...]) * _dot(ya_ref[...].astype(BF16), wa_ref[...])
    acc += jax.nn.sigmoid(gb_ref[...]) * _dot(yb_ref[...].astype(BF16), wb_ref[...])
    acc += jax.nn.sigmoid(gc_ref[...]) * _dot(yc_ref[...].astype(BF16), wc_ref[...])
    o_ref[...] = acc.astype(o_ref.dtype)


def _merge(ya, yb, yc, proj, wa, wb, wc, tm, tn):
    t = ya.shape[0]
    nblk = D_MODEL // tn
    g0 = C_GATE // tn
    return pl.pallas_call(
        _merge_body,
        grid=(t // tm, nblk),
        in_specs=[
            pl.BlockSpec((tm, GLA_V), lambda i, j: (i, 0)),
            pl.BlockSpec((tm, CM_WIDTH), lambda i, j: (i, 0)),
            pl.BlockSpec((tm, SSD_INNER), lambda i, j: (i, 0)),
            pl.BlockSpec((tm, tn), lambda i, j: (i, g0 + j)),
            pl.BlockSpec((tm, tn), lambda i, j: (i, g0 + nblk + j)),
            pl.BlockSpec((tm, tn), lambda i, j: (i, g0 + 2 * nblk + j)),
            pl.BlockSpec((GLA_V, tn), lambda i, j: (0, j)),
            pl.BlockSpec((CM_WIDTH, tn), lambda i, j: (0, j)),
            pl.BlockSpec((SSD_INNER, tn), lambda i, j: (0, j)),
        ],
        out_specs=pl.BlockSpec((tm, tn), lambda i, j: (i, j)),
        out_shape=jax.ShapeDtypeStruct((t, D_MODEL), BF16),
        compiler_params=_params(("parallel", "parallel")),
        name="merge",
    )(ya, yb, yc, proj, proj, proj, wa, wb, wc)


def _ple_body(x_ref, xr_ref, g_ref, p_ref, wg_ref, wp_ref, o_ref, n_scr):
    @pl.when(pl.program_id(1) == 0)
    def _():
        x = x_ref[...]
        ms = jnp.mean(x * x, axis=-1, keepdims=True)
        n_scr[...] = (x * lax.rsqrt(ms + EPS) * g_ref[...]).astype(BF16)

    gate = jax.nn.sigmoid(_dot(n_scr[...], wg_ref[...]))
    emb = _dot(p_ref[...].astype(BF16), wp_ref[...])
    o_ref[...] = xr_ref[...] + gate * emb


def _ple(x, g, p, wg, wp, tm, tn):
    t = x.shape[0]
    return pl.pallas_call(
        _ple_body,
        grid=(t // tm, D_MODEL // tn),
        in_specs=[
            pl.BlockSpec((tm, D_MODEL), lambda i, j: (i, 0)),
            pl.BlockSpec((tm, tn), lambda i, j: (i, j)),
            pl.BlockSpec((1, D_MODEL), lambda i, j: (0, 0)),
            pl.BlockSpec((tm, PLE_DIM), lambda i, j: (i, 0)),
            pl.BlockSpec((D_MODEL, tn), lambda i, j: (0, j)),
            pl.BlockSpec((PLE_DIM, tn), lambda i, j: (0, j)),
        ],
        out_specs=pl.BlockSpec((tm, tn), lambda i, j: (i, j)),
        out_shape=jax.ShapeDtypeStruct((t, D_MODEL), F32),
        scratch_shapes=[pltpu.VMEM((tm, D_MODEL), BF16)],
        compiler_params=_params(("parallel", "arbitrary")),
        name="ple",
    )(x, x, g, p, wg, wp)


def _rmsnorm_body(x_ref, g_ref, o_ref):
    x = x_ref[...]
    ms = jnp.mean(x * x, axis=-1, keepdims=True)
    o_ref[...] = x * lax.rsqrt(ms + EPS) * g_ref[...]


def _rmsnorm(x, g, tm):
    t, k = x.shape
    return pl.pallas_call(
        _rmsnorm_body,
        grid=(t // tm,),
        in_specs=[pl.BlockSpec((tm, k), lambda i: (i, 0)), pl.BlockSpec((1, k), lambda i: (0, 0))],
        out_specs=pl.BlockSpec((tm, k), lambda i: (i, 0)),
        out_shape=jax.ShapeDtypeStruct((t, k), F32),
        compiler_params=_params(("parallel",)),
        name="final_rmsnorm",
    )(x, g)


def _cmlp_body(u_ref, v_ref, lng_ref, lnb_ref, ws_ref, bst_ref, yb_ref, *rest, seq_len):
    u = jax.nn.gelu(u_ref[...])
    v = jax.nn.gelu(v_ref[...])
    mu = jnp.mean(v, axis=-1, keepdims=True)
    vc = v - mu
    var = jnp.mean(vc * vc, axis=-1, keepdims=True)
    vn = vc * lax.rsqrt(var + EPS) * lng_ref[...] + lnb_ref[...]
    if rest:
        rest[0][...] = vn
    r = lax.broadcasted_iota(jnp.int32, (CM_CHUNK, CM_CHUNK), 0)
    c = lax.broadcasted_iota(jnp.int32, (CM_CHUNK, CM_CHUNK), 1)
    if seq_len >= CM_CHUNK:
        keep = r >= c
    else:
        sh = seq_len.bit_length() - 1
        keep = ((r >> sh) == (c >> sh)) & ((r & (seq_len - 1)) >= (c & (seq_len - 1)))
    for g in range(CM_GROUPS):
        sl = slice(g * CM_GROUP_DIM, (g + 1) * CM_GROUP_DIM)
        w = jnp.where(keep, ws_ref[g], 0.0).astype(BF16)
        mixed = _dot(w, vn[:, sl].astype(BF16)) + bst_ref[:, g:g + 1]
        yb_ref[:, sl] = (u[:, sl] * mixed).astype(yb_ref.dtype)


def _cmlp(proj, ln_g, ln_b, ws_tiled, bs_t, seq_len, emit_v, out_dtype):
    t = proj.shape[0]
    out_shape = [jax.ShapeDtypeStruct((t, CM_WIDTH), out_dtype)]
    out_specs = [pl.BlockSpec((CM_CHUNK, CM_WIDTH), lambda i: (i, 0))]
    if emit_v:
        out_shape.append(jax.ShapeDtypeStruct((t, CM_WIDTH), F32))
        out_specs.append(pl.BlockSpec((CM_CHUNK, CM_WIDTH), lambda i: (i, 0)))
    cu = C_UV // CM_WIDTH
    res = pl.pallas_call(
        functools.partial(_cmlp_body, seq_len=seq_len),
        grid=(t // CM_CHUNK,),
        in_specs=[
            pl.BlockSpec((CM_CHUNK, CM_WIDTH), lambda i: (i, cu)),
            pl.BlockSpec((CM_CHUNK, CM_WIDTH), lambda i: (i, cu + 1)),
            pl.BlockSpec((1, CM_WIDTH), lambda i: (0, 0)),
            pl.BlockSpec((1, CM_WIDTH), lambda i: (0, 0)),
            pl.BlockSpec((CM_GROUPS, CM_CHUNK, CM_CHUNK), lambda i: (0, 0, 0)),
            pl.BlockSpec((CM_CHUNK, CM_GROUPS), lambda i: (0, 0)),
        ],
        out_specs=out_specs,
        out_shape=out_shape,
        compiler_params=_params(("parallel",)),
        name="chunk_mlp",
    )(proj, proj, ln_g, ln_b, ws_tiled, bs_t)
    return (res[0], res[1]) if emit_v else (res[0], None)


def _gla_level_constants(n):
    levels = n.bit_length() - 1
    t = np.arange(n)[:, None]
    j = np.arange(n)[None, :]
    sums = [j <= t, j > t]
    masks = [t == j]
    for l in range(1, levels + 1):
        w, half = 1 << l, 1 << (l - 1)
        start = (t >> l) << l
        upper = ((t >> (l - 1)) & 1) == 1
        a_up = (j >= start + half) & (j <= t)
        a_lo = (j > t) & (j < start + half)
        sums.append(np.where(upper, a_up, a_lo))
        jj = j
        upper_t = ((t >> (l - 1)) & 1) == 1
        lower_s = ((jj >> (l - 1)) & 1) == 0
        masks.append(((t >> l) == (jj >> l)) & upper_t & lower_s)
    return (jnp.asarray(np.stack(sums), BF16), jnp.asarray(np.stack(masks), F32), levels)


def _gla_log_decay(lr_ref, wlr_ref, blr_ref):
    x = _dot(lr_ref[...].astype(BF16), wlr_ref[...]) + blr_ref[...]
    return _log_sigmoid(x) * (1.0 / GLA_TAU)


def _gla_finish(o, g, gn):
    ms = jnp.mean(o * o, axis=-1, keepdims=True)
    return o * lax.rsqrt(ms + EPS) * gn * _silu(g)


def _gla_prompt_body(q_ref, k_ref, v_ref, g_ref, lr_ref, wlr_ref, blr_ref, gn_ref, a_ref, m_ref,
                     ya_ref, s_ref, *, levels):
    rows = q_ref.shape[0]

    @pl.when(pl.program_id(1) == 0)
    def _():
        s_ref[...] = jnp.zeros(s_ref.shape, F32)

    q = q_ref[...] * (GLA_DK ** -0.5)
    k = k_ref[...]
    la = _gla_log_decay(lr_ref, wlr_ref, blr_ref)
    la2 = jnp.concatenate(_split2(la), axis=1)

    def decay_sum(i):
        d = _dot(a_ref[i], la2)
        return d[:, :GLA_QK] + d[:, GLA_QK:]

    b = decay_sum(0)
    qe = (q * jnp.exp(b)).astype(BF16)
    kd = (k * jnp.exp(decay_sum(1))).astype(BF16)
    e_end_t = jnp.exp(jnp.broadcast_to(b[rows - 1:rows, :], (SUBLANES, GLA_QK))).T
    row = lax.broadcasted_iota(jnp.int32, (rows, GLA_QK), 0)
    xs = []
    for l in range(1, levels + 1):
        e = jnp.exp(decay_sum(l + 1))
        upper = ((row >> (l - 1)) & 1) == 1
        xs.append((jnp.where(upper, q, k) * e).astype(BF16))
    qb = q.astype(BF16)
    kb = k.astype(BF16)
    for h in range(GLA_HEADS):
        ks = slice(h * GLA_DK, (h + 1) * GLA_DK)
        vs = slice(h * GLA_DV, (h + 1) * GLA_DV)
        att = m_ref[0] * _nt(qb[:, ks], kb[:, ks])
        for l in range(1, levels + 1):
            x = xs[l - 1][:, ks]
            att += m_ref[l] * _nt(x, x)
        vh = v_ref[:, vs].astype(BF16)
        s = s_ref[0, h]
        o = _dot(qe[:, ks], s.astype(BF16)) + _dot(att.astype(BF16), vh)
        s_ref[0, h] = s * e_end_t[ks, 0:1] + _tn(kd[:, ks], vh)
        ya_ref[:, vs] = _gla_finish(o, g_ref[:, vs], gn_ref[:, vs]).astype(ya_ref.dtype)


def _gla_prompt(proj, small, bsz, length, w_lr, b_lr, gn):
    nchunk = length // CHUNK
    a_mats, masks, levels = _gla_level_constants(CHUNK)
    row = lambda b, c: b * nchunk + c
    return pl.pallas_call(
        functools.partial(_gla_prompt_body, levels=levels),
        grid=(bsz, nchunk),
        in_specs=[
            pl.BlockSpec((CHUNK, GLA_QK), lambda b, c: (row(b, c), C_Q // GLA_QK)),
            pl.BlockSpec((CHUNK, GLA_QK), lambda b, c: (row(b, c), C_K // GLA_QK)),
            pl.BlockSpec((CHUNK, GLA_V), lambda b, c: (row(b, c), C_V // GLA_V)),
            pl.BlockSpec((CHUNK, GLA_V), lambda b, c: (row(b, c), C_G // GLA_V)),
            pl.BlockSpec((CHUNK, LANES), lambda b, c: (row(b, c), SMALL_LR)),
            pl.BlockSpec((LANES, GLA_QK), lambda b, c: (0, 0)),
            pl.BlockSpec((1, GLA_QK), lambda b, c: (0, 0)),
            pl.BlockSpec((1, GLA_V), lambda b, c: (0, 0)),
            pl.BlockSpec(a_mats.shape, lambda b, c: (0, 0, 0)),
            pl.BlockSpec(masks.shape, lambda b, c: (0, 0, 0)),
        ],
        out_specs=[
            pl.BlockSpec((CHUNK, GLA_V), lambda b, c: (row(b, c), 0)),
            pl.BlockSpec((1, GLA_HEADS, GLA_DK, GLA_DV), lambda b, c: (b, 0, 0, 0)),
        ],
        out_shape=[
            jax.ShapeDtypeStruct((bsz * length, GLA_V), BF16),
            jax.ShapeDtypeStruct((bsz, GLA_HEADS, GLA_DK, GLA_DV), F32),
        ],
        compiler_params=_params(("parallel", "arbitrary")),
        name="gla_prompt",
    )(proj, proj, proj, proj, small, w_lr, b_lr, gn, a_mats, masks)


def _roll_rows(x, d):
    return pltpu.roll(x, d, 0) if d else x


def _gla_sample_body(q_ref, k_ref, v_ref, g_ref, lr_ref, wlr_ref, blr_ref, gn_ref, s0_ref,
                     ya_ref, s_ref):
    n = SAMPLE_LEN
    pos_k = lax.broadcasted_iota(jnp.int32, (SAMPLE_ROWS, GLA_QK), 0) & (n - 1)
    row_v = lax.broadcasted_iota(jnp.int32, (SAMPLE_ROWS, GLA_DV), 0)
    q = q_ref[...] * (GLA_DK ** -0.5)
    k = k_ref[...]
    v = v_ref[...]
    la = _gla_log_decay(lr_ref, wlr_ref, blr_ref)
    back = [_roll_rows(la, d) for d in range(n)]
    win = [None, la]
    for d in range(2, n):
        win.append(win[-1] + back[d - 1])
    b = la
    suffix = jnp.zeros_like(la)
    for d in range(1, n):
        b = b + jnp.where(pos_k >= d, back[d], 0.0)
        suffix = suffix + jnp.where(pos_k < n - d, pltpu.roll(la, SAMPLE_ROWS - d, 0), 0.0)
    eb = jnp.exp(b)
    qe = (q * eb).astype(BF16)
    kd = k * jnp.exp(suffix)
    eb_t = eb.T
    nseq = SAMPLE_ROWS // n
    seq_k = lax.broadcasted_iota(jnp.int32, (SAMPLE_ROWS, GLA_QK), 0) >> SAMPLE_SHIFT
    kd_seq = [jnp.where(seq_k == j, kd, 0.0).astype(BF16) for j in range(nseq)]
    pair = []
    for d in range(n):
        p = q * _roll_rows(k, d)
        if d:
            p = p * jnp.exp(win[d])
        pair.append(p)
    vback = [_roll_rows(v, d) for d in range(n)]
    for h in range(GLA_HEADS):
        ks = slice(h * GLA_DK, (h + 1) * GLA_DK)
        vs = slice(h * GLA_DV, (h + 1) * GLA_DV)
        o = jnp.zeros((SAMPLE_ROWS, GLA_DV), F32)
        for d in range(n):
            score = jnp.sum(pair[d][:, ks], axis=-1, keepdims=True)
            o = o + jnp.where((row_v & (n - 1)) >= d, score * vback[d][:, vs], 0.0)
        vh = v[:, vs].astype(BF16)
        for j in range(nseq):
            s = s0_ref[j, h]
            o = o + jnp.where((row_v >> SAMPLE_SHIFT) == j, _dot(qe[:, ks], s.astype(BF16)), 0.0)
            col = n * j + n - 1
            s_ref[j, h] = s * eb_t[ks, col:col + 1] + _tn(kd_seq[j][:, ks], vh)
        ya_ref[:, vs] = _gla_finish(o, g_ref[:, vs], gn_ref[:, vs]).astype(ya_ref.dtype)


def _stacked_state_call(body, prev_out, n_in):
    if prev_out is None:
        return body, [], [], {}
    wrapped = lambda *refs: body(*refs[:n_in], *refs[n_in + 1:])
    return wrapped, [pl.BlockSpec(memory_space=pl.ANY)], [prev_out], {n_in: 1}


def _gla_sample(proj, small, state_all, layer, prev_out, w_lr, b_lr, gn):
    t = proj.shape[0]
    nseq = SAMPLE_ROWS // SAMPLE_LEN
    state_spec = pl.BlockSpec((None, nseq, GLA_HEADS, GLA_DK, GLA_DV), lambda i: (layer, i, 0, 0, 0))
    body, extra_specs, extra_args, aliases = _stacked_state_call(_gla_sample_body, prev_out, 9)
    return pl.pallas_call(
        body,
        grid=(t // SAMPLE_ROWS,),
        in_specs=[
            pl.BlockSpec((SAMPLE_ROWS, GLA_QK), lambda i: (i, C_Q // GLA_QK)),
            pl.BlockSpec((SAMPLE_ROWS, GLA_QK), lambda i: (i, C_K // GLA_QK)),
            pl.BlockSpec((SAMPLE_ROWS, GLA_V), lambda i: (i, C_V // GLA_V)),
            pl.BlockSpec((SAMPLE_ROWS, GLA_V), lambda i: (i, C_G // GLA_V)),
            pl.BlockSpec((SAMPLE_ROWS, LANES), lambda i: (i, SMALL_LR)),
            pl.BlockSpec((LANES, GLA_QK), lambda i: (0, 0)),
            pl.BlockSpec((1, GLA_QK), lambda i: (0, 0)),
            pl.BlockSpec((1, GLA_V), lambda i: (0, 0)),
            state_spec,
        ] + extra_specs,
        out_specs=[pl.BlockSpec((SAMPLE_ROWS, GLA_V), lambda i: (i, 0)), state_spec],
        out_shape=[
            jax.ShapeDtypeStruct((t, GLA_V), F32),
            jax.ShapeDtypeStruct(state_all.shape, F32),
        ],
        input_output_aliases=aliases,
        compiler_params=_params(("parallel",)),
        name="gla_sample",
    )(proj, proj, proj, proj, small, w_lr, b_lr, gn, state_all, *extra_args)


XBC_PART = 2048


def _ssd_gate_norm(y, z, gn):
    yz = y * _silu(z)
    ms = jnp.mean(yz * yz, axis=-1, keepdims=True)
    return yz * lax.rsqrt(ms + EPS) * gn


def _ssd_prompt_body(x1_ref, x2_ref, bc_ref, z1_ref, z2_ref, dt_ref, cw_ref, cb_ref, dtb_ref, alog_ref,
                     dx_ref, gn_ref, e_ref, a_ref, yc_ref, h_ref, tail_scr, ht_scr):
    rows = x1_ref.shape[0]
    step = pl.program_id(1)

    @pl.when(step == 0)
    def _():
        tail_scr[...] = jnp.zeros(tail_scr.shape, F32)
        ht_scr[...] = jnp.zeros(ht_scr.shape, F32)

    row8 = lax.broadcasted_iota(jnp.int32, (SUBLANES, XBC_PART), 0)

    def conv(x_ref, part):
        cs = slice(part * XBC_PART, (part + 1) * XBC_PART)
        x = x_ref[...]
        prev = tail_scr[:, cs]
        acc = cb_ref[:, cs] + x * cw_ref[SSD_CONV - 1:SSD_CONV, cs]
        for d in range(1, SSD_CONV):
            xr = pltpu.roll(x, d, 0)
            first = jnp.where(row8 < d, pltpu.roll(prev, d, 0), xr[:SUBLANES])
            xd = jnp.concatenate([first, xr[SUBLANES:]], axis=0)
            acc = acc + xd * cw_ref[SSD_CONV - 1 - d:SSD_CONV - d, cs]
        tail_scr[:, cs] = x[rows - SUBLANES:, :]
        return _silu(acc)

    xs_halves = [conv(x1_ref, 0), conv(x2_ref, 1)]
    bc = conv(bc_ref, 2)
    half_w = SSD_GROUPS * SSD_STATE
    bmat = bc[:, :half_w]
    cmat = bc[:, half_w:]
    z_halves = [z1_ref, z2_ref]

    dt = _softplus(dt_ref[...] + dtb_ref[...])
    a = dt * (-jnp.exp(alog_ref[...]))
    a2 = jnp.concatenate(_split2(a), axis=1)

    def head_sum(i):
        d = _dot(a_ref[i], a2)
        return d[:, :LANES] + d[:, LANES:]

    cs_in = head_sum(0)
    cs_suf = head_sum(1)
    cs_t = cs_in.T
    dt_t = dt.T
    cs_hi, cs_lo = _split2(cs_in)
    suf_hi, suf_lo = _split2(cs_suf)
    dt_hi, dt_lo = _split2(dt)
    r_i = lax.broadcasted_iota(jnp.int32, (rows, rows), 0)
    c_i = lax.broadcasted_iota(jnp.int32, (rows, rows), 1)
    causal = r_i >= c_i
    lane_head = lax.broadcasted_iota(jnp.int32, (rows, SSD_GROUP_WIDTH), 1) >> HEADDIM_SHIFT
    heads_per_group = SSD_HEADS // SSD_GROUPS
    per_half = SSD_GROUPS // 2
    for g in range(SSD_GROUPS):
        gs = slice(g * SSD_GROUP_WIDTH, (g + 1) * SSD_GROUP_WIDTH)
        ls = slice((g % per_half) * SSD_GROUP_WIDTH, (g % per_half + 1) * SSD_GROUP_WIDTH)
        ns = slice(g * SSD_STATE, (g + 1) * SSD_STATE)
        xg = xs_halves[g // per_half][:, ls]
        eg = e_ref[:, gs]
        csx = _dot(cs_hi, eg) + _dot(cs_lo, eg)
        sufx = _dot(suf_hi, eg) + _dot(suf_lo, eg)
        dtx = _dot(dt_hi, eg) + _dot(dt_lo, eg)
        bg = bmat[:, ns].astype(BF16)
        cg = cmat[:, ns].astype(BF16)
        cb = _nt(cg, bg)
        ws = []
        xb = []
        for r in range(heads_per_group):
            h = g * heads_per_group + r
            dm = cs_in[:, h:h + 1] - cs_t[h:h + 1, :]
            dec = jnp.where(causal, jnp.exp(jnp.minimum(dm, 0.0)), 0.0)
            ws.append((cb * dec * dt_t[h:h + 1, :]).astype(BF16))
            xb.append(jnp.where(lane_head == r, xg, 0.0).astype(BF16))
        y = _dot(jnp.concatenate(ws, axis=1), jnp.concatenate(xb, axis=0))
        ht = ht_scr[g]
        y = y + _dot(cg, ht.astype(BF16)) * jnp.exp(csx)
        y = y + dx_ref[:, gs] * xg
        wx = (jnp.exp(sufx) * dtx * xg).astype(BF16)
        ht_scr[g] = ht * jnp.exp(csx[rows - 1:rows, :]) + _tn(bg, wx)
        z = z_halves[g // per_half][:, ls]
        yc_ref[:, gs] = _ssd_gate_norm(y, z, gn_ref[:, gs]).astype(yc_ref.dtype)

    @pl.when(step == pl.num_programs(1) - 1)
    def _():
        for g in range(SSD_GROUPS):
            h_ref[0, g] = ht_scr[g].T


def _tri_constants(n):
    t = np.arange(n)[:, None]
    j = np.arange(n)[None, :]
    return jnp.asarray(np.stack([j <= t, j > t]), BF16)


def _head_expand_matrix():
    h = np.arange(LANES)[:, None]
    lane = np.arange(SSD_INNER)[None, :]
    return jnp.asarray(h == lane // SSD_HEADDIM, BF16)


def _ssd_prompt(proj, small, bsz, length, conv_w, conv_b, dt_bias, a_log, d_x, gn):
    nchunk = length // CHUNK
    row = lambda b, c: b * nchunk + c
    cx = C_XBC // XBC_PART
    cz = C_Z // XBC_PART
    tri = _tri_constants(CHUNK)
    e = _head_expand_matrix()
    full = lambda shape: pl.BlockSpec(shape, lambda b, c: (0,) * len(shape))
    return pl.pallas_call(
        _ssd_prompt_body,
        grid=(bsz, nchunk),
        in_specs=[
            pl.BlockSpec((CHUNK, XBC_PART), lambda b, c: (row(b, c), cx)),
            pl.BlockSpec((CHUNK, XBC_PART), lambda b, c: (row(b, c), cx + 1)),
            pl.BlockSpec((CHUNK, XBC_PART), lambda b, c: (row(b, c), cx + 2)),
            pl.BlockSpec((CHUNK, XBC_PART), lambda b, c: (row(b, c), cz)),
            pl.BlockSpec((CHUNK, XBC_PART), lambda b, c: (row(b, c), cz + 1)),
            pl.BlockSpec((CHUNK, LANES), lambda b, c: (row(b, c), SMALL_DT)),
            full((SSD_CONV, SSD_CONV_DIM)),
            full((1, SSD_CONV_DIM)),
            full((1, LANES)),
            full((1, LANES)),
            full((1, SSD_INNER)),
            full((1, SSD_INNER)),
            full((LANES, SSD_INNER)),
            full(tri.shape),
        ],
        out_specs=[
            pl.BlockSpec((CHUNK, SSD_INNER), lambda b, c: (row(b, c), 0)),
            pl.BlockSpec((1, SSD_GROUPS, SSD_GROUP_WIDTH, SSD_STATE), lambda b, c: (b, 0, 0, 0)),
        ],
        out_shape=[
            jax.ShapeDtypeStruct((bsz * length, SSD_INNER), BF16),
            jax.ShapeDtypeStruct((bsz, SSD_GROUPS, SSD_GROUP_WIDTH, SSD_STATE), F32),
        ],
        scratch_shapes=[
            pltpu.VMEM((SUBLANES, SSD_CONV_DIM), F32),
            pltpu.VMEM((SSD_GROUPS, SSD_STATE, SSD_GROUP_WIDTH), F32),
        ],
        compiler_params=_params(("parallel", "arbitrary")),
        name="ssd_prompt",
    )(proj, proj, proj, proj, proj, small, conv_w, conv_b, dt_bias, a_log, d_x, gn, e, tri)


def _ssd_sample_body(x1_ref, x2_ref, bc_ref, z1_ref, z2_ref, dt_ref, cw_ref, cb_ref, dtb_ref, alog_ref,
                     dx_ref, gn_ref, e_ref, prev_ref, h0_ref, yc_ref, h_ref):
    n = SAMPLE_LEN
    nseq = SAMPLE_ROWS // n
    pos_p = lax.broadcasted_iota(jnp.int32, (SAMPLE_ROWS, XBC_PART), 0) & (n - 1)

    def conv(x_ref, part):
        cs = slice(part * XBC_PART, (part + 1) * XBC_PART)
        x = x_ref[...]
        prev = prev_ref[0][:, cs]
        acc = cb_ref[:, cs] + x * cw_ref[SSD_CONV - 1:SSD_CONV, cs]
        for d in range(1, SSD_CONV):
            xd = jnp.where(pos_p < d, pltpu.roll(prev, d, 0), pltpu.roll(x, d, 0))
            acc = acc + xd * cw_ref[SSD_CONV - 1 - d:SSD_CONV - d, cs]
        return _silu(acc)

    xs = jnp.concatenate([conv(x1_ref, 0), conv(x2_ref, 1)], axis=1)
    bc = conv(bc_ref, 2)
    half_w = SSD_GROUPS * SSD_STATE
    bmat = bc[:, :half_w]
    cmat = bc[:, half_w:]
    z = jnp.concatenate([z1_ref[...], z2_ref[...]], axis=1)

    dt = _softplus(dt_ref[...] + dtb_ref[...])
    dt_hi, dt_lo = _split2(dt)
    dtx = _dot(dt_hi, e_ref[...]) + _dot(dt_lo, e_ref[...])
    a_hi, a_lo = _split2(-jnp.exp(alog_ref[...]))
    anegx = _dot(jnp.broadcast_to(a_hi, (SUBLANES, LANES)), e_ref[...]) + \
        _dot(jnp.broadcast_to(a_lo, (SUBLANES, LANES)), e_ref[...])
    ax = dtx * anegx
    pos = lax.broadcasted_iota(jnp.int32, (SAMPLE_ROWS, SSD_INNER), 0) & (n - 1)
    seq_g = lax.broadcasted_iota(jnp.int32, (SAMPLE_ROWS, SSD_GROUP_WIDTH), 0) >> SAMPLE_SHIFT
    back = [_roll_rows(ax, d) for d in range(n)]
    win = [None, ax]
    for d in range(2, n):
        win.append(win[-1] + back[d - 1])
    csx = ax
    sufx = jnp.zeros_like(ax)
    for d in range(1, n):
        csx = csx + jnp.where(pos >= d, back[d], 0.0)
        sufx = sufx + jnp.where(pos < n - d, pltpu.roll(ax, SAMPLE_ROWS - d, 0), 0.0)
    ecs = jnp.exp(csx)
    ecs_t = ecs.T
    wx = jnp.exp(sufx) * dtx * xs

    y = dx_ref[...] * xs
    for d in range(n):
        prod = cmat * _roll_rows(bmat, d)
        cbx = jnp.concatenate(
            [jnp.broadcast_to(jnp.sum(prod[:, g * SSD_STATE:(g + 1) * SSD_STATE], axis=-1, keepdims=True),
                              (SAMPLE_ROWS, SSD_GROUP_WIDTH)) for g in range(SSD_GROUPS)], axis=1)
        term = cbx * _roll_rows(dtx, d) * _roll_rows(xs, d)
        if d:
            term = term * jnp.exp(win[d])
        y = y + jnp.where(pos >= d, term, 0.0)

    y_inter = []
    for g in range(SSD_GROUPS):
        gs = slice(g * SSD_GROUP_WIDTH, (g + 1) * SSD_GROUP_WIDTH)
        ns = slice(g * SSD_STATE, (g + 1) * SSD_STATE)
        bg = bmat[:, ns].astype(BF16)
        cg = cmat[:, ns].astype(BF16)
        acc = jnp.zeros((SAMPLE_ROWS, SSD_GROUP_WIDTH), F32)
        for j in range(nseq):
            h0 = h0_ref[j, g]
            acc = acc + jnp.where(seq_g == j, _nt(cg, h0.astype(BF16)), 0.0)
            wxj = jnp.where(seq_g == j, wx[:, gs], 0.0).astype(BF16)
            col = n * j + n - 1
            h_ref[j, g] = h0 * ecs_t[gs, col:col + 1] + _tn(wxj, bg)
        y_inter.append(acc)
    y = y + jnp.concatenate(y_inter, axis=1) * ecs
    for g in range(SSD_GROUPS):
        gs = slice(g * SSD_GROUP_WIDTH, (g + 1) * SSD_GROUP_WIDTH)
        yc_ref[:, gs] = _ssd_gate_norm(y[:, gs], z[:, gs], gn_ref[:, gs]).astype(yc_ref.dtype)


def _ssd_sample(proj, small, conv_prev, state_all, layer, prev_out, conv_w, conv_b, dt_bias, a_log, d_x, gn):
    t = proj.shape[0]
    nseq = SAMPLE_ROWS // SAMPLE_LEN
    cx = C_XBC // XBC_PART
    cz = C_Z // XBC_PART
    e = _head_expand_matrix()
    full = lambda shape: pl.BlockSpec(shape, lambda i: (0,) * len(shape))
    state_spec = pl.BlockSpec((None, nseq, SSD_GROUPS, SSD_GROUP_WIDTH, SSD_STATE), lambda i: (layer, i, 0, 0, 0))
    body, extra_specs, extra_args, aliases = _stacked_state_call(_ssd_sample_body, prev_out, 15)
    return pl.pallas_call(
        body,
        grid=(t // SAMPLE_ROWS,),
        in_specs=[
            pl.BlockSpec((SAMPLE_ROWS, XBC_PART), lambda i: (i, cx)),
            pl.BlockSpec((SAMPLE_ROWS, XBC_PART), lambda i: (i, cx + 1)),
            pl.BlockSpec((SAMPLE_ROWS, XBC_PART), lambda i: (i, cx + 2)),
            pl.BlockSpec((SAMPLE_ROWS, XBC_PART), lambda i: (i, cz)),
            pl.BlockSpec((SAMPLE_ROWS, XBC_PART), lambda i: (i, cz + 1)),
            pl.BlockSpec((SAMPLE_ROWS, LANES), lambda i: (i, SMALL_DT)),
            full((SSD_CONV, SSD_CONV_DIM)),
            full((1, SSD_CONV_DIM)),
            full((1, LANES)),
            full((1, LANES)),
            full((1, SSD_INNER)),
            full((1, SSD_INNER)),
            full((LANES, SSD_INNER)),
            pl.BlockSpec((1, SAMPLE_ROWS, SSD_CONV_DIM), lambda i: (i, 0, 0)),
            state_spec,
        ] + extra_specs,
        out_specs=[pl.BlockSpec((SAMPLE_ROWS, SSD_INNER), lambda i: (i, 0)), state_spec],
        out_shape=[
            jax.ShapeDtypeStruct((t, SSD_INNER), F32),
            jax.ShapeDtypeStruct(state_all.shape, F32),
        ],
        input_output_aliases=aliases,
        compiler_params=_params(("parallel",)),
        name="ssd_sample",
    )(proj, proj, proj, proj, proj, small, conv_w, conv_b, dt_bias, a_log, d_x, gn, e, conv_prev, state_all,
      *extra_args)


def _narrow_w_in(w):
    lr = w[:, NATIVE_UV - GLA_LOWRANK:NATIVE_UV]
    dt = w[:, NATIVE_DT:NATIVE_GATE]
    pad = lambda n: jnp.zeros((w.shape[0], n), w.dtype)
    return jnp.concatenate([lr, pad(LANES - GLA_LOWRANK), dt, pad(LANES - SSD_HEADS)], axis=-1).astype(BF16)


def _pad_lanes(v, n):
    return jnp.pad(v, (0, n - v.shape[0])).reshape(1, n)


def _layer_weights(i, g_mix, w_in, w_gla_lr, b_gla_lr, g_gla_norm, cm_ln_g, cm_ln_b, cm_ws, cm_bs,
                   ssd_conv_w, ssd_conv_b, ssd_dt_bias, ssd_a_log, ssd_d, g_ssd_norm,
                   w_br_gla, w_br_cm, w_br_ssd, w_o, g_ffn, w_ffn_in, w_ffn_out,
                   g_ple, w_ple_gate, w_ple_proj):
    row = lambda v: v[i].reshape(1, -1)
    n_tile = CM_CHUNK // SAMPLE_LEN
    return dict(
        g_mix=row(g_mix),
        w_in=w_in,
        w_in_narrow=_narrow_w_in(w_in[i]),
        layer=i,
        w_lr=jnp.pad(w_gla_lr[i], ((0, LANES - GLA_LOWRANK), (0, 0))).astype(BF16),
        b_lr=row(b_gla_lr),
        g_gla=row(g_gla_norm),
        ln_g=row(cm_ln_g),
        ln_b=row(cm_ln_b),
        ws_prompt=cm_ws[i],
        bs_prompt=cm_bs[i].T,
        ws_sample=jnp.tile(cm_ws[i][:, :SAMPLE_LEN, :SAMPLE_LEN], (1, n_tile, n_tile)),
        bs_sample=jnp.tile(cm_bs[i][:, :SAMPLE_LEN].T, (n_tile, 1)),
        conv_w=ssd_conv_w[i],
        conv_b=row(ssd_conv_b),
        dt_bias=_pad_lanes(ssd_dt_bias[i], LANES),
        a_log=_pad_lanes(ssd_a_log[i], LANES),
        d_x=jnp.repeat(ssd_d[i], SSD_HEADDIM).reshape(1, SSD_INNER),
        g_ssd=row(g_ssd_norm),
        w_br_gla=w_br_gla[i].astype(BF16),
        w_br_cm=w_br_cm[i].astype(BF16),
        w_br_ssd=w_br_ssd[i].astype(BF16),
        w_o=w_o[i].astype(BF16),
        g_ffn=row(g_ffn),
        w_ffn_in=w_ffn_in,
        w_ffn_out=w_ffn_out[i].astype(BF16),
        g_ple=row(g_ple),
        w_ple_gate=w_ple_gate[i].astype(BF16),
        w_ple_proj=w_ple_proj[i].astype(BF16),
    )


def _dense_tail(x, p, proj, ya, yb, yc, w, tm_ws):
    tm = 512
    mix = _merge(ya, yb, yc, proj, w["w_br_gla"], w["w_br_cm"], w["w_br_ssd"], tm, 512)
    x = _matmul_residual(mix, w["w_o"], x, tm, 512)
    n = _rms_cast(x, w["g_ffn"], 512, BF16)
    h = _ffn_in(n, w["w_ffn_in"], w["layer"], tm_ws, 512)
    x = _matmul_residual(h, w["w_ffn_out"], x, tm, 512)
    return _ple(x, w["g_ple"], p, w["w_ple_gate"], w["w_ple_proj"], tm, 512)


def _input_projection(x, w, tm):
    n = _rms_cast(x, w["g_mix"], 512, BF16)
    return _proj_matmul(n, w["w_in"], w["layer"], tm), _matmul_small(n, w["w_in_narrow"], 512)


def _last_conv_rows(proj, bsz, length):
    rows = proj.reshape(bsz, length, proj.shape[-1])[:, length - (SSD_CONV - 1):]
    return rows[:, :, C_XBC:C_XBC + SSD_CONV_DIM]


def _prompt_layer(x, p, bsz, length, w):
    proj, small = _input_projection(x, w, 1024)
    ya, s_gla = _gla_prompt(proj, small, bsz, length, w["w_lr"], w["b_lr"], w["g_gla"])
    yb, _ = _cmlp(proj, w["ln_g"], w["ln_b"], w["ws_prompt"], w["bs_prompt"], CM_CHUNK, False, BF16)
    yc, s_ssm = _ssd_prompt(proj, small, bsz, length, w["conv_w"], w["conv_b"], w["dt_bias"], w["a_log"],
                            w["d_x"], w["g_ssd"])
    s_conv = _last_conv_rows(proj, bsz, length)
    x = _dense_tail(x, p, proj, ya, yb, yc, w, 1024)
    s_ssm = s_ssm.reshape(bsz, SSD_HEADS, SSD_HEADDIM, SSD_STATE)
    return x, s_gla, s_ssm, s_conv


def _sample_layer(x, p, bsz, gla_all, ssm_all, s_conv, prev_gla, prev_ssm, w):
    n = SAMPLE_LEN
    proj, small = _input_projection(x, w, 512)
    ya, gla_out = _gla_sample(proj, small, gla_all, w["layer"], prev_gla, w["w_lr"], w["b_lr"], w["g_gla"])
    yb, v_rows = _cmlp(proj, w["ln_g"], w["ln_b"], w["ws_sample"], w["bs_sample"], n, True, F32)
    sc = s_conv.reshape(bsz // 2, 2, SSD_CONV - 1, SSD_CONV_DIM)
    zrow = jnp.zeros((bsz // 2, 1, SSD_CONV_DIM), F32)
    conv_prev = jnp.concatenate([zrow, sc[:, 1], zrow, sc[:, 0]], axis=1)
    yc, ssm_out = _ssd_sample(proj, small, conv_prev, ssm_all, w["layer"], prev_ssm, w["conv_w"], w["conv_b"],
                              w["dt_bias"], w["a_log"], w["d_x"], w["g_ssd"])
    s_conv_new = _last_conv_rows(proj, bsz, n)
    x = _dense_tail(x, p, proj, ya, yb, yc, w, 512)
    return x, gla_out, ssm_out, s_conv_new, v_rows.reshape(bsz, n, CM_WIDTH)


def kernel(x_prompt, x_sample, state_gla, state_ssm, state_conv, p_prompt, p_sample, g_mix, w_in, w_gla_lr, b_gla_lr, g_gla_norm, cm_ln_g, cm_ln_b, cm_ws, cm_bs, ssd_conv_w, ssd_conv_b, ssd_dt_bias, ssd_a_log, ssd_d, g_ssd_norm, w_br_gla, w_br_cm, w_br_ssd, w_o, g_ffn, w_ffn_in, w_ffn_out, g_ple, w_ple_gate, w_ple_proj, g_final):
    weights = (g_mix, w_in, w_gla_lr, b_gla_lr, g_gla_norm, cm_ln_g, cm_ln_b, cm_ws, cm_bs,
               ssd_conv_w, ssd_conv_b, ssd_dt_bias, ssd_a_log, ssd_d, g_ssd_norm,
               w_br_gla, w_br_cm, w_br_ssd, w_o, g_ffn, w_ffn_in, w_ffn_out,
               g_ple, w_ple_gate, w_ple_proj)
    bp, lp, _ = x_prompt.shape
    bs, ls, _ = x_sample.shape
    assert ls == SAMPLE_LEN and lp % CHUNK == 0 and bs % 2 == 0
    xp = x_prompt.reshape(bp * lp, D_MODEL)
    xs = x_sample.reshape(bs * ls, D_MODEL)
    gla_p, ssm_p, conv_p = [], [], []
    conv_s, v_s = [], []
    ssm_all = state_ssm.reshape(DEPTH, bs, SSD_GROUPS, SSD_GROUP_WIDTH, SSD_STATE)
    gla_s = ssm_s = None
    for i in range(DEPTH):
        w = _layer_weights(i, *weights)
        xp, sg, sm, sc = _prompt_layer(xp, p_prompt[i].reshape(bp * lp, PLE_DIM), bp, lp, w)
        gla_p.append(sg)
        ssm_p.append(sm)
        conv_p.append(sc)
        xs, gla_s, ssm_s, sc, vr = _sample_layer(xs, p_sample[i].reshape(bs * ls, PLE_DIM), bs,
                                                 state_gla, ssm_all, state_conv[i], gla_s, ssm_s, w)
        conv_s.append(sc)
        v_s.append(vr)
    gf = g_final.reshape(1, D_MODEL)
    y_prompt = _rmsnorm(xp, gf, 512).reshape(bp, lp, D_MODEL)
    y_sample = _rmsnorm(xs, gf, 512).reshape(bs, ls, D_MODEL)
    ssm_s = ssm_s.reshape(DEPTH, bs, SSD_HEADS, SSD_HEADDIM, SSD_STATE)
    return (y_prompt, y_sample, jnp.stack(gla_p), gla_s, jnp.stack(ssm_p), ssm_s,
            jnp.stack(conv_p), jnp.stack(conv_s), jnp.stack(v_s))
```

```python
import functools

import numpy as np
import jax
import jax.numpy as jnp
from jax import lax
from jax.experimental import pallas as pl
from jax.experimental.pallas import tpu as pltpu

F32 = jnp.float32
BF16 = jnp.bfloat16

D_MODEL = 2048
DEPTH = 2
GLA_HEADS = 4
GLA_DK = 256
GLA_DV = 512
GLA_QK = 1024
GLA_V = 2048
GLA_LOWRANK = 16
GLA_TAU = 16.0
CM_GROUPS = 8
CM_WIDTH = 2048
CM_GROUP_DIM = 256
CM_CHUNK = 128
SSD_INNER = 4096
SSD_HEADDIM = 64
SSD_HEADS = 64
SSD_GROUPS = 8
SSD_STATE = 128
SSD_CONV = 4
SSD_CONV_DIM = 6144
SSD_GROUP_WIDTH = SSD_INNER // SSD_GROUPS
D_FF = 5632
PLE_DIM = 256
EPS = 1e-6

LANES = 128
SUBLANES = 8
VMEM_LIMIT = 52 * 1024 * 1024

C_Q = 0
C_K = 1024
C_V = 2048
C_G = 4096
C_UV = 6144
C_Z = 10240
C_XBC = 14336
C_GATE = 20480
C_LR = 26624
SMALL_LR = 0
SMALL_DT = 1

CHUNK = 128
SAMPLE_ROWS = 8
SAMPLE_LEN = 4
SAMPLE_SHIFT = 2
HEADDIM_SHIFT = 6


def _params(sem):
    return pltpu.CompilerParams(dimension_semantics=sem, vmem_limit_bytes=VMEM_LIMIT)


def _nt(a, b):
    return lax.dot_general(a, b, (((1,), (1,)), ((), ())), preferred_element_type=F32)


def _tn(a, b):
    return lax.dot_general(a, b, (((0,), (0,)), ((), ())), preferred_element_type=F32)


def _dot(a, b):
    return jnp.dot(a, b, preferred_element_type=F32)


def _split2(x):
    hi = x.astype(BF16)
    lo = (x - hi.astype(F32)).astype(BF16)
    return hi, lo


def _softplus(x):
    return jnp.maximum(x, 0.0) + jnp.log1p(jnp.exp(-jnp.abs(x)))


def _log_sigmoid(x):
    return jnp.minimum(x, 0.0) - jnp.log1p(jnp.exp(-jnp.abs(x)))


def _silu(x):
    return x * jax.nn.sigmoid(x)


def _rms_cast_body(x_ref, g_ref, o_ref):
    x = x_ref[...]
    ms = jnp.mean(x * x, axis=-1, keepdims=True)
    o_ref[...] = (x * lax.rsqrt(ms + EPS) * g_ref[...]).astype(o_ref.dtype)


def _rms_cast(x, g, tm, dtype):
    t, k = x.shape
    return pl.pallas_call(
        _rms_cast_body,
        grid=(t // tm,),
        in_specs=[pl.BlockSpec((tm, k), lambda i: (i, 0)), pl.BlockSpec((1, k), lambda i: (0, 0))],
        out_specs=pl.BlockSpec((tm, k), lambda i: (i, 0)),
        out_shape=jax.ShapeDtypeStruct((t, k), dtype),
        compiler_params=_params(("parallel",)),
        name="rms_cast",
    )(x, g)


NATIVE_UV = 2 * GLA_QK + 2 * GLA_V + GLA_LOWRANK
NATIVE_DT = NATIVE_UV + 2 * CM_WIDTH + SSD_INNER + SSD_CONV_DIM
NATIVE_GATE = NATIVE_DT + SSD_HEADS
PROJ_TN = 1024
W_ROW_CHUNK = 256


def _proj_body(n_ref, w_ref, wn_ref, o_ref, w_scr):
    j = pl.program_id(0)
    tn = w_ref.shape[0]

    def load_weights(shift):
        for r in range(0, tn - shift, W_ROW_CHUNK):
            rows = min(W_ROW_CHUNK, tn - shift - r)
            w_scr[r:r + rows, :] = w_ref[r + shift:r + shift + rows, :].astype(BF16)
        if shift:
            w_scr[tn - shift:tn, :] = wn_ref[0:shift, :].astype(BF16)

    @pl.when(pl.program_id(1) == 0)
    def _():
        j_uv = C_UV // PROJ_TN
        j_gate = C_GATE // PROJ_TN
        pl.when(j < j_uv)(lambda: load_weights(0))
        pl.when((j >= j_uv) & (j < j_gate))(lambda: load_weights(NATIVE_UV - C_UV))
        pl.when(j >= j_gate)(lambda: load_weights(NATIVE_GATE - C_GATE))

    o_ref[...] = _nt(n_ref[...], w_scr[...])


def _proj_matmul(n, w_in_t, layer, tm):
    t, k = n.shape
    tn = PROJ_TN
    return pl.pallas_call(
        _proj_body,
        grid=(C_LR // tn, t // tm),
        in_specs=[
            pl.BlockSpec((tm, k), lambda j, i: (i, 0)),
            pl.BlockSpec((None, tn, k), lambda j, i: (layer, j, 0)),
            pl.BlockSpec((None, LANES, k), lambda j, i: (layer, (j + 1) * (tn // LANES), 0)),
        ],
        out_specs=pl.BlockSpec((tm, tn), lambda j, i: (i, j)),
        out_shape=jax.ShapeDtypeStruct((t, C_LR), F32),
        scratch_shapes=[pltpu.VMEM((tn, k), BF16)],
        compiler_params=_params(("parallel", "arbitrary")),
        name="proj_matmul",
    )(n, w_in_t, w_in_t)


def _mm_nt_body(a_ref, w_ref, o_ref):
    o_ref[...] = _nt(a_ref[...], w_ref[...].astype(BF16))


def _matmul_small_nt(a, w_t, tm):
    t, k = a.shape
    n = w_t.shape[0]
    return pl.pallas_call(
        _mm_nt_body,
        grid=(t // tm,),
        in_specs=[pl.BlockSpec((tm, k), lambda i: (i, 0)), pl.BlockSpec((n, k), lambda i: (0, 0))],
        out_specs=pl.BlockSpec((tm, n), lambda i: (i, 0)),
        out_shape=jax.ShapeDtypeStruct((t, n), F32),
        compiler_params=_params(("parallel",)),
        name="matmul_small",
    )(a, w_t)


def _ffn_in_body(n_ref, wg_ref, wu_ref, o_ref, wg_scr, wu_scr):
    @pl.when(pl.program_id(1) == 0)
    def _():
        for r in range(0, wg_ref.shape[0], W_ROW_CHUNK):
            rs = slice(r, r + W_ROW_CHUNK)
            wg_scr[rs, :] = wg_ref[rs, :].astype(BF16)
            wu_scr[rs, :] = wu_ref[rs, :].astype(BF16)

    n = n_ref[...]
    gate = _dot(n, wg_scr[...])
    up = _dot(n, wu_scr[...])
    o_ref[...] = (_silu(gate) * up).astype(o_ref.dtype)


def _ffn_in(n, w, layer, tm, tn):
    t, k = n.shape
    nblk = D_FF // tn
    return pl.pallas_call(
        _ffn_in_body,
        grid=(nblk, t // tm),
        in_specs=[
            pl.BlockSpec((tm, k), lambda j, i: (i, 0)),
            pl.BlockSpec((None, k, tn), lambda j, i: (layer, 0, j)),
            pl.BlockSpec((None, k, tn), lambda j, i: (layer, 0, j + nblk)),
        ],
        out_specs=pl.BlockSpec((tm, tn), lambda j, i: (i, j)),
        out_shape=jax.ShapeDtypeStruct((t, D_FF), BF16),
        scratch_shapes=[pltpu.VMEM((k, tn), BF16), pltpu.VMEM((k, tn), BF16)],
        compiler_params=_params(("parallel", "arbitrary")),
        name="ffn_in",
    )(n, w, w)


def _mm_res_body(a_ref, w_ref, r_ref, o_ref):
    o_ref[...] = r_ref[...] + _dot(a_ref[...].astype(BF16), w_ref[...])


def _matmul_residual(a, w, res, tm, tn):
    t, k = a.shape
    n = w.shape[1]
    return pl.pallas_call(
        _mm_res_body,
        grid=(t // tm, n // tn),
        in_specs=[
            pl.BlockSpec((tm, k), lambda i, j: (i, 0)),
            pl.BlockSpec((k, tn), lambda i, j: (0, j)),
            pl.BlockSpec((tm, tn), lambda i, j: (i, j)),
        ],
        out_specs=pl.BlockSpec((tm, tn), lambda i, j: (i, j)),
        out_shape=jax.ShapeDtypeStruct((t, n), F32),
        compiler_params=_params(("parallel", "parallel")),
        name="matmul_residual",
    )(a, w, res)


def _merge_body(ya_ref, yb_ref, yc_ref, ga_ref, gb_ref, gc_ref, wa_ref, wb_ref, wc_ref, o_ref):
    acc = jax.nn.sigmoid(ga_ref[...]) * _dot(ya_ref[...].astype(BF16), wa_ref[...])
    acc += jax.nn.sigmoid(gb_ref[...]) * _dot(yb_ref[...].astype(BF16), wb_ref[...])
    acc += jax.nn.sigmoid(gc_ref[...]) * _dot(yc_ref[...].astype(BF16), wc_ref[...])
    o_ref[...] = acc.astype(o_ref.dtype)


def _merge(ya, yb, yc, proj, wa, wb, wc, tm, tn):
    t = ya.shape[0]
    nblk = D_MODEL // tn
    g0 = C_GATE // tn
    return pl.pallas_call(
        _merge_body,
        grid=(t // tm, nblk),
        in_specs=[
            pl.BlockSpec((tm, GLA_V), lambda i, j: (i, 0)),
            pl.BlockSpec((tm, CM_WIDTH), lambda i, j: (i, 0)),
            pl.BlockSpec((tm, SSD_INNER), lambda i, j: (i, 0)),
            pl.BlockSpec((tm, tn), lambda i, j: (i, g0 + j)),
            pl.BlockSpec((tm, tn), lambda i, j: (i, g0 + nblk + j)),
            pl.BlockSpec((tm, tn), lambda i, j: (i, g0 + 2 * nblk + j)),
            pl.BlockSpec((GLA_V, tn), lambda i, j: (0, j)),
            pl.BlockSpec((CM_WIDTH, tn), lambda i, j: (0, j)),
            pl.BlockSpec((SSD_INNER, tn), lambda i, j: (0, j)),
        ],
        out_specs=pl.BlockSpec((tm, tn), lambda i, j: (i, j)),
        out_shape=jax.ShapeDtypeStruct((t, D_MODEL), BF16),
        compiler_params=_params(("parallel", "parallel")),
        name="merge",
    )(ya, yb, yc, proj, proj, proj, wa, wb, wc)


def _rms(x, g):
    ms = jnp.mean(x * x, axis=-1, keepdims=True)
    return x * lax.rsqrt(ms + EPS) * g


def _out_proj_body(mix_ref, w_ref, x_ref, g_ref, x1_ref, n_ref):
    x1 = x_ref[...] + _dot(mix_ref[...], w_ref[...])
    x1_ref[...] = x1
    n_ref[...] = _rms(x1, g_ref[...]).astype(n_ref.dtype)


def _out_proj(mix, w, x, g_next, tm):
    t, k = mix.shape
    row_block = lambda width: pl.BlockSpec((tm, width), lambda i: (i, 0))
    return pl.pallas_call(
        _out_proj_body,
        grid=(t // tm,),
        in_specs=[
            row_block(k),
            pl.BlockSpec((k, D_MODEL), lambda i: (0, 0)),
            row_block(D_MODEL),
            pl.BlockSpec((1, D_MODEL), lambda i: (0, 0)),
        ],
        out_specs=[row_block(D_MODEL), row_block(D_MODEL)],
        out_shape=[jax.ShapeDtypeStruct((t, D_MODEL), F32), jax.ShapeDtypeStruct((t, D_MODEL), BF16)],
        compiler_params=_params(("parallel",)),
        name="out_proj",
    )(mix, w, x, g_next)


def _ple_body(x_ref, g_ref, p_ref, wg_ref, wp_ref, gn_ref, *out_refs, final):
    x = x_ref[...]
    n = _rms(x, g_ref[...]).astype(BF16)
    gate = jax.nn.sigmoid(_dot(n, wg_ref[...]))
    emb = _dot(p_ref[...].astype(BF16), wp_ref[...])
    y = x + gate * emb
    if final:
        out_refs[0][...] = _rms(y, gn_ref[...])
    else:
        out_refs[0][...] = y
        out_refs[1][...] = _rms(y, gn_ref[...]).astype(BF16)


def _ple(x, g, p, wg, wp, g_next, final, tm):
    t = x.shape[0]
    row_block = lambda width: pl.BlockSpec((tm, width), lambda i: (i, 0))
    whole = lambda a: pl.BlockSpec(a.shape, lambda i: (0, 0))
    if final:
        out_specs = [row_block(D_MODEL)]
        out_shape = [jax.ShapeDtypeStruct((t, D_MODEL), F32)]
    else:
        out_specs = [row_block(D_MODEL), row_block(D_MODEL)]
        out_shape = [jax.ShapeDtypeStruct((t, D_MODEL), F32), jax.ShapeDtypeStruct((t, D_MODEL), BF16)]
    return pl.pallas_call(
        functools.partial(_ple_body, final=final),
        grid=(t // tm,),
        in_specs=[row_block(D_MODEL), whole(g), row_block(PLE_DIM), whole(wg), whole(wp), whole(g_next)],
        out_specs=out_specs,
        out_shape=out_shape,
        compiler_params=_params(("parallel",)),
        name="ple",
    )(x, g, p, wg, wp, g_next)


def _cmlp_body(u_ref, v_ref, lng_ref, lnb_ref, ws_ref, bst_ref, yb_ref, *rest, seq_len):
    u = jax.nn.gelu(u_ref[...])
    v = jax.nn.gelu(v_ref[...])
    mu = jnp.mean(v, axis=-1, keepdims=True)
    vc = v - mu
    var = jnp.mean(vc * vc, axis=-1, keepdims=True)
    vn = vc * lax.rsqrt(var + EPS) * lng_ref[...] + lnb_ref[...]
    if rest:
        rest[0][...] = vn
    r = lax.broadcasted_iota(jnp.int32, (CM_CHUNK, CM_CHUNK), 0)
    c = lax.broadcasted_iota(jnp.int32, (CM_CHUNK, CM_CHUNK), 1)
    if seq_len >= CM_CHUNK:
        keep = r >= c
    else:
        sh = seq_len.bit_length() - 1
        keep = ((r >> sh) == (c >> sh)) & ((r & (seq_len - 1)) >= (c & (seq_len - 1)))
    for g in range(CM_GROUPS):
        sl = slice(g * CM_GROUP_DIM, (g + 1) * CM_GROUP_DIM)
        w = jnp.where(keep, ws_ref[g], 0.0).astype(BF16)
        mixed = _dot(w, vn[:, sl].astype(BF16)) + bst_ref[:, g:g + 1]
        yb_ref[:, sl] = (u[:, sl] * mixed).astype(yb_ref.dtype)


def _cmlp(proj, ln_g, ln_b, ws_tiled, bs_t, seq_len, emit_v, out_dtype):
    t = proj.shape[0]
    out_shape = [jax.ShapeDtypeStruct((t, CM_WIDTH), out_dtype)]
    out_specs = [pl.BlockSpec((CM_CHUNK, CM_WIDTH), lambda i: (i, 0))]
    if emit_v:
        out_shape.append(jax.ShapeDtypeStruct((t, CM_WIDTH), F32))
        out_specs.append(pl.BlockSpec((CM_CHUNK, CM_WIDTH), lambda i: (i, 0)))
    cu = C_UV // CM_WIDTH
    res = pl.pallas_call(
        functools.partial(_cmlp_body, seq_len=seq_len),
        grid=(t // CM_CHUNK,),
        in_specs=[
            pl.BlockSpec((CM_CHUNK, CM_WIDTH), lambda i: (i, cu)),
            pl.BlockSpec((CM_CHUNK, CM_WIDTH), lambda i: (i, cu + 1)),
            pl.BlockSpec((1, CM_WIDTH), lambda i: (0, 0)),
            pl.BlockSpec((1, CM_WIDTH), lambda i: (0, 0)),
            pl.BlockSpec((CM_GROUPS, CM_CHUNK, CM_CHUNK), lambda i: (0, 0, 0)),
            pl.BlockSpec((CM_CHUNK, CM_GROUPS), lambda i: (0, 0)),
        ],
        out_specs=out_specs,
        out_shape=out_shape,
        compiler_params=_params(("parallel",)),
        name="chunk_mlp",
    )(proj, proj, ln_g, ln_b, ws_tiled, bs_t)
    return (res[0], res[1]) if emit_v else (res[0], None)


def _gla_level_constants(n):
    levels = n.bit_length() - 1
    t = np.arange(n)[:, None]
    j = np.arange(n)[None, :]
    sums = [j <= t, j > t]
    masks = [t == j]
    for l in range(1, levels + 1):
        w, half = 1 << l, 1 << (l - 1)
        start = (t >> l) << l
        upper = ((t >> (l - 1)) & 1) == 1
        a_up = (j >= start + half) & (j <= t)
        a_lo = (j > t) & (j < start + half)
        sums.append(np.where(upper, a_up, a_lo))
        jj = j
        upper_t = ((t >> (l - 1)) & 1) == 1
        lower_s = ((jj >> (l - 1)) & 1) == 0
        masks.append(((t >> l) == (jj >> l)) & upper_t & lower_s)
    return (jnp.asarray(np.stack(sums), BF16), jnp.asarray(np.stack(masks), F32), levels)


def _gla_log_decay(lr_ref, wlr_ref, blr_ref):
    x = _dot(lr_ref[...].astype(BF16), wlr_ref[...]) + blr_ref[...]
    return _log_sigmoid(x) * (1.0 / GLA_TAU)


def _gla_finish(o, g, gn):
    ms = jnp.mean(o * o, axis=-1, keepdims=True)
    return o * lax.rsqrt(ms + EPS) * gn * _silu(g)


def _gla_prompt_body(q_ref, k_ref, v_ref, g_ref, lr_ref, wlr_ref, blr_ref, gn_ref, a_ref, m_ref,
                     ya_ref, s_ref, *, levels):
    rows = q_ref.shape[0]

    @pl.when(pl.program_id(1) == 0)
    def _():
        s_ref[...] = jnp.zeros(s_ref.shape, F32)

    q = q_ref[...] * (GLA_DK ** -0.5)
    k = k_ref[...]
    la = _gla_log_decay(lr_ref, wlr_ref, blr_ref)
    la2 = jnp.concatenate(_split2(la), axis=1)

    def decay_sum(i):
        d = _dot(a_ref[i], la2)
        return d[:, :GLA_QK] + d[:, GLA_QK:]

    b = decay_sum(0)
    qe = (q * jnp.exp(b)).astype(BF16)
    kd = (k * jnp.exp(decay_sum(1))).astype(BF16)
    e_end_t = jnp.exp(jnp.broadcast_to(b[rows - 1:rows, :], (SUBLANES, GLA_QK))).T
    row = lax.broadcasted_iota(jnp.int32, (rows, GLA_QK), 0)
    xs = []
    for l in range(1, levels + 1):
        e = jnp.exp(decay_sum(l + 1))
        upper = ((row >> (l - 1)) & 1) == 1
        xs.append((jnp.where(upper, q, k) * e).astype(BF16))
    qb = q.astype(BF16)
    kb = k.astype(BF16)
    for h in range(GLA_HEADS):
        ks = slice(h * GLA_DK, (h + 1) * GLA_DK)
        vs = slice(h * GLA_DV, (h + 1) * GLA_DV)
        att = m_ref[0] * _nt(qb[:, ks], kb[:, ks])
        for l in range(1, levels + 1):
            x = xs[l - 1][:, ks]
            att += m_ref[l] * _nt(x, x)
        vh = v_ref[:, vs].astype(BF16)
        s = s_ref[0, h]
        o = _dot(qe[:, ks], s.astype(BF16)) + _dot(att.astype(BF16), vh)
        s_ref[0, h] = s * e_end_t[ks, 0:1] + _tn(kd[:, ks], vh)
        ya_ref[:, vs] = _gla_finish(o, g_ref[:, vs], gn_ref[:, vs]).astype(ya_ref.dtype)


def _gla_prompt(proj, small, bsz, length, w_lr, b_lr, gn):
    nchunk = length // CHUNK
    a_mats, masks, levels = _gla_level_constants(CHUNK)
    row = lambda b, c: b * nchunk + c
    return pl.pallas_call(
        functools.partial(_gla_prompt_body, levels=levels),
        grid=(bsz, nchunk),
        in_specs=[
            pl.BlockSpec((CHUNK, GLA_QK), lambda b, c: (row(b, c), C_Q // GLA_QK)),
            pl.BlockSpec((CHUNK, GLA_QK), lambda b, c: (row(b, c), C_K // GLA_QK)),
            pl.BlockSpec((CHUNK, GLA_V), lambda b, c: (row(b, c), C_V // GLA_V)),
            pl.BlockSpec((CHUNK, GLA_V), lambda b, c: (row(b, c), C_G // GLA_V)),
            pl.BlockSpec((CHUNK, LANES), lambda b, c: (row(b, c), SMALL_LR)),
            pl.BlockSpec((LANES, GLA_QK), lambda b, c: (0, 0)),
            pl.BlockSpec((1, GLA_QK), lambda b, c: (0, 0)),
            pl.BlockSpec((1, GLA_V), lambda b, c: (0, 0)),
            pl.BlockSpec(a_mats.shape, lambda b, c: (0, 0, 0)),
            pl.BlockSpec(masks.shape, lambda b, c: (0, 0, 0)),
        ],
        out_specs=[
            pl.BlockSpec((CHUNK, GLA_V), lambda b, c: (row(b, c), 0)),
            pl.BlockSpec((1, GLA_HEADS, GLA_DK, GLA_DV), lambda b, c: (b, 0, 0, 0)),
        ],
        out_shape=[
            jax.ShapeDtypeStruct((bsz * length, GLA_V), BF16),
            jax.ShapeDtypeStruct((bsz, GLA_HEADS, GLA_DK, GLA_DV), F32),
        ],
        compiler_params=_params(("parallel", "arbitrary")),
        name="gla_prompt",
    )(proj, proj, proj, proj, small, w_lr, b_lr, gn, a_mats, masks)


def _roll_rows(x, d):
    return pltpu.roll(x, d, 0) if d else x


def _gla_sample_body(q_ref, k_ref, v_ref, g_ref, lr_ref, wlr_ref, blr_ref, gn_ref, s0_ref,
                     ya_ref, s_ref):
    n = SAMPLE_LEN
    pos_k = lax.broadcasted_iota(jnp.int32, (SAMPLE_ROWS, GLA_QK), 0) & (n - 1)
    row_v = lax.broadcasted_iota(jnp.int32, (SAMPLE_ROWS, GLA_DV), 0)
    q = q_ref[...] * (GLA_DK ** -0.5)
    k = k_ref[...]
    v = v_ref[...]
    la = _gla_log_decay(lr_ref, wlr_ref, blr_ref)
    back = [_roll_rows(la, d) for d in range(n)]
    win = [None, la]
    for d in range(2, n):
        win.append(win[-1] + back[d - 1])
    b = la
    suffix = jnp.zeros_like(la)
    for d in range(1, n):
        b = b + jnp.where(pos_k >= d, back[d], 0.0)
        suffix = suffix + jnp.where(pos_k < n - d, pltpu.roll(la, SAMPLE_ROWS - d, 0), 0.0)
    eb = jnp.exp(b)
    qe = (q * eb).astype(BF16)
    kd = k * jnp.exp(suffix)
    eb_t = eb.T
    nseq = SAMPLE_ROWS // n
    seq_k = lax.broadcasted_iota(jnp.int32, (SAMPLE_ROWS, GLA_QK), 0) >> SAMPLE_SHIFT
    kd_seq = [jnp.where(seq_k == j, kd, 0.0).astype(BF16) for j in range(nseq)]
    pair = []
    for d in range(n):
        p = q * _roll_rows(k, d)
        if d:
            p = p * jnp.exp(win[d])
        pair.append(p)
    vback = [_roll_rows(v, d) for d in range(n)]
    for h in range(GLA_HEADS):
        ks = slice(h * GLA_DK, (h + 1) * GLA_DK)
        vs = slice(h * GLA_DV, (h + 1) * GLA_DV)
        o = jnp.zeros((SAMPLE_ROWS, GLA_DV), F32)
        for d in range(n):
            score = jnp.sum(pair[d][:, ks], axis=-1, keepdims=True)
            o = o + jnp.where((row_v & (n - 1)) >= d, score * vback[d][:, vs], 0.0)
        vh = v[:, vs].astype(BF16)
        for j in range(nseq):
            s = s0_ref[j, h]
            o = o + jnp.where((row_v >> SAMPLE_SHIFT) == j, _dot(qe[:, ks], s.astype(BF16)), 0.0)
            col = n * j + n - 1
            s_ref[j, h] = s * eb_t[ks, col:col + 1] + _tn(kd_seq[j][:, ks], vh)
        ya_ref[:, vs] = _gla_finish(o, g_ref[:, vs], gn_ref[:, vs]).astype(ya_ref.dtype)


def _stacked_state_call(body, prev_out, n_in):
    if prev_out is None:
        return body, [], [], {}
    wrapped = lambda *refs: body(*refs[:n_in], *refs[n_in + 1:])
    return wrapped, [pl.BlockSpec(memory_space=pl.ANY)], [prev_out], {n_in: 1}


def _gla_sample(proj, small, state_all, layer, prev_out, w_lr, b_lr, gn):
    t = proj.shape[0]
    nseq = SAMPLE_ROWS // SAMPLE_LEN
    state_spec = pl.BlockSpec((None, nseq, GLA_HEADS, GLA_DK, GLA_DV), lambda i: (layer, i, 0, 0, 0))
    body, extra_specs, extra_args, aliases = _stacked_state_call(_gla_sample_body, prev_out, 9)
    return pl.pallas_call(
        body,
        grid=(t // SAMPLE_ROWS,),
        in_specs=[
            pl.BlockSpec((SAMPLE_ROWS, GLA_QK), lambda i: (i, C_Q // GLA_QK)),
            pl.BlockSpec((SAMPLE_ROWS, GLA_QK), lambda i: (i, C_K // GLA_QK)),
            pl.BlockSpec((SAMPLE_ROWS, GLA_V), lambda i: (i, C_V // GLA_V)),
            pl.BlockSpec((SAMPLE_ROWS, GLA_V), lambda i: (i, C_G // GLA_V)),
            pl.BlockSpec((SAMPLE_ROWS, LANES), lambda i: (i, SMALL_LR)),
            pl.BlockSpec((LANES, GLA_QK), lambda i: (0, 0)),
            pl.BlockSpec((1, GLA_QK), lambda i: (0, 0)),
            pl.BlockSpec((1, GLA_V), lambda i: (0, 0)),
            state_spec,
        ] + extra_specs,
        out_specs=[pl.BlockSpec((SAMPLE_ROWS, GLA_V), lambda i: (i, 0)), state_spec],
        out_shape=[
            jax.ShapeDtypeStruct((t, GLA_V), F32),
            jax.ShapeDtypeStruct(state_all.shape, F32),
        ],
        input_output_aliases=aliases,
        compiler_params=_params(("parallel",)),
        name="gla_sample",
    )(proj, proj, proj, proj, small, w_lr, b_lr, gn, state_all, *extra_args)


XBC_PART = 2048


def _ssd_gate_norm(y, z, gn):
    yz = y * _silu(z)
    ms = jnp.mean(yz * yz, axis=-1, keepdims=True)
    return yz * lax.rsqrt(ms + EPS) * gn


def _ssd_prompt_body(x1_ref, x2_ref, bc_ref, z1_ref, z2_ref, dt_ref, cw_ref, cb_ref, dtb_ref, alog_ref,
                     dx_ref, gn_ref, e_ref, a_ref, yc_ref, h_ref, tail_scr, ht_scr):
    rows = x1_ref.shape[0]
    step = pl.program_id(1)

    @pl.when(step == 0)
    def _():
        tail_scr[...] = jnp.zeros(tail_scr.shape, F32)
        ht_scr[...] = jnp.zeros(ht_scr.shape, F32)

    row8 = lax.broadcasted_iota(jnp.int32, (SUBLANES, XBC_PART), 0)

    def conv(x_ref, part):
        cs = slice(part * XBC_PART, (part + 1) * XBC_PART)
        x = x_ref[...]
        prev = tail_scr[:, cs]
        acc = cb_ref[:, cs] + x * cw_ref[SSD_CONV - 1:SSD_CONV, cs]
        for d in range(1, SSD_CONV):
            xr = pltpu.roll(x, d, 0)
            first = jnp.where(row8 < d, pltpu.roll(prev, d, 0), xr[:SUBLANES])
            xd = jnp.concatenate([first, xr[SUBLANES:]], axis=0)
            acc = acc + xd * cw_ref[SSD_CONV - 1 - d:SSD_CONV - d, cs]
        tail_scr[:, cs] = x[rows - SUBLANES:, :]
        return _silu(acc)

    xs_halves = [conv(x1_ref, 0), conv(x2_ref, 1)]
    bc = conv(bc_ref, 2)
    half_w = SSD_GROUPS * SSD_STATE
    bmat = bc[:, :half_w]
    cmat = bc[:, half_w:]
    z_halves = [z1_ref, z2_ref]

    dt = _softplus(dt_ref[...] + dtb_ref[...])
    a = dt * (-jnp.exp(alog_ref[...]))
    a2 = jnp.concatenate(_split2(a), axis=1)

    def head_sum(i):
        d = _dot(a_ref[i], a2)
        return d[:, :LANES] + d[:, LANES:]

    cs_in = head_sum(0)
    cs_suf = head_sum(1)
    cs_t = cs_in.T
    dt_t = dt.T
    cs_hi, cs_lo = _split2(cs_in)
    suf_hi, suf_lo = _split2(cs_suf)
    dt_hi, dt_lo = _split2(dt)
    r_i = lax.broadcasted_iota(jnp.int32, (rows, rows), 0)
    c_i = lax.broadcasted_iota(jnp.int32, (rows, rows), 1)
    causal = r_i >= c_i
    lane_head = lax.broadcasted_iota(jnp.int32, (rows, SSD_GROUP_WIDTH), 1) >> HEADDIM_SHIFT
    heads_per_group = SSD_HEADS // SSD_GROUPS
    per_half = SSD_GROUPS // 2
    for g in range(SSD_GROUPS):
        gs = slice(g * SSD_GROUP_WIDTH, (g + 1) * SSD_GROUP_WIDTH)
        ls = slice((g % per_half) * SSD_GROUP_WIDTH, (g % per_half + 1) * SSD_GROUP_WIDTH)
        ns = slice(g * SSD_STATE, (g + 1) * SSD_STATE)
        xg = xs_halves[g // per_half][:, ls]
        eg = e_ref[:, gs]
        csx = _dot(cs_hi, eg) + _dot(cs_lo, eg)
        sufx = _dot(suf_hi, eg) + _dot(suf_lo, eg)
        dtx = _dot(dt_hi, eg) + _dot(dt_lo, eg)
        bg = bmat[:, ns].astype(BF16)
        cg = cmat[:, ns].astype(BF16)
        cb = _nt(cg, bg)
        ws = []
        xb = []
        for r in range(heads_per_group):
            h = g * heads_per_group + r
            dm = cs_in[:, h:h + 1] - cs_t[h:h + 1, :]
            dec = jnp.where(causal, jnp.exp(jnp.minimum(dm, 0.0)), 0.0)
            ws.append((cb * dec * dt_t[h:h + 1, :]).astype(BF16))
            xb.append(jnp.where(lane_head == r, xg, 0.0).astype(BF16))
        y = _dot(jnp.concatenate(ws, axis=1), jnp.concatenate(xb, axis=0))
        ht = ht_scr[g]
        y = y + _dot(cg, ht.astype(BF16)) * jnp.exp(csx)
        y = y + dx_ref[:, gs] * xg
        wx = (jnp.exp(sufx) * dtx * xg).astype(BF16)
        ht_scr[g] = ht * jnp.exp(csx[rows - 1:rows, :]) + _tn(bg, wx)
        z = z_halves[g // per_half][:, ls]
        yc_ref[:, gs] = _ssd_gate_norm(y, z, gn_ref[:, gs]).astype(yc_ref.dtype)

    @pl.when(step == pl.num_programs(1) - 1)
    def _():
        for g in range(SSD_GROUPS):
            h_ref[0, g] = ht_scr[g].T


def _tri_constants(n):
    t = np.arange(n)[:, None]
    j = np.arange(n)[None, :]
    return jnp.asarray(np.stack([j <= t, j > t]), BF16)


def _head_expand_matrix():
    h = np.arange(LANES)[:, None]
    lane = np.arange(SSD_INNER)[None, :]
    return jnp.asarray(h == lane // SSD_HEADDIM, BF16)


def _ssd_prompt(proj, small, bsz, length, conv_w, conv_b, dt_bias, a_log, d_x, gn):
    nchunk = length // CHUNK
    row = lambda b, c: b * nchunk + c
    cx = C_XBC // XBC_PART
    cz = C_Z // XBC_PART
    tri = _tri_constants(CHUNK)
    e = _head_expand_matrix()
    full = lambda shape: pl.BlockSpec(shape, lambda b, c: (0,) * len(shape))
    return pl.pallas_call(
        _ssd_prompt_body,
        grid=(bsz, nchunk),
        in_specs=[
            pl.BlockSpec((CHUNK, XBC_PART), lambda b, c: (row(b, c), cx)),
            pl.BlockSpec((CHUNK, XBC_PART), lambda b, c: (row(b, c), cx + 1)),
            pl.BlockSpec((CHUNK, XBC_PART), lambda b, c: (row(b, c), cx + 2)),
            pl.BlockSpec((CHUNK, XBC_PART), lambda b, c: (row(b, c), cz)),
            pl.BlockSpec((CHUNK, XBC_PART), lambda b, c: (row(b, c), cz + 1)),
            pl.BlockSpec((CHUNK, LANES), lambda b, c: (row(b, c), SMALL_DT)),
            full((SSD_CONV, SSD_CONV_DIM)),
            full((1, SSD_CONV_DIM)),
            full((1, LANES)),
            full((1, LANES)),
            full((1, SSD_INNER)),
            full((1, SSD_INNER)),
            full((LANES, SSD_INNER)),
            full(tri.shape),
        ],
        out_specs=[
            pl.BlockSpec((CHUNK, SSD_INNER), lambda b, c: (row(b, c), 0)),
            pl.BlockSpec((1, SSD_GROUPS, SSD_GROUP_WIDTH, SSD_STATE), lambda b, c: (b, 0, 0, 0)),
        ],
        out_shape=[
            jax.ShapeDtypeStruct((bsz * length, SSD_INNER), BF16),
            jax.ShapeDtypeStruct((bsz, SSD_GROUPS, SSD_GROUP_WIDTH, SSD_STATE), F32),
        ],
        scratch_shapes=[
            pltpu.VMEM((SUBLANES, SSD_CONV_DIM), F32),
            pltpu.VMEM((SSD_GROUPS, SSD_STATE, SSD_GROUP_WIDTH), F32),
        ],
        compiler_params=_params(("parallel", "arbitrary")),
        name="ssd_prompt",
    )(proj, proj, proj, proj, proj, small, conv_w, conv_b, dt_bias, a_log, d_x, gn, e, tri)


def _ssd_sample_body(x1_ref, x2_ref, bc_ref, z1_ref, z2_ref, dt_ref, cw_ref, cb_ref, dtb_ref, alog_ref,
                     dx_ref, gn_ref, e_ref, prev_ref, h0_ref, yc_ref, h_ref):
    n = SAMPLE_LEN
    nseq = SAMPLE_ROWS // n
    pos_p = lax.broadcasted_iota(jnp.int32, (SAMPLE_ROWS, XBC_PART), 0) & (n - 1)

    def conv(x_ref, part):
        cs = slice(part * XBC_PART, (part + 1) * XBC_PART)
        x = x_ref[...]
        prev = prev_ref[0][:, cs]
        acc = cb_ref[:, cs] + x * cw_ref[SSD_CONV - 1:SSD_CONV, cs]
        for d in range(1, SSD_CONV):
            xd = jnp.where(pos_p < d, pltpu.roll(prev, d, 0), pltpu.roll(x, d, 0))
            acc = acc + xd * cw_ref[SSD_CONV - 1 - d:SSD_CONV - d, cs]
        return _silu(acc)

    xs = jnp.concatenate([conv(x1_ref, 0), conv(x2_ref, 1)], axis=1)
    bc = conv(bc_ref, 2)
    half_w = SSD_GROUPS * SSD_STATE
    bmat = bc[:, :half_w]
    cmat = bc[:, half_w:]
    z = jnp.concatenate([z1_ref[...], z2_ref[...]], axis=1)

    dt = _softplus(dt_ref[...] + dtb_ref[...])
    dt_hi, dt_lo = _split2(dt)
    dtx = _dot(dt_hi, e_ref[...]) + _dot(dt_lo, e_ref[...])
    a_hi, a_lo = _split2(-jnp.exp(alog_ref[...]))
    anegx = _dot(jnp.broadcast_to(a_hi, (SUBLANES, LANES)), e_ref[...]) + \
        _dot(jnp.broadcast_to(a_lo, (SUBLANES, LANES)), e_ref[...])
    ax = dtx * anegx
    pos = lax.broadcasted_iota(jnp.int32, (SAMPLE_ROWS, SSD_INNER), 0) & (n - 1)
    seq_g = lax.broadcasted_iota(jnp.int32, (SAMPLE_ROWS, SSD_GROUP_WIDTH), 0) >> SAMPLE_SHIFT
    back = [_roll_rows(ax, d) for d in range(n)]
    win = [None, ax]
    for d in range(2, n):
        win.append(win[-1] + back[d - 1])
    csx = ax
    sufx = jnp.zeros_like(ax)
    for d in range(1, n):
        csx = csx + jnp.where(pos >= d, back[d], 0.0)
        sufx = sufx + jnp.where(pos < n - d, pltpu.roll(ax, SAMPLE_ROWS - d, 0), 0.0)
    ecs = jnp.exp(csx)
    ecs_t = ecs.T
    wx = jnp.exp(sufx) * dtx * xs

    y = dx_ref[...] * xs
    for d in range(n):
        prod = cmat * _roll_rows(bmat, d)
        cbx = jnp.concatenate(
            [jnp.broadcast_to(jnp.sum(prod[:, g * SSD_STATE:(g + 1) * SSD_STATE], axis=-1, keepdims=True),
                              (SAMPLE_ROWS, SSD_GROUP_WIDTH)) for g in range(SSD_GROUPS)], axis=1)
        term = cbx * _roll_rows(dtx, d) * _roll_rows(xs, d)
        if d:
            term = term * jnp.exp(win[d])
        y = y + jnp.where(pos >= d, term, 0.0)

    y_inter = []
    for g in range(SSD_GROUPS):
        gs = slice(g * SSD_GROUP_WIDTH, (g + 1) * SSD_GROUP_WIDTH)
        ns = slice(g * SSD_STATE, (g + 1) * SSD_STATE)
        bg = bmat[:, ns].astype(BF16)
        cg = cmat[:, ns].astype(BF16)
        acc = jnp.zeros((SAMPLE_ROWS, SSD_GROUP_WIDTH), F32)
        for j in range(nseq):
            h0 = h0_ref[j, g]
            acc = acc + jnp.where(seq_g == j, _nt(cg, h0.astype(BF16)), 0.0)
            wxj = jnp.where(seq_g == j, wx[:, gs], 0.0).astype(BF16)
            col = n * j + n - 1
            h_ref[j, g] = h0 * ecs_t[gs, col:col + 1] + _tn(wxj, bg)
        y_inter.append(acc)
    y = y + jnp.concatenate(y_inter, axis=1) * ecs
    for g in range(SSD_GROUPS):
        gs = slice(g * SSD_GROUP_WIDTH, (g + 1) * SSD_GROUP_WIDTH)
        yc_ref[:, gs] = _ssd_gate_norm(y[:, gs], z[:, gs], gn_ref[:, gs]).astype(yc_ref.dtype)


def _ssd_sample(proj, small, conv_prev, state_all, layer, prev_out, conv_w, conv_b, dt_bias, a_log, d_x, gn):
    t = proj.shape[0]
    nseq = SAMPLE_ROWS // SAMPLE_LEN
    cx = C_XBC // XBC_PART
    cz = C_Z // XBC_PART
    e = _head_expand_matrix()
    full = lambda shape: pl.BlockSpec(shape, lambda i: (0,) * len(shape))
    state_spec = pl.BlockSpec((None, nseq, SSD_GROUPS, SSD_GROUP_WIDTH, SSD_STATE), lambda i: (layer, i, 0, 0, 0))
    body, extra_specs, extra_args, aliases = _stacked_state_call(_ssd_sample_body, prev_out, 15)
    return pl.pallas_call(
        body,
        grid=(t // SAMPLE_ROWS,),
        in_specs=[
            pl.BlockSpec((SAMPLE_ROWS, XBC_PART), lambda i: (i, cx)),
            pl.BlockSpec((SAMPLE_ROWS, XBC_PART), lambda i: (i, cx + 1)),
            pl.BlockSpec((SAMPLE_ROWS, XBC_PART), lambda i: (i, cx + 2)),
            pl.BlockSpec((SAMPLE_ROWS, XBC_PART), lambda i: (i, cz)),
            pl.BlockSpec((SAMPLE_ROWS, XBC_PART), lambda i: (i, cz + 1)),
            pl.BlockSpec((SAMPLE_ROWS, LANES), lambda i: (i, SMALL_DT)),
            full((SSD_CONV, SSD_CONV_DIM)),
            full((1, SSD_CONV_DIM)),
            full((1, LANES)),
            full((1, LANES)),
            full((1, SSD_INNER)),
            full((1, SSD_INNER)),
            full((LANES, SSD_INNER)),
            pl.BlockSpec((1, SAMPLE_ROWS, SSD_CONV_DIM), lambda i: (i, 0, 0)),
            state_spec,
        ] + extra_specs,
        out_specs=[pl.BlockSpec((SAMPLE_ROWS, SSD_INNER), lambda i: (i, 0)), state_spec],
        out_shape=[
            jax.ShapeDtypeStruct((t, SSD_INNER), F32),
            jax.ShapeDtypeStruct(state_all.shape, F32),
        ],
        input_output_aliases=aliases,
        compiler_params=_params(("parallel",)),
        name="ssd_sample",
    )(proj, proj, proj, proj, proj, small, conv_w, conv_b, dt_bias, a_log, d_x, gn, e, conv_prev, state_all,
      *extra_args)


def _narrow_w_in_t(w_t):
    lr = w_t[NATIVE_UV - GLA_LOWRANK:NATIVE_UV]
    dt = w_t[NATIVE_DT:NATIVE_GATE]
    pad = lambda n: jnp.zeros((n, w_t.shape[1]), w_t.dtype)
    return jnp.concatenate([lr, pad(LANES - GLA_LOWRANK), dt, pad(LANES - SSD_HEADS)], axis=0)


def _pad_lanes(v, n):
    return jnp.pad(v, (0, n - v.shape[0])).reshape(1, n)


def _layer_weights(i, w_in_t, g_next, g_mix, w_in, w_gla_lr, b_gla_lr, g_gla_norm, cm_ln_g, cm_ln_b, cm_ws,
                   cm_bs, ssd_conv_w, ssd_conv_b, ssd_dt_bias, ssd_a_log, ssd_d, g_ssd_norm,
                   w_br_gla, w_br_cm, w_br_ssd, w_o, g_ffn, w_ffn_in, w_ffn_out,
                   g_ple, w_ple_gate, w_ple_proj):
    row = lambda v: v[i].reshape(1, -1)
    n_tile = CM_CHUNK // SAMPLE_LEN
    return dict(
        w_in_t=w_in_t,
        w_in_narrow=_narrow_w_in_t(w_in_t[i]),
        layer=i,
        g_next=g_next.reshape(1, -1),
        w_lr=jnp.pad(w_gla_lr[i], ((0, LANES - GLA_LOWRANK), (0, 0))).astype(BF16),
        b_lr=row(b_gla_lr),
        g_gla=row(g_gla_norm),
        ln_g=row(cm_ln_g),
        ln_b=row(cm_ln_b),
        ws_prompt=cm_ws[i],
        bs_prompt=cm_bs[i].T,
        ws_sample=jnp.tile(cm_ws[i][:, :SAMPLE_LEN, :SAMPLE_LEN], (1, n_tile, n_tile)),
        bs_sample=jnp.tile(cm_bs[i][:, :SAMPLE_LEN].T, (n_tile, 1)),
        conv_w=ssd_conv_w[i],
        conv_b=row(ssd_conv_b),
        dt_bias=_pad_lanes(ssd_dt_bias[i], LANES),
        a_log=_pad_lanes(ssd_a_log[i], LANES),
        d_x=jnp.repeat(ssd_d[i], SSD_HEADDIM).reshape(1, SSD_INNER),
        g_ssd=row(g_ssd_norm),
        w_br_gla=w_br_gla[i].astype(BF16),
        w_br_cm=w_br_cm[i].astype(BF16),
        w_br_ssd=w_br_ssd[i].astype(BF16),
        w_o=w_o[i].astype(BF16),
        g_ffn=row(g_ffn),
        w_ffn_in=w_ffn_in,
        w_ffn_out=w_ffn_out[i].astype(BF16),
        g_ple=row(g_ple),
        w_ple_gate=w_ple_gate[i].astype(BF16),
        w_ple_proj=w_ple_proj[i].astype(BF16),
    )


def _dense_tail(x, p, proj, ya, yb, yc, w, tm_ws, final):
    mix = _merge(ya, yb, yc, proj, w["w_br_gla"], w["w_br_cm"], w["w_br_ssd"], 512, 512)
    x, n = _out_proj(mix, w["w_o"], x, w["g_ffn"], 256)
    h = _ffn_in(n, w["w_ffn_in"], w["layer"], tm_ws, 512)
    x = _matmul_residual(h, w["w_ffn_out"], x, 512, 512)
    return _ple(x, w["g_ple"], p, w["w_ple_gate"], w["w_ple_proj"], w["g_next"], final, 256)


def _input_projection(n, w, tm):
    return _proj_matmul(n, w["w_in_t"], w["layer"], tm), _matmul_small_nt(n, w["w_in_narrow"], 512)


def _last_conv_rows(proj, bsz, length):
    keep = SSD_CONV - 1
    if length % SUBLANES == 0:
        rows = proj.reshape(bsz, length, proj.shape[-1])[:, length - keep:]
        return rows[:, :, C_XBC:C_XBC + SSD_CONV_DIM]
    xbc = proj[:, C_XBC:C_XBC + SSD_CONV_DIM]
    return xbc.reshape(bsz, length, SSD_CONV_DIM)[:, length - keep:]


def _prompt_layer(x, n, p, bsz, length, w, final):
    proj, small = _input_projection(n, w, 1024)
    ya, s_gla = _gla_prompt(proj, small, bsz, length, w["w_lr"], w["b_lr"], w["g_gla"])
    yb, _ = _cmlp(proj, w["ln_g"], w["ln_b"], w["ws_prompt"], w["bs_prompt"], CM_CHUNK, False, BF16)
    yc, s_ssm = _ssd_prompt(proj, small, bsz, length, w["conv_w"], w["conv_b"], w["dt_bias"], w["a_log"],
                            w["d_x"], w["g_ssd"])
    s_conv = _last_conv_rows(proj, bsz, length)
    outs = _dense_tail(x, p, proj, ya, yb, yc, w, 1024, final)
    s_ssm = s_ssm.reshape(bsz, SSD_HEADS, SSD_HEADDIM, SSD_STATE)
    return outs, s_gla, s_ssm, s_conv


def _sample_layer(x, n_in, p, bsz, gla_all, ssm_all, s_conv, prev_gla, prev_ssm, w, final):
    n = SAMPLE_LEN
    proj, small = _input_projection(n_in, w, 512)
    ya, gla_out = _gla_sample(proj, small, gla_all, w["layer"], prev_gla, w["w_lr"], w["b_lr"], w["g_gla"])
    yb, v_rows = _cmlp(proj, w["ln_g"], w["ln_b"], w["ws_sample"], w["bs_sample"], n, True, F32)
    sc = s_conv.reshape(bsz // 2, 2, SSD_CONV - 1, SSD_CONV_DIM)
    zrow = jnp.zeros((bsz // 2, 1, SSD_CONV_DIM), F32)
    conv_prev = jnp.concatenate([zrow, sc[:, 1], zrow, sc[:, 0]], axis=1)
    yc, ssm_out = _ssd_sample(proj, small, conv_prev, ssm_all, w["layer"], prev_ssm, w["conv_w"], w["conv_b"],
                              w["dt_bias"], w["a_log"], w["d_x"], w["g_ssd"])
    s_conv_new = _last_conv_rows(proj, bsz, n)
    outs = _dense_tail(x, p, proj, ya, yb, yc, w, 512, final)
    return outs, gla_out, ssm_out, s_conv_new, v_rows.reshape(bsz, n, CM_WIDTH)


def kernel(x_prompt, x_sample, state_gla, state_ssm, state_conv, p_prompt, p_sample, g_mix, w_in, w_gla_lr, b_gla_lr, g_gla_norm, cm_ln_g, cm_ln_b, cm_ws, cm_bs, ssd_conv_w, ssd_conv_b, ssd_dt_bias, ssd_a_log, ssd_d, g_ssd_norm, w_br_gla, w_br_cm, w_br_ssd, w_o, g_ffn, w_ffn_in, w_ffn_out, g_ple, w_ple_gate, w_ple_proj, g_final):
    weights = (g_mix, w_in, w_gla_lr, b_gla_lr, g_gla_norm, cm_ln_g, cm_ln_b, cm_ws, cm_bs,
               ssd_conv_w, ssd_conv_b, ssd_dt_bias, ssd_a_log, ssd_d, g_ssd_norm,
               w_br_gla, w_br_cm, w_br_ssd, w_o, g_ffn, w_ffn_in, w_ffn_out,
               g_ple, w_ple_gate, w_ple_proj)
    bp, lp, _ = x_prompt.shape
    bs, ls, _ = x_sample.shape
    assert ls == SAMPLE_LEN and lp % CHUNK == 0 and bs % 2 == 0
    xp = x_prompt.reshape(bp * lp, D_MODEL)
    xs = x_sample.reshape(bs * ls, D_MODEL)
    g_first = g_mix[0].reshape(1, D_MODEL)
    np_ = _rms_cast(xp, g_first, 512, BF16)
    ns = _rms_cast(xs, g_first, 512, BF16)
    w_in_t = jnp.swapaxes(w_in, 1, 2)
    gla_p, ssm_p, conv_p = [], [], []
    conv_s, v_s = [], []
    ssm_all = state_ssm.reshape(DEPTH, bs, SSD_GROUPS, SSD_GROUP_WIDTH, SSD_STATE)
    gla_s = ssm_s = None
    for i in range(DEPTH):
        final = i == DEPTH - 1
        g_next = g_final if final else g_mix[i + 1]
        w = _layer_weights(i, w_in_t, g_next, *weights)
        outs, sg, sm, sc = _prompt_layer(xp, np_, p_prompt[i].reshape(bp * lp, PLE_DIM), bp, lp, w, final)
        gla_p.append(sg)
        ssm_p.append(sm)
        conv_p.append(sc)
        xp, np_ = (outs[0], None) if final else outs
        outs, gla_s, ssm_s, sc, vr = _sample_layer(xs, ns, p_sample[i].reshape(bs * ls, PLE_DIM), bs,
                                                   state_gla, ssm_all, state_conv[i], gla_s, ssm_s, w, final)
        conv_s.append(sc)
        v_s.append(vr)
        xs, ns = (outs[0], None) if final else outs
    y_prompt = xp.reshape(bp, lp, D_MODEL)
    y_sample = xs.reshape(bs, ls, D_MODEL)
    ssm_s = ssm_s.reshape(DEPTH, bs, SSD_HEADS, SSD_HEADDIM, SSD_STATE)
    return (y_prompt, y_sample, jnp.stack(gla_p), gla_s, jnp.stack(ssm_p), ssm_s,
            jnp.stack(conv_p), jnp.stack(conv_s), jnp.stack(v_s))
```

```python
import functools

import numpy as np
import jax
import jax.numpy as jnp
from jax import lax
from jax.experimental import pallas as pl
from jax.experimental.pallas import tpu as pltpu

F32 = jnp.float32
BF16 = jnp.bfloat16

D_MODEL = 2048
DEPTH = 2
GLA_HEADS = 4
GLA_DK = 256
GLA_DV = 512
GLA_QK = 1024
GLA_V = 2048
GLA_LOWRANK = 16
GLA_TAU = 16.0
CM_GROUPS = 8
CM_WIDTH = 2048
CM_GROUP_DIM = 256
CM_CHUNK = 128
SSD_INNER = 4096
SSD_HEADDIM = 64
SSD_HEADS = 64
SSD_GROUPS = 8
SSD_STATE = 128
SSD_CONV = 4
SSD_CONV_DIM = 6144
SSD_GROUP_WIDTH = SSD_INNER // SSD_GROUPS
D_FF = 5632
PLE_DIM = 256
EPS = 1e-6

LANES = 128
SUBLANES = 8
VMEM_LIMIT = 52 * 1024 * 1024

C_Q = 0
C_K = 1024
C_V = 2048
C_G = 4096
C_UV = 6144
C_Z = 10240
C_XBC = 14336
C_GATE = 20480
C_LR = 26624
SMALL_LR = 0
SMALL_DT = 1

CHUNK = 128
SAMPLE_ROWS = 8
SAMPLE_LEN = 4
SAMPLE_SHIFT = 2
HEADDIM_SHIFT = 6


def _params(sem):
    return pltpu.CompilerParams(dimension_semantics=sem, vmem_limit_bytes=VMEM_LIMIT)


def _nt(a, b):
    return lax.dot_general(a, b, (((1,), (1,)), ((), ())), preferred_element_type=F32)


def _tn(a, b):
    return lax.dot_general(a, b, (((0,), (0,)), ((), ())), preferred_element_type=F32)


def _dot(a, b):
    return jnp.dot(a, b, preferred_element_type=F32)


def _split2(x):
    hi = x.astype(BF16)
    lo = (x - hi.astype(F32)).astype(BF16)
    return hi, lo


def _softplus(x):
    return jnp.maximum(x, 0.0) + jnp.log1p(jnp.exp(-jnp.abs(x)))


def _log_sigmoid(x):
    return jnp.minimum(x, 0.0) - jnp.log1p(jnp.exp(-jnp.abs(x)))


def _silu(x):
    return x * jax.nn.sigmoid(x)


def _rms_cast_body(x_ref, g_ref, o_ref):
    x = x_ref[...]
    ms = jnp.mean(x * x, axis=-1, keepdims=True)
    o_ref[...] = (x * lax.rsqrt(ms + EPS) * g_ref[...]).astype(o_ref.dtype)


def _rms_cast(x, g, tm, dtype):
    t, k = x.shape
    return pl.pallas_call(
        _rms_cast_body,
        grid=(t // tm,),
        in_specs=[pl.BlockSpec((tm, k), lambda i: (i, 0)), pl.BlockSpec((1, k), lambda i: (0, 0))],
        out_specs=pl.BlockSpec((tm, k), lambda i: (i, 0)),
        out_shape=jax.ShapeDtypeStruct((t, k), dtype),
        compiler_params=_params(("parallel",)),
        name="rms_cast",
    )(x, g)


NATIVE_UV = 2 * GLA_QK + 2 * GLA_V + GLA_LOWRANK
NATIVE_DT = NATIVE_UV + 2 * CM_WIDTH + SSD_INNER + SSD_CONV_DIM
NATIVE_GATE = NATIVE_DT + SSD_HEADS
PROJ_TN = 1024
W_ROW_CHUNK = 256


def _proj_body(n_ref, w_ref, wn_ref, o_ref, w_scr):
    j = pl.program_id(0)
    tn = w_ref.shape[0]

    def load_weights(shift):
        for r in range(0, tn - shift, W_ROW_CHUNK):
            rows = min(W_ROW_CHUNK, tn - shift - r)
            w_scr[r:r + rows, :] = w_ref[r + shift:r + shift + rows, :].astype(BF16)
        if shift:
            w_scr[tn - shift:tn, :] = wn_ref[0:shift, :].astype(BF16)

    @pl.when(pl.program_id(1) == 0)
    def _():
        j_uv = C_UV // PROJ_TN
        j_gate = C_GATE // PROJ_TN
        pl.when(j < j_uv)(lambda: load_weights(0))
        pl.when((j >= j_uv) & (j < j_gate))(lambda: load_weights(NATIVE_UV - C_UV))
        pl.when(j >= j_gate)(lambda: load_weights(NATIVE_GATE - C_GATE))

    o_ref[...] = _nt(n_ref[...], w_scr[...])


def _proj_matmul(n, w_in_t, layer, tm):
    t, k = n.shape
    tn = PROJ_TN
    return pl.pallas_call(
        _proj_body,
        grid=(C_LR // tn, t // tm),
        in_specs=[
            pl.BlockSpec((tm, k), lambda j, i: (i, 0)),
            pl.BlockSpec((None, tn, k), lambda j, i: (layer, j, 0)),
            pl.BlockSpec((None, LANES, k), lambda j, i: (layer, (j + 1) * (tn // LANES), 0)),
        ],
        out_specs=pl.BlockSpec((tm, tn), lambda j, i: (i, j)),
        out_shape=jax.ShapeDtypeStruct((t, C_LR), F32),
        scratch_shapes=[pltpu.VMEM((tn, k), BF16)],
        compiler_params=_params(("parallel", "arbitrary")),
        name="proj_matmul",
    )(n, w_in_t, w_in_t)


def _mm_nt_body(a_ref, w_ref, o_ref):
    o_ref[...] = _nt(a_ref[...], w_ref[...].astype(BF16))


def _matmul_small_nt(a, w_t, tm):
    t, k = a.shape
    n = w_t.shape[0]
    return pl.pallas_call(
        _mm_nt_body,
        grid=(t // tm,),
        in_specs=[pl.BlockSpec((tm, k), lambda i: (i, 0)), pl.BlockSpec((n, k), lambda i: (0, 0))],
        out_specs=pl.BlockSpec((tm, n), lambda i: (i, 0)),
        out_shape=jax.ShapeDtypeStruct((t, n), F32),
        compiler_params=_params(("parallel",)),
        name="matmul_small",
    )(a, w_t)


def _ffn_in_body(n_ref, wg_ref, wu_ref, o_ref, wg_scr, wu_scr):
    @pl.when(pl.program_id(1) == 0)
    def _():
        for r in range(0, wg_ref.shape[0], W_ROW_CHUNK):
            rs = slice(r, r + W_ROW_CHUNK)
            wg_scr[rs, :] = wg_ref[rs, :].astype(BF16)
            wu_scr[rs, :] = wu_ref[rs, :].astype(BF16)

    n = n_ref[...]
    gate = _dot(n, wg_scr[...])
    up = _dot(n, wu_scr[...])
    o_ref[...] = (_silu(gate) * up).astype(o_ref.dtype)


def _ffn_in(n, w, layer, tm, tn):
    t, k = n.shape
    nblk = D_FF // tn
    return pl.pallas_call(
        _ffn_in_body,
        grid=(nblk, t // tm),
        in_specs=[
            pl.BlockSpec((tm, k), lambda j, i: (i, 0)),
            pl.BlockSpec((None, k, tn), lambda j, i: (layer, 0, j)),
            pl.BlockSpec((None, k, tn), lambda j, i: (layer, 0, j + nblk)),
        ],
        out_specs=pl.BlockSpec((tm, tn), lambda j, i: (i, j)),
        out_shape=jax.ShapeDtypeStruct((t, D_FF), BF16),
        scratch_shapes=[pltpu.VMEM((k, tn), BF16), pltpu.VMEM((k, tn), BF16)],
        compiler_params=_params(("parallel", "arbitrary")),
        name="ffn_in",
    )(n, w, w)


def _mm_res_body(a_ref, w_ref, r_ref, o_ref, w_scr):
    @pl.when(pl.program_id(1) == 0)
    def _():
        for r in range(0, w_ref.shape[0], W_ROW_CHUNK):
            rs = slice(r, r + W_ROW_CHUNK)
            w_scr[rs, :] = w_ref[rs, :].astype(BF16)

    o_ref[...] = r_ref[...] + _dot(a_ref[...], w_scr[...])


def _matmul_residual(a, w, layer, res, tm, tn):
    t, k = a.shape
    n = w.shape[2]
    return pl.pallas_call(
        _mm_res_body,
        grid=(n // tn, t // tm),
        in_specs=[
            pl.BlockSpec((tm, k), lambda j, i: (i, 0)),
            pl.BlockSpec((None, k, tn), lambda j, i: (layer, 0, j)),
            pl.BlockSpec((tm, tn), lambda j, i: (i, j)),
        ],
        out_specs=pl.BlockSpec((tm, tn), lambda j, i: (i, j)),
        out_shape=jax.ShapeDtypeStruct((t, n), F32),
        scratch_shapes=[pltpu.VMEM((k, tn), BF16)],
        compiler_params=_params(("parallel", "arbitrary")),
        name="matmul_residual",
    )(a, w, res)


def _merge_body(ya_ref, yb_ref, yc_ref, ga_ref, gb_ref, gc_ref, wa_ref, wb_ref, wc_ref, o_ref):
    acc = jax.nn.sigmoid(ga_ref[...]) * _dot(ya_ref[...].astype(BF16), wa_ref[...])
    acc += jax.nn.sigmoid(gb_ref[...]) * _dot(yb_ref[...].astype(BF16), wb_ref[...])
    acc += jax.nn.sigmoid(gc_ref[...]) * _dot(yc_ref[...].astype(BF16), wc_ref[...])
    o_ref[...] = acc.astype(o_ref.dtype)


def _merge(ya, yb, yc, proj, wa, wb, wc, tm, tn):
    t = ya.shape[0]
    nblk = D_MODEL // tn
    g0 = C_GATE // tn
    return pl.pallas_call(
        _merge_body,
        grid=(t // tm, nblk),
        in_specs=[
            pl.BlockSpec((tm, GLA_V), lambda i, j: (i, 0)),
            pl.BlockSpec((tm, CM_WIDTH), lambda i, j: (i, 0)),
            pl.BlockSpec((tm, SSD_INNER), lambda i, j: (i, 0)),
            pl.BlockSpec((tm, tn), lambda i, j: (i, g0 + j)),
            pl.BlockSpec((tm, tn), lambda i, j: (i, g0 + nblk + j)),
            pl.BlockSpec((tm, tn), lambda i, j: (i, g0 + 2 * nblk + j)),
            pl.BlockSpec((GLA_V, tn), lambda i, j: (0, j)),
            pl.BlockSpec((CM_WIDTH, tn), lambda i, j: (0, j)),
            pl.BlockSpec((SSD_INNER, tn), lambda i, j: (0, j)),
        ],
        out_specs=pl.BlockSpec((tm, tn), lambda i, j: (i, j)),
        out_shape=jax.ShapeDtypeStruct((t, D_MODEL), BF16),
        compiler_params=_params(("parallel", "parallel")),
        name="merge",
    )(ya, yb, yc, proj, proj, proj, wa, wb, wc)


def _rms(x, g):
    ms = jnp.mean(x * x, axis=-1, keepdims=True)
    return x * lax.rsqrt(ms + EPS) * g


def _out_proj_body(mix_ref, w_ref, x_ref, g_ref, x1_ref, n_ref):
    x1 = x_ref[...] + _dot(mix_ref[...], w_ref[...])
    x1_ref[...] = x1
    n_ref[...] = _rms(x1, g_ref[...]).astype(n_ref.dtype)


def _out_proj(mix, w, x, g_next, tm):
    t, k = mix.shape
    row_block = lambda width: pl.BlockSpec((tm, width), lambda i: (i, 0))
    return pl.pallas_call(
        _out_proj_body,
        grid=(t // tm,),
        in_specs=[
            row_block(k),
            pl.BlockSpec((k, D_MODEL), lambda i: (0, 0)),
            row_block(D_MODEL),
            pl.BlockSpec((1, D_MODEL), lambda i: (0, 0)),
        ],
        out_specs=[row_block(D_MODEL), row_block(D_MODEL)],
        out_shape=[jax.ShapeDtypeStruct((t, D_MODEL), F32), jax.ShapeDtypeStruct((t, D_MODEL), BF16)],
        compiler_params=_params(("parallel",)),
        name="out_proj",
    )(mix, w, x, g_next)


def _ple_body(x_ref, g_ref, p_ref, wg_ref, wp_ref, gn_ref, *out_refs, final):
    x = x_ref[...]
    n = _rms(x, g_ref[...]).astype(BF16)
    gate = jax.nn.sigmoid(_dot(n, wg_ref[...]))
    emb = _dot(p_ref[...].astype(BF16), wp_ref[...])
    y = x + gate * emb
    if final:
        out_refs[0][...] = _rms(y, gn_ref[...])
    else:
        out_refs[0][...] = y
        out_refs[1][...] = _rms(y, gn_ref[...]).astype(BF16)


def _ple(x, g, p, wg, wp, g_next, final, tm):
    t = x.shape[0]
    row_block = lambda width: pl.BlockSpec((tm, width), lambda i: (i, 0))
    whole = lambda a: pl.BlockSpec(a.shape, lambda i: (0, 0))
    if final:
        out_specs = [row_block(D_MODEL)]
        out_shape = [jax.ShapeDtypeStruct((t, D_MODEL), F32)]
    else:
        out_specs = [row_block(D_MODEL), row_block(D_MODEL)]
        out_shape = [jax.ShapeDtypeStruct((t, D_MODEL), F32), jax.ShapeDtypeStruct((t, D_MODEL), BF16)]
    return pl.pallas_call(
        functools.partial(_ple_body, final=final),
        grid=(t // tm,),
        in_specs=[row_block(D_MODEL), whole(g), row_block(PLE_DIM), whole(wg), whole(wp), whole(g_next)],
        out_specs=out_specs,
        out_shape=out_shape,
        compiler_params=_params(("parallel",)),
        name="ple",
    )(x, g, p, wg, wp, g_next)


def _cmlp_body(u_ref, v_ref, lng_ref, lnb_ref, ws_ref, bst_ref, yb_ref, *rest, seq_len):
    u = jax.nn.gelu(u_ref[...])
    v = jax.nn.gelu(v_ref[...])
    mu = jnp.mean(v, axis=-1, keepdims=True)
    vc = v - mu
    var = jnp.mean(vc * vc, axis=-1, keepdims=True)
    vn = vc * lax.rsqrt(var + EPS) * lng_ref[...] + lnb_ref[...]
    if rest:
        rest[0][...] = vn
    r = lax.broadcasted_iota(jnp.int32, (CM_CHUNK, CM_CHUNK), 0)
    c = lax.broadcasted_iota(jnp.int32, (CM_CHUNK, CM_CHUNK), 1)
    if seq_len >= CM_CHUNK:
        keep = r >= c
    else:
        sh = seq_len.bit_length() - 1
        keep = ((r >> sh) == (c >> sh)) & ((r & (seq_len - 1)) >= (c & (seq_len - 1)))
    for g in range(CM_GROUPS):
        sl = slice(g * CM_GROUP_DIM, (g + 1) * CM_GROUP_DIM)
        w = jnp.where(keep, ws_ref[g], 0.0).astype(BF16)
        mixed = _dot(w, vn[:, sl].astype(BF16)) + bst_ref[:, g:g + 1]
        yb_ref[:, sl] = (u[:, sl] * mixed).astype(yb_ref.dtype)


def _cmlp(proj, ln_g, ln_b, ws_tiled, bs_t, seq_len, emit_v, out_dtype):
    t = proj.shape[0]
    out_shape = [jax.ShapeDtypeStruct((t, CM_WIDTH), out_dtype)]
    out_specs = [pl.BlockSpec((CM_CHUNK, CM_WIDTH), lambda i: (i, 0))]
    if emit_v:
        out_shape.append(jax.ShapeDtypeStruct((t, CM_WIDTH), F32))
        out_specs.append(pl.BlockSpec((CM_CHUNK, CM_WIDTH), lambda i: (i, 0)))
    cu = C_UV // CM_WIDTH
    res = pl.pallas_call(
        functools.partial(_cmlp_body, seq_len=seq_len),
        grid=(t // CM_CHUNK,),
        in_specs=[
            pl.BlockSpec((CM_CHUNK, CM_WIDTH), lambda i: (i, cu)),
            pl.BlockSpec((CM_CHUNK, CM_WIDTH), lambda i: (i, cu + 1)),
            pl.BlockSpec((1, CM_WIDTH), lambda i: (0, 0)),
            pl.BlockSpec((1, CM_WIDTH), lambda i: (0, 0)),
            pl.BlockSpec((CM_GROUPS, CM_CHUNK, CM_CHUNK), lambda i: (0, 0, 0)),
            pl.BlockSpec((CM_CHUNK, CM_GROUPS), lambda i: (0, 0)),
        ],
        out_specs=out_specs,
        out_shape=out_shape,
        compiler_params=_params(("parallel",)),
        name="chunk_mlp",
    )(proj, proj, ln_g, ln_b, ws_tiled, bs_t)
    return (res[0], res[1]) if emit_v else (res[0], None)


def _gla_level_constants(n):
    levels = n.bit_length() - 1
    t = np.arange(n)[:, None]
    j = np.arange(n)[None, :]
    sums = [j <= t, j > t]
    masks = [t == j]
    for l in range(1, levels + 1):
        w, half = 1 << l, 1 << (l - 1)
        start = (t >> l) << l
        upper = ((t >> (l - 1)) & 1) == 1
        a_up = (j >= start + half) & (j <= t)
        a_lo = (j > t) & (j < start + half)
        sums.append(np.where(upper, a_up, a_lo))
        jj = j
        upper_t = ((t >> (l - 1)) & 1) == 1
        lower_s = ((jj >> (l - 1)) & 1) == 0
        masks.append(((t >> l) == (jj >> l)) & upper_t & lower_s)
    return (jnp.asarray(np.stack(sums), BF16), jnp.asarray(np.stack(masks), F32), levels)


def _gla_log_decay(lr_ref, wlr_ref, blr_ref):
    x = _dot(lr_ref[...].astype(BF16), wlr_ref[...]) + blr_ref[...]
    return _log_sigmoid(x) * (1.0 / GLA_TAU)


def _gla_finish(o, g, gn):
    ms = jnp.mean(o * o, axis=-1, keepdims=True)
    return o * lax.rsqrt(ms + EPS) * gn * _silu(g)


def _gla_prompt_body(q_ref, k_ref, v_ref, g_ref, lr_ref, wlr_ref, blr_ref, gn_ref, a_ref, m_ref,
                     ya_ref, s_ref, *, levels):
    rows = q_ref.shape[0]

    @pl.when(pl.program_id(1) == 0)
    def _():
        s_ref[...] = jnp.zeros(s_ref.shape, F32)

    q = q_ref[...] * (GLA_DK ** -0.5)
    k = k_ref[...]
    la = _gla_log_decay(lr_ref, wlr_ref, blr_ref)
    la2 = jnp.concatenate(_split2(la), axis=1)

    def decay_sum(i):
        d = _dot(a_ref[i], la2)
        return d[:, :GLA_QK] + d[:, GLA_QK:]

    b = decay_sum(0)
    qe = (q * jnp.exp(b)).astype(BF16)
    kd = (k * jnp.exp(decay_sum(1))).astype(BF16)
    e_end_t = jnp.exp(jnp.broadcast_to(b[rows - 1:rows, :], (SUBLANES, GLA_QK))).T
    row = lax.broadcasted_iota(jnp.int32, (rows, GLA_QK), 0)
    xs = []
    for l in range(1, levels + 1):
        e = jnp.exp(decay_sum(l + 1))
        upper = ((row >> (l - 1)) & 1) == 1
        xs.append((jnp.where(upper, q, k) * e).astype(BF16))
    qb = q.astype(BF16)
    kb = k.astype(BF16)
    for h in range(GLA_HEADS):
        ks = slice(h * GLA_DK, (h + 1) * GLA_DK)
        vs = slice(h * GLA_DV, (h + 1) * GLA_DV)
        att = m_ref[0] * _nt(qb[:, ks], kb[:, ks])
        for l in range(1, levels + 1):
            x = xs[l - 1][:, ks]
            att += m_ref[l] * _nt(x, x)
        vh = v_ref[:, vs].astype(BF16)
        s = s_ref[0, h]
        o = _dot(qe[:, ks], s.astype(BF16)) + _dot(att.astype(BF16), vh)
        s_ref[0, h] = s * e_end_t[ks, 0:1] + _tn(kd[:, ks], vh)
        ya_ref[:, vs] = _gla_finish(o, g_ref[:, vs], gn_ref[:, vs]).astype(ya_ref.dtype)


def _gla_prompt(proj, small, bsz, length, w_lr, b_lr, gn):
    nchunk = length // CHUNK
    a_mats, masks, levels = _gla_level_constants(CHUNK)
    row = lambda b, c: b * nchunk + c
    return pl.pallas_call(
        functools.partial(_gla_prompt_body, levels=levels),
        grid=(bsz, nchunk),
        in_specs=[
            pl.BlockSpec((CHUNK, GLA_QK), lambda b, c: (row(b, c), C_Q // GLA_QK)),
            pl.BlockSpec((CHUNK, GLA_QK), lambda b, c: (row(b, c), C_K // GLA_QK)),
            pl.BlockSpec((CHUNK, GLA_V), lambda b, c: (row(b, c), C_V // GLA_V)),
            pl.BlockSpec((CHUNK, GLA_V), lambda b, c: (row(b, c), C_G // GLA_V)),
            pl.BlockSpec((CHUNK, LANES), lambda b, c: (row(b, c), SMALL_LR)),
            pl.BlockSpec((LANES, GLA_QK), lambda b, c: (0, 0)),
            pl.BlockSpec((1, GLA_QK), lambda b, c: (0, 0)),
            pl.BlockSpec((1, GLA_V), lambda b, c: (0, 0)),
            pl.BlockSpec(a_mats.shape, lambda b, c: (0, 0, 0)),
            pl.BlockSpec(masks.shape, lambda b, c: (0, 0, 0)),
        ],
        out_specs=[
            pl.BlockSpec((CHUNK, GLA_V), lambda b, c: (row(b, c), 0)),
            pl.BlockSpec((1, GLA_HEADS, GLA_DK, GLA_DV), lambda b, c: (b, 0, 0, 0)),
        ],
        out_shape=[
            jax.ShapeDtypeStruct((bsz * length, GLA_V), BF16),
            jax.ShapeDtypeStruct((bsz, GLA_HEADS, GLA_DK, GLA_DV), F32),
        ],
        compiler_params=_params(("parallel", "arbitrary")),
        name="gla_prompt",
    )(proj, proj, proj, proj, small, w_lr, b_lr, gn, a_mats, masks)


def _roll_rows(x, d):
    return pltpu.roll(x, d, 0) if d else x


def _gla_sample_body(q_ref, k_ref, v_ref, g_ref, lr_ref, wlr_ref, blr_ref, gn_ref, s0_ref,
                     ya_ref, s_ref):
    n = SAMPLE_LEN
    pos_k = lax.broadcasted_iota(jnp.int32, (SAMPLE_ROWS, GLA_QK), 0) & (n - 1)
    row_v = lax.broadcasted_iota(jnp.int32, (SAMPLE_ROWS, GLA_DV), 0)
    q = q_ref[...] * (GLA_DK ** -0.5)
    k = k_ref[...]
    v = v_ref[...]
    la = _gla_log_decay(lr_ref, wlr_ref, blr_ref)
    back = [_roll_rows(la, d) for d in range(n)]
    win = [None, la]
    for d in range(2, n):
        win.append(win[-1] + back[d - 1])
    b = la
    suffix = jnp.zeros_like(la)
    for d in range(1, n):
        b = b + jnp.where(pos_k >= d, back[d], 0.0)
        suffix = suffix + jnp.where(pos_k < n - d, pltpu.roll(la, SAMPLE_ROWS - d, 0), 0.0)
    eb = jnp.exp(b)
    qe = (q * eb).astype(BF16)
    kd = k * jnp.exp(suffix)
    eb_t = eb.T
    nseq = SAMPLE_ROWS // n
    seq_k = lax.broadcasted_iota(jnp.int32, (SAMPLE_ROWS, GLA_QK), 0) >> SAMPLE_SHIFT
    kd_seq = [jnp.where(seq_k == j, kd, 0.0).astype(BF16) for j in range(nseq)]
    pair = []
    for d in range(n):
        p = q * _roll_rows(k, d)
        if d:
            p = p * jnp.exp(win[d])
        pair.append(p)
    vback = [_roll_rows(v, d) for d in range(n)]
    for h in range(GLA_HEADS):
        ks = slice(h * GLA_DK, (h + 1) * GLA_DK)
        vs = slice(h * GLA_DV, (h + 1) * GLA_DV)
        o = jnp.zeros((SAMPLE_ROWS, GLA_DV), F32)
        for d in range(n):
            score = jnp.sum(pair[d][:, ks], axis=-1, keepdims=True)
            o = o + jnp.where((row_v & (n - 1)) >= d, score * vback[d][:, vs], 0.0)
        vh = v[:, vs].astype(BF16)
        for j in range(nseq):
            s = s0_ref[j, h]
            o = o + jnp.where((row_v >> SAMPLE_SHIFT) == j, _dot(qe[:, ks], s.astype(BF16)), 0.0)
            col = n * j + n - 1
            s_ref[j, h] = s * eb_t[ks, col:col + 1] + _tn(kd_seq[j][:, ks], vh)
        ya_ref[:, vs] = _gla_finish(o, g_ref[:, vs], gn_ref[:, vs]).astype(ya_ref.dtype)


def _stacked_state_call(body, prev_out, n_in):
    if prev_out is None:
        return body, [], [], {}
    wrapped = lambda *refs: body(*refs[:n_in], *refs[n_in + 1:])
    return wrapped, [pl.BlockSpec(memory_space=pl.ANY)], [prev_out], {n_in: 1}


def _gla_sample(proj, small, state_all, layer, prev_out, w_lr, b_lr, gn):
    t = proj.shape[0]
    nseq = SAMPLE_ROWS // SAMPLE_LEN
    state_spec = pl.BlockSpec((None, nseq, GLA_HEADS, GLA_DK, GLA_DV), lambda i: (layer, i, 0, 0, 0))
    body, extra_specs, extra_args, aliases = _stacked_state_call(_gla_sample_body, prev_out, 9)
    return pl.pallas_call(
        body,
        grid=(t // SAMPLE_ROWS,),
        in_specs=[
            pl.BlockSpec((SAMPLE_ROWS, GLA_QK), lambda i: (i, C_Q // GLA_QK)),
            pl.BlockSpec((SAMPLE_ROWS, GLA_QK), lambda i: (i, C_K // GLA_QK)),
            pl.BlockSpec((SAMPLE_ROWS, GLA_V), lambda i: (i, C_V // GLA_V)),
            pl.BlockSpec((SAMPLE_ROWS, GLA_V), lambda i: (i, C_G // GLA_V)),
            pl.BlockSpec((SAMPLE_ROWS, LANES), lambda i: (i, SMALL_LR)),
            pl.BlockSpec((LANES, GLA_QK), lambda i: (0, 0)),
            pl.BlockSpec((1, GLA_QK), lambda i: (0, 0)),
            pl.BlockSpec((1, GLA_V), lambda i: (0, 0)),
            state_spec,
        ] + extra_specs,
        out_specs=[pl.BlockSpec((SAMPLE_ROWS, GLA_V), lambda i: (i, 0)), state_spec],
        out_shape=[
            jax.ShapeDtypeStruct((t, GLA_V), F32),
            jax.ShapeDtypeStruct(state_all.shape, F32),
        ],
        input_output_aliases=aliases,
        compiler_params=_params(("parallel",)),
        name="gla_sample",
    )(proj, proj, proj, proj, small, w_lr, b_lr, gn, state_all, *extra_args)


XBC_PART = 2048


def _ssd_gate_norm(y, z, gn):
    yz = y * _silu(z)
    ms = jnp.mean(yz * yz, axis=-1, keepdims=True)
    return yz * lax.rsqrt(ms + EPS) * gn


def _ssd_prompt_body(x1_ref, x2_ref, bc_ref, z1_ref, z2_ref, dt_ref, cw_ref, cb_ref, dtb_ref, alog_ref,
                     dx_ref, gn_ref, e_ref, a_ref, hm_ref, yc_ref, h_ref, tail_scr, ht_scr):
    rows = x1_ref.shape[0]
    step = pl.program_id(1)

    @pl.when(step == 0)
    def _():
        tail_scr[...] = jnp.zeros(tail_scr.shape, F32)
        ht_scr[...] = jnp.zeros(ht_scr.shape, F32)

    row8 = lax.broadcasted_iota(jnp.int32, (SUBLANES, XBC_PART), 0)

    def conv(x_ref, part):
        cs = slice(part * XBC_PART, (part + 1) * XBC_PART)
        x = x_ref[...]
        prev = tail_scr[:, cs]
        acc = cb_ref[:, cs] + x * cw_ref[SSD_CONV - 1:SSD_CONV, cs]
        for d in range(1, SSD_CONV):
            xr = pltpu.roll(x, d, 0)
            first = jnp.where(row8 < d, pltpu.roll(prev, d, 0), xr[:SUBLANES])
            xd = jnp.concatenate([first, xr[SUBLANES:]], axis=0)
            acc = acc + xd * cw_ref[SSD_CONV - 1 - d:SSD_CONV - d, cs]
        tail_scr[:, cs] = x[rows - SUBLANES:, :]
        return _silu(acc)

    xs_halves = [conv(x1_ref, 0), conv(x2_ref, 1)]
    bc = conv(bc_ref, 2)
    half_w = SSD_GROUPS * SSD_STATE
    bmat = bc[:, :half_w]
    cmat = bc[:, half_w:]
    z_halves = [z1_ref, z2_ref]

    dt = _softplus(dt_ref[...] + dtb_ref[...])
    a = dt * (-jnp.exp(alog_ref[...]))
    a2 = jnp.concatenate(_split2(a), axis=1)

    def head_sum(i):
        d = _dot(a_ref[i], a2)
        return d[:, :LANES] + d[:, LANES:]

    cs_in = head_sum(0)
    cs_suf = head_sum(1)
    cs_t = cs_in.T
    cs_hi, cs_lo = _split2(cs_in)
    suf_hi, suf_lo = _split2(cs_suf)
    dt_hi, dt_lo = _split2(dt)
    r_i = lax.broadcasted_iota(jnp.int32, (rows, rows), 0)
    c_i = lax.broadcasted_iota(jnp.int32, (rows, rows), 1)
    causal = r_i >= c_i
    heads_per_group = SSD_HEADS // SSD_GROUPS
    per_half = SSD_GROUPS // 2
    for g in range(SSD_GROUPS):
        gs = slice(g * SSD_GROUP_WIDTH, (g + 1) * SSD_GROUP_WIDTH)
        ls = slice((g % per_half) * SSD_GROUP_WIDTH, (g % per_half + 1) * SSD_GROUP_WIDTH)
        ns = slice(g * SSD_STATE, (g + 1) * SSD_STATE)
        xg = xs_halves[g // per_half][:, ls]
        eg = e_ref[:, gs]
        csx = _dot(cs_hi, eg) + _dot(cs_lo, eg)
        sufx = _dot(suf_hi, eg) + _dot(suf_lo, eg)
        dtx = _dot(dt_hi, eg) + _dot(dt_lo, eg)
        bg = bmat[:, ns].astype(BF16)
        cg = cmat[:, ns].astype(BF16)
        cb = jnp.where(causal, _nt(cg, bg), 0.0)
        xdt = dtx * xg
        xdt_b = xdt.astype(BF16)
        ws = []
        xb = []
        for r in range(heads_per_group):
            h = g * heads_per_group + r
            dm = cs_in[:, h:h + 1] - cs_t[h:h + 1, :]
            ws.append((cb * jnp.exp(jnp.minimum(dm, 0.0))).astype(BF16))
            xb.append(xdt_b * hm_ref[r])
        y = _dot(jnp.concatenate(ws, axis=1), jnp.concatenate(xb, axis=0))
        ht = ht_scr[g]
        y = y + _dot(cg, ht.astype(BF16)) * jnp.exp(csx)
        y = y + dx_ref[:, gs] * xg
        wx = (jnp.exp(sufx) * xdt).astype(BF16)
        ht_scr[g] = ht * jnp.exp(csx[rows - 1:rows, :]) + _tn(bg, wx)
        z = z_halves[g // per_half][:, ls]
        yc_ref[:, gs] = _ssd_gate_norm(y, z, gn_ref[:, gs]).astype(yc_ref.dtype)

    @pl.when(step == pl.num_programs(1) - 1)
    def _():
        for g in range(SSD_GROUPS):
            h_ref[0, g] = ht_scr[g].T


def _tri_constants(n):
    t = np.arange(n)[:, None]
    j = np.arange(n)[None, :]
    return jnp.asarray(np.stack([j <= t, j > t]), BF16)


def _head_lane_masks(rows):
    lane = np.arange(SSD_GROUP_WIDTH)[None, None, :]
    r = np.arange(SSD_HEADS // SSD_GROUPS)[:, None, None]
    return jnp.asarray(np.broadcast_to(lane // SSD_HEADDIM == r, (SSD_HEADS // SSD_GROUPS, rows, SSD_GROUP_WIDTH)), BF16)


def _head_expand_matrix():
    h = np.arange(LANES)[:, None]
    lane = np.arange(SSD_INNER)[None, :]
    return jnp.asarray(h == lane // SSD_HEADDIM, BF16)


def _ssd_prompt(proj, small, bsz, length, conv_w, conv_b, dt_bias, a_log, d_x, gn):
    nchunk = length // CHUNK
    row = lambda b, c: b * nchunk + c
    cx = C_XBC // XBC_PART
    cz = C_Z // XBC_PART
    tri = _tri_constants(CHUNK)
    e = _head_expand_matrix()
    head_masks = _head_lane_masks(CHUNK)
    full = lambda shape: pl.BlockSpec(shape, lambda b, c: (0,) * len(shape))
    return pl.pallas_call(
        _ssd_prompt_body,
        grid=(bsz, nchunk),
        in_specs=[
            pl.BlockSpec((CHUNK, XBC_PART), lambda b, c: (row(b, c), cx)),
            pl.BlockSpec((CHUNK, XBC_PART), lambda b, c: (row(b, c), cx + 1)),
            pl.BlockSpec((CHUNK, XBC_PART), lambda b, c: (row(b, c), cx + 2)),
            pl.BlockSpec((CHUNK, XBC_PART), lambda b, c: (row(b, c), cz)),
            pl.BlockSpec((CHUNK, XBC_PART), lambda b, c: (row(b, c), cz + 1)),
            pl.BlockSpec((CHUNK, LANES), lambda b, c: (row(b, c), SMALL_DT)),
            full((SSD_CONV, SSD_CONV_DIM)),
            full((1, SSD_CONV_DIM)),
            full((1, LANES)),
            full((1, LANES)),
            full((1, SSD_INNER)),
            full((1, SSD_INNER)),
            full((LANES, SSD_INNER)),
            full(tri.shape),
            full(head_masks.shape),
        ],
        out_specs=[
            pl.BlockSpec((CHUNK, SSD_INNER), lambda b, c: (row(b, c), 0)),
            pl.BlockSpec((1, SSD_GROUPS, SSD_GROUP_WIDTH, SSD_STATE), lambda b, c: (b, 0, 0, 0)),
        ],
        out_shape=[
            jax.ShapeDtypeStruct((bsz * length, SSD_INNER), BF16),
            jax.ShapeDtypeStruct((bsz, SSD_GROUPS, SSD_GROUP_WIDTH, SSD_STATE), F32),
        ],
        scratch_shapes=[
            pltpu.VMEM((SUBLANES, SSD_CONV_DIM), F32),
            pltpu.VMEM((SSD_GROUPS, SSD_STATE, SSD_GROUP_WIDTH), F32),
        ],
        compiler_params=_params(("parallel", "arbitrary")),
        name="ssd_prompt",
    )(proj, proj, proj, proj, proj, small, conv_w, conv_b, dt_bias, a_log, d_x, gn, e, tri, head_masks)


def _ssd_sample_body(x1_ref, x2_ref, bc_ref, z1_ref, z2_ref, dt_ref, cw_ref, cb_ref, dtb_ref, alog_ref,
                     dx_ref, gn_ref, e_ref, prev_ref, h0_ref, yc_ref, h_ref):
    n = SAMPLE_LEN
    nseq = SAMPLE_ROWS // n
    pos_p = lax.broadcasted_iota(jnp.int32, (SAMPLE_ROWS, XBC_PART), 0) & (n - 1)

    def conv(x_ref, part):
        cs = slice(part * XBC_PART, (part + 1) * XBC_PART)
        x = x_ref[...]
        prev = prev_ref[0][:, cs]
        acc = cb_ref[:, cs] + x * cw_ref[SSD_CONV - 1:SSD_CONV, cs]
        for d in range(1, SSD_CONV):
            xd = jnp.where(pos_p < d, pltpu.roll(prev, d, 0), pltpu.roll(x, d, 0))
            acc = acc + xd * cw_ref[SSD_CONV - 1 - d:SSD_CONV - d, cs]
        return _silu(acc)

    xs = jnp.concatenate([conv(x1_ref, 0), conv(x2_ref, 1)], axis=1)
    bc = conv(bc_ref, 2)
    half_w = SSD_GROUPS * SSD_STATE
    bmat = bc[:, :half_w]
    cmat = bc[:, half_w:]
    z = jnp.concatenate([z1_ref[...], z2_ref[...]], axis=1)

    dt = _softplus(dt_ref[...] + dtb_ref[...])
    dt_hi, dt_lo = _split2(dt)
    dtx = _dot(dt_hi, e_ref[...]) + _dot(dt_lo, e_ref[...])
    a_hi, a_lo = _split2(-jnp.exp(alog_ref[...]))
    anegx = _dot(jnp.broadcast_to(a_hi, (SUBLANES, LANES)), e_ref[...]) + \
        _dot(jnp.broadcast_to(a_lo, (SUBLANES, LANES)), e_ref[...])
    ax = dtx * anegx
    pos = lax.broadcasted_iota(jnp.int32, (SAMPLE_ROWS, SSD_INNER), 0) & (n - 1)
    seq_g = lax.broadcasted_iota(jnp.int32, (SAMPLE_ROWS, SSD_GROUP_WIDTH), 0) >> SAMPLE_SHIFT
    back = [_roll_rows(ax, d) for d in range(n)]
    win = [None, ax]
    for d in range(2, n):
        win.append(win[-1] + back[d - 1])
    csx = ax
    sufx = jnp.zeros_like(ax)
    for d in range(1, n):
        csx = csx + jnp.where(pos >= d, back[d], 0.0)
        sufx = sufx + jnp.where(pos < n - d, pltpu.roll(ax, SAMPLE_ROWS - d, 0), 0.0)
    ecs = jnp.exp(csx)
    ecs_t = ecs.T
    wx = jnp.exp(sufx) * dtx * xs

    y = dx_ref[...] * xs
    for d in range(n):
        prod = cmat * _roll_rows(bmat, d)
        cbx = jnp.concatenate(
            [jnp.broadcast_to(jnp.sum(prod[:, g * SSD_STATE:(g + 1) * SSD_STATE], axis=-1, keepdims=True),
                              (SAMPLE_ROWS, SSD_GROUP_WIDTH)) for g in range(SSD_GROUPS)], axis=1)
        term = cbx * _roll_rows(dtx, d) * _roll_rows(xs, d)
        if d:
            term = term * jnp.exp(win[d])
        y = y + jnp.where(pos >= d, term, 0.0)

    y_inter = []
    for g in range(SSD_GROUPS):
        gs = slice(g * SSD_GROUP_WIDTH, (g + 1) * SSD_GROUP_WIDTH)
        ns = slice(g * SSD_STATE, (g + 1) * SSD_STATE)
        bg = bmat[:, ns].astype(BF16)
        cg = cmat[:, ns].astype(BF16)
        acc = jnp.zeros((SAMPLE_ROWS, SSD_GROUP_WIDTH), F32)
        for j in range(nseq):
            h0 = h0_ref[j, g]
            acc = acc + jnp.where(seq_g == j, _nt(cg, h0.astype(BF16)), 0.0)
            wxj = jnp.where(seq_g == j, wx[:, gs], 0.0).astype(BF16)
            col = n * j + n - 1
            h_ref[j, g] = h0 * ecs_t[gs, col:col + 1] + _tn(wxj, bg)
        y_inter.append(acc)
    y = y + jnp.concatenate(y_inter, axis=1) * ecs
    for g in range(SSD_GROUPS):
        gs = slice(g * SSD_GROUP_WIDTH, (g + 1) * SSD_GROUP_WIDTH)
        yc_ref[:, gs] = _ssd_gate_norm(y[:, gs], z[:, gs], gn_ref[:, gs]).astype(yc_ref.dtype)


def _ssd_sample(proj, small, conv_prev, state_all, layer, prev_out, conv_w, conv_b, dt_bias, a_log, d_x, gn):
    t = proj.shape[0]
    nseq = SAMPLE_ROWS // SAMPLE_LEN
    cx = C_XBC // XBC_PART
    cz = C_Z // XBC_PART
    e = _head_expand_matrix()
    full = lambda shape: pl.BlockSpec(shape, lambda i: (0,) * len(shape))
    state_spec = pl.BlockSpec((None, nseq, SSD_GROUPS, SSD_GROUP_WIDTH, SSD_STATE), lambda i: (layer, i, 0, 0, 0))
    body, extra_specs, extra_args, aliases = _stacked_state_call(_ssd_sample_body, prev_out, 15)
    return pl.pallas_call(
        body,
        grid=(t // SAMPLE_ROWS,),
        in_specs=[
            pl.BlockSpec((SAMPLE_ROWS, XBC_PART), lambda i: (i, cx)),
            pl.BlockSpec((SAMPLE_ROWS, XBC_PART), lambda i: (i, cx + 1)),
            pl.BlockSpec((SAMPLE_ROWS, XBC_PART), lambda i: (i, cx + 2)),
            pl.BlockSpec((SAMPLE_ROWS, XBC_PART), lambda i: (i, cz)),
            pl.BlockSpec((SAMPLE_ROWS, XBC_PART), lambda i: (i, cz + 1)),
            pl.BlockSpec((SAMPLE_ROWS, LANES), lambda i: (i, SMALL_DT)),
            full((SSD_CONV, SSD_CONV_DIM)),
            full((1, SSD_CONV_DIM)),
            full((1, LANES)),
            full((1, LANES)),
            full((1, SSD_INNER)),
            full((1, SSD_INNER)),
            full((LANES, SSD_INNER)),
            pl.BlockSpec((1, SAMPLE_ROWS, SSD_CONV_DIM), lambda i: (i, 0, 0)),
            state_spec,
        ] + extra_specs,
        out_specs=[pl.BlockSpec((SAMPLE_ROWS, SSD_INNER), lambda i: (i, 0)), state_spec],
        out_shape=[
            jax.ShapeDtypeStruct((t, SSD_INNER), F32),
            jax.ShapeDtypeStruct(state_all.shape, F32),
        ],
        input_output_aliases=aliases,
        compiler_params=_params(("parallel",)),
        name="ssd_sample",
    )(proj, proj, proj, proj, proj, small, conv_w, conv_b, dt_bias, a_log, d_x, gn, e, conv_prev, state_all,
      *extra_args)


def _narrow_w_in_t(w_in_t, layer):
    k = w_in_t.shape[2]
    rows = lambda a, b: lax.slice(w_in_t, (layer, a, 0), (layer + 1, b, k)).reshape(b - a, k)
    lr = rows(NATIVE_UV - GLA_LOWRANK, NATIVE_UV)
    dt = rows(NATIVE_DT, NATIVE_GATE)
    pad = lambda n: jnp.zeros((n, k), w_in_t.dtype)
    return jnp.concatenate([lr, pad(LANES - GLA_LOWRANK), dt, pad(LANES - SSD_HEADS)], axis=0)


def _pad_lanes(v, n):
    return jnp.pad(v, (0, n - v.shape[0])).reshape(1, n)


def _layer_weights(i, w_in_t, g_next, g_mix, w_in, w_gla_lr, b_gla_lr, g_gla_norm, cm_ln_g, cm_ln_b, cm_ws,
                   cm_bs, ssd_conv_w, ssd_conv_b, ssd_dt_bias, ssd_a_log, ssd_d, g_ssd_norm,
                   w_br_gla, w_br_cm, w_br_ssd, w_o, g_ffn, w_ffn_in, w_ffn_out,
                   g_ple, w_ple_gate, w_ple_proj):
    row = lambda v: v[i].reshape(1, -1)
    n_tile = CM_CHUNK // SAMPLE_LEN
    return dict(
        w_in_t=w_in_t,
        w_in_narrow=_narrow_w_in_t(w_in_t, i),
        layer=i,
        g_next=g_next.reshape(1, -1),
        w_lr=jnp.pad(w_gla_lr[i], ((0, LANES - GLA_LOWRANK), (0, 0))).astype(BF16),
        b_lr=row(b_gla_lr),
        g_gla=row(g_gla_norm),
        ln_g=row(cm_ln_g),
        ln_b=row(cm_ln_b),
        ws_prompt=cm_ws[i],
        bs_prompt=cm_bs[i].T,
        ws_sample=jnp.tile(cm_ws[i][:, :SAMPLE_LEN, :SAMPLE_LEN], (1, n_tile, n_tile)),
        bs_sample=jnp.tile(cm_bs[i][:, :SAMPLE_LEN].T, (n_tile, 1)),
        conv_w=ssd_conv_w[i],
        conv_b=row(ssd_conv_b),
        dt_bias=_pad_lanes(ssd_dt_bias[i], LANES),
        a_log=_pad_lanes(ssd_a_log[i], LANES),
        d_x=jnp.repeat(ssd_d[i], SSD_HEADDIM).reshape(1, SSD_INNER),
        g_ssd=row(g_ssd_norm),
        w_br_gla=w_br_gla[i].astype(BF16),
        w_br_cm=w_br_cm[i].astype(BF16),
        w_br_ssd=w_br_ssd[i].astype(BF16),
        w_o=w_o[i].astype(BF16),
        g_ffn=row(g_ffn),
        w_ffn_in=w_ffn_in,
        w_ffn_out=w_ffn_out,
        g_ple=row(g_ple),
        w_ple_gate=w_ple_gate[i].astype(BF16),
        w_ple_proj=w_ple_proj[i].astype(BF16),
    )


def _dense_tail(x, p, proj, ya, yb, yc, w, tm_ws, final):
    mix = _merge(ya, yb, yc, proj, w["w_br_gla"], w["w_br_cm"], w["w_br_ssd"], 512, 512)
    x, n = _out_proj(mix, w["w_o"], x, w["g_ffn"], 256)
    h = _ffn_in(n, w["w_ffn_in"], w["layer"], tm_ws, 512)
    x = _matmul_residual(h, w["w_ffn_out"], w["layer"], x, 512, 512)
    return _ple(x, w["g_ple"], p, w["w_ple_gate"], w["w_ple_proj"], w["g_next"], final, 256)


def _input_projection(n, w, tm):
    return _proj_matmul(n, w["w_in_t"], w["layer"], tm), _matmul_small_nt(n, w["w_in_narrow"], 512)


def _last_conv_rows(proj, bsz, length):
    keep = SSD_CONV - 1
    if length % SUBLANES == 0:
        rows = proj.reshape(bsz, length, proj.shape[-1])[:, length - keep:]
        return rows[:, :, C_XBC:C_XBC + SSD_CONV_DIM]
    xbc = proj[:, C_XBC:C_XBC + SSD_CONV_DIM]
    return xbc.reshape(bsz, length, SSD_CONV_DIM)[:, length - keep:]


def _prompt_layer(x, n, p, bsz, length, w, final):
    proj, small = _input_projection(n, w, 1024)
    ya, s_gla = _gla_prompt(proj, small, bsz, length, w["w_lr"], w["b_lr"], w["g_gla"])
    yb, _ = _cmlp(proj, w["ln_g"], w["ln_b"], w["ws_prompt"], w["bs_prompt"], CM_CHUNK, False, BF16)
    yc, s_ssm = _ssd_prompt(proj, small, bsz, length, w["conv_w"], w["conv_b"], w["dt_bias"], w["a_log"],
                            w["d_x"], w["g_ssd"])
    s_conv = _last_conv_rows(proj, bsz, length)
    outs = _dense_tail(x, p, proj, ya, yb, yc, w, 1024, final)
    s_ssm = s_ssm.reshape(bsz, SSD_HEADS, SSD_HEADDIM, SSD_STATE)
    return outs, s_gla, s_ssm, s_conv


def _sample_layer(x, n_in, p, bsz, gla_all, ssm_all, s_conv, prev_gla, prev_ssm, w, final):
    n = SAMPLE_LEN
    proj, small = _input_projection(n_in, w, 512)
    ya, gla_out = _gla_sample(proj, small, gla_all, w["layer"], prev_gla, w["w_lr"], w["b_lr"], w["g_gla"])
    yb, v_rows = _cmlp(proj, w["ln_g"], w["ln_b"], w["ws_sample"], w["bs_sample"], n, True, F32)
    sc = s_conv.reshape(bsz // 2, 2, SSD_CONV - 1, SSD_CONV_DIM)
    zrow = jnp.zeros((bsz // 2, 1, SSD_CONV_DIM), F32)
    conv_prev = jnp.concatenate([zrow, sc[:, 1], zrow, sc[:, 0]], axis=1)
    yc, ssm_out = _ssd_sample(proj, small, conv_prev, ssm_all, w["layer"], prev_ssm, w["conv_w"], w["conv_b"],
                              w["dt_bias"], w["a_log"], w["d_x"], w["g_ssd"])
    s_conv_new = _last_conv_rows(proj, bsz, n)
    outs = _dense_tail(x, p, proj, ya, yb, yc, w, 512, final)
    return outs, gla_out, ssm_out, s_conv_new, v_rows.reshape(bsz, n, CM_WIDTH)


def kernel(x_prompt, x_sample, state_gla, state_ssm, state_conv, p_prompt, p_sample, g_mix, w_in, w_gla_lr, b_gla_lr, g_gla_norm, cm_ln_g, cm_ln_b, cm_ws, cm_bs, ssd_conv_w, ssd_conv_b, ssd_dt_bias, ssd_a_log, ssd_d, g_ssd_norm, w_br_gla, w_br_cm, w_br_ssd, w_o, g_ffn, w_ffn_in, w_ffn_out, g_ple, w_ple_gate, w_ple_proj, g_final):
    weights = (g_mix, w_in, w_gla_lr, b_gla_lr, g_gla_norm, cm_ln_g, cm_ln_b, cm_ws, cm_bs,
               ssd_conv_w, ssd_conv_b, ssd_dt_bias, ssd_a_log, ssd_d, g_ssd_norm,
               w_br_gla, w_br_cm, w_br_ssd, w_o, g_ffn, w_ffn_in, w_ffn_out,
               g_ple, w_ple_gate, w_ple_proj)
    bp, lp, _ = x_prompt.shape
    bs, ls, _ = x_sample.shape
    assert ls == SAMPLE_LEN and lp % CHUNK == 0 and bs % 2 == 0
    xp = x_prompt.reshape(bp * lp, D_MODEL)
    xs = x_sample.reshape(bs * ls, D_MODEL)
    g_first = g_mix[0].reshape(1, D_MODEL)
    np_ = _rms_cast(xp, g_first, 512, BF16)
    ns = _rms_cast(xs, g_first, 512, BF16)
    w_in_t = jnp.swapaxes(w_in, 1, 2)
    gla_p, ssm_p, conv_p = [], [], []
    conv_s, v_s = [], []
    ssm_all = state_ssm.reshape(DEPTH, bs, SSD_GROUPS, SSD_GROUP_WIDTH, SSD_STATE)
    gla_s = ssm_s = None
    for i in range(DEPTH):
        final = i == DEPTH - 1
        g_next = g_final if final else g_mix[i + 1]
        w = _layer_weights(i, w_in_t, g_next, *weights)
        outs, sg, sm, sc = _prompt_layer(xp, np_, p_prompt[i].reshape(bp * lp, PLE_DIM), bp, lp, w, final)
        gla_p.append(sg)
        ssm_p.append(sm)
        conv_p.append(sc)
        xp, np_ = (outs[0], None) if final else outs
        outs, gla_s, ssm_s, sc, vr = _sample_layer(xs, ns, p_sample[i].reshape(bs * ls, PLE_DIM), bs,
                                                   state_gla, ssm_all, state_conv[i], gla_s, ssm_s, w, final)
        conv_s.append(sc)
        v_s.append(vr)
        xs, ns = (outs[0], None) if final else outs
    y_prompt = xp.reshape(bp, lp, D_MODEL)
    y_sample = xs.reshape(bs, ls, D_MODEL)
    ssm_s = ssm_s.reshape(DEPTH, bs, SSD_HEADS, SSD_HEADDIM, SSD_STATE)
    return (y_prompt, y_sample, jnp.stack(gla_p), gla_s, jnp.stack(ssm_p), ssm_s,
            jnp.stack(conv_p), jnp.stack(conv_s), jnp.stack(v_s))
```

```python
import functools

import numpy as np
import jax
import jax.numpy as jnp
from jax import lax
from jax.experimental import pallas as pl
from jax.experimental.pallas import tpu as pltpu

F32 = jnp.float32
BF16 = jnp.bfloat16

D_MODEL = 2048
DEPTH = 2
GLA_HEADS = 4
GLA_DK = 256
GLA_DV = 512
GLA_QK = 1024
GLA_V = 2048
GLA_LOWRANK = 16
GLA_TAU = 16.0
CM_GROUPS = 8
CM_WIDTH = 2048
CM_GROUP_DIM = 256
CM_CHUNK = 128
SSD_INNER = 4096
SSD_HEADDIM = 64
SSD_HEADS = 64
SSD_GROUPS = 8
SSD_STATE = 128
SSD_CONV = 4
SSD_CONV_DIM = 6144
SSD_GROUP_WIDTH = SSD_INNER // SSD_GROUPS
D_FF = 5632
PLE_DIM = 256
EPS = 1e-6

LANES = 128
SUBLANES = 8
VMEM_LIMIT = 52 * 1024 * 1024

C_Q = 0
C_K = 1024
C_V = 2048
C_G = 4096
C_UV = 6144
C_Z = 10240
C_XBC = 14336
C_GATE = 20480
C_LR = 26624
SMALL_LR = 0
SMALL_DT = 1

CHUNK = 128
SAMPLE_ROWS = 8
SAMPLE_LEN = 4
SAMPLE_SHIFT = 2
HEADDIM_SHIFT = 6


def _params(sem):
    return pltpu.CompilerParams(dimension_semantics=sem, vmem_limit_bytes=VMEM_LIMIT)


def _nt(a, b):
    return lax.dot_general(a, b, (((1,), (1,)), ((), ())), preferred_element_type=F32)


def _tn(a, b):
    return lax.dot_general(a, b, (((0,), (0,)), ((), ())), preferred_element_type=F32)


def _dot(a, b):
    return jnp.dot(a, b, preferred_element_type=F32)


def _split2(x):
    hi = x.astype(BF16)
    lo = (x - hi.astype(F32)).astype(BF16)
    return hi, lo


def _softplus(x):
    return jnp.maximum(x, 0.0) + jnp.log(1.0 + jnp.exp(-jnp.abs(x)))


def _log_sigmoid(x):
    return jnp.minimum(x, 0.0) - jnp.log(1.0 + jnp.exp(-jnp.abs(x)))


def _silu(x):
    return x * jax.nn.sigmoid(x)


def _rms_cast_body(x_ref, g_ref, o_ref):
    x = x_ref[...]
    ms = jnp.mean(x * x, axis=-1, keepdims=True)
    o_ref[...] = (x * lax.rsqrt(ms + EPS) * g_ref[...]).astype(o_ref.dtype)


def _rms_cast(x, g, tm, dtype):
    t, k = x.shape
    return pl.pallas_call(
        _rms_cast_body,
        grid=(t // tm,),
        in_specs=[pl.BlockSpec((tm, k), lambda i: (i, 0)), pl.BlockSpec((1, k), lambda i: (0, 0))],
        out_specs=pl.BlockSpec((tm, k), lambda i: (i, 0)),
        out_shape=jax.ShapeDtypeStruct((t, k), dtype),
        compiler_params=_params(("parallel",)),
        name="rms_cast",
    )(x, g)


NATIVE_UV = 2 * GLA_QK + 2 * GLA_V + GLA_LOWRANK
NATIVE_DT = NATIVE_UV + 2 * CM_WIDTH + SSD_INNER + SSD_CONV_DIM
NATIVE_GATE = NATIVE_DT + SSD_HEADS
PROJ_TN = 1024
W_ROW_CHUNK = 256


def _on_row_streams(prompt_step, sample_step):
    i = pl.program_id(1)
    last = pl.num_programs(1) - 1
    pl.when(i < last)(prompt_step)
    pl.when(i == last)(sample_step)


def _row_stream_specs(rows_p, tm, rows_s, width_in, width_out):
    n_p = rows_p // tm
    p_row = lambda i: jnp.minimum(i, n_p - 1)
    in_specs = [pl.BlockSpec((tm, width_in), lambda j, i: (p_row(i), 0)),
                pl.BlockSpec((rows_s, width_in), lambda j, i: (0, 0))]
    out_specs = [pl.BlockSpec((tm, width_out), lambda j, i: (p_row(i), j)),
                 pl.BlockSpec((rows_s, width_out), lambda j, i: (0, j))]
    return n_p + 1, in_specs, out_specs


def _proj_body(np_ref, ns_ref, w_ref, wn_ref, op_ref, os_ref, w_scr):
    j = pl.program_id(0)
    tn = w_ref.shape[0]

    def load_weights(shift):
        for r in range(0, tn - shift, W_ROW_CHUNK):
            rows = min(W_ROW_CHUNK, tn - shift - r)
            w_scr[r:r + rows, :] = w_ref[r + shift:r + shift + rows, :].astype(BF16)
        if shift:
            w_scr[tn - shift:tn, :] = wn_ref[0:shift, :].astype(BF16)

    @pl.when(pl.program_id(1) == 0)
    def _():
        j_uv = C_UV // PROJ_TN
        j_gate = C_GATE // PROJ_TN
        pl.when(j < j_uv)(lambda: load_weights(0))
        pl.when((j >= j_uv) & (j < j_gate))(lambda: load_weights(NATIVE_UV - C_UV))
        pl.when(j >= j_gate)(lambda: load_weights(NATIVE_GATE - C_GATE))

    def prompt_step():
        op_ref[...] = _nt(np_ref[...], w_scr[...])

    def sample_step():
        os_ref[...] = _nt(ns_ref[...], w_scr[...])

    _on_row_streams(prompt_step, sample_step)


def _proj_matmul(n_p, n_s, w_in_t, layer, tm):
    k = n_p.shape[1]
    tn = PROJ_TN
    row_steps, in_specs, out_specs = _row_stream_specs(n_p.shape[0], tm, n_s.shape[0], k, tn)
    return pl.pallas_call(
        _proj_body,
        grid=(C_LR // tn, row_steps),
        in_specs=in_specs + [
            pl.BlockSpec((None, tn, k), lambda j, i: (layer, j, 0)),
            pl.BlockSpec((None, LANES, k), lambda j, i: (layer, (j + 1) * (tn // LANES), 0)),
        ],
        out_specs=out_specs,
        out_shape=[jax.ShapeDtypeStruct((n_p.shape[0], C_LR), F32), jax.ShapeDtypeStruct((n_s.shape[0], C_LR), F32)],
        scratch_shapes=[pltpu.VMEM((tn, k), BF16)],
        compiler_params=_params(("parallel", "arbitrary")),
        name="proj_matmul",
    )(n_p, n_s, w_in_t, w_in_t)


def _mm_nt_body(a_ref, w_ref, o_ref):
    o_ref[...] = _nt(a_ref[...], w_ref[...].astype(BF16))


def _matmul_small_nt(a, w_t, tm):
    t, k = a.shape
    n = w_t.shape[0]
    return pl.pallas_call(
        _mm_nt_body,
        grid=(t // tm,),
        in_specs=[pl.BlockSpec((tm, k), lambda i: (i, 0)), pl.BlockSpec((n, k), lambda i: (0, 0))],
        out_specs=pl.BlockSpec((tm, n), lambda i: (i, 0)),
        out_shape=jax.ShapeDtypeStruct((t, n), F32),
        compiler_params=_params(("parallel",)),
        name="matmul_small",
    )(a, w_t)


def _ffn_in_body(np_ref, ns_ref, wg_ref, wu_ref, op_ref, os_ref, wg_scr, wu_scr):
    @pl.when(pl.program_id(1) == 0)
    def _():
        for r in range(0, wg_ref.shape[0], W_ROW_CHUNK):
            rs = slice(r, r + W_ROW_CHUNK)
            wg_scr[rs, :] = wg_ref[rs, :].astype(BF16)
            wu_scr[rs, :] = wu_ref[rs, :].astype(BF16)

    def swiglu(n_ref, o_ref):
        n = n_ref[...]
        gate = _dot(n, wg_scr[...])
        up = _dot(n, wu_scr[...])
        o_ref[...] = (_silu(gate) * up).astype(o_ref.dtype)

    _on_row_streams(lambda: swiglu(np_ref, op_ref), lambda: swiglu(ns_ref, os_ref))


def _ffn_in(n_p, n_s, w, layer, tm, tn):
    k = n_p.shape[1]
    nblk = D_FF // tn
    row_steps, in_specs, out_specs = _row_stream_specs(n_p.shape[0], tm, n_s.shape[0], k, tn)
    return pl.pallas_call(
        _ffn_in_body,
        grid=(nblk, row_steps),
        in_specs=in_specs + [
            pl.BlockSpec((None, k, tn), lambda j, i: (layer, 0, j)),
            pl.BlockSpec((None, k, tn), lambda j, i: (layer, 0, j + nblk)),
        ],
        out_specs=out_specs,
        out_shape=[jax.ShapeDtypeStruct((n_p.shape[0], D_FF), BF16), jax.ShapeDtypeStruct((n_s.shape[0], D_FF), BF16)],
        scratch_shapes=[pltpu.VMEM((k, tn), BF16), pltpu.VMEM((k, tn), BF16)],
        compiler_params=_params(("parallel", "arbitrary")),
        name="ffn_in",
    )(n_p, n_s, w, w)


def _mm_res_body(a_ref, w_ref, r_ref, o_ref, w_scr):
    @pl.when(pl.program_id(1) == 0)
    def _():
        for r in range(0, w_ref.shape[0], W_ROW_CHUNK):
            rs = slice(r, r + W_ROW_CHUNK)
            w_scr[rs, :] = w_ref[rs, :].astype(BF16)

    o_ref[...] = r_ref[...] + _dot(a_ref[...], w_scr[...])


def _matmul_residual(a, w, layer, res, tm, tn):
    t, k = a.shape
    n = w.shape[2]
    return pl.pallas_call(
        _mm_res_body,
        grid=(n // tn, t // tm),
        in_specs=[
            pl.BlockSpec((tm, k), lambda j, i: (i, 0)),
            pl.BlockSpec((None, k, tn), lambda j, i: (layer, 0, j)),
            pl.BlockSpec((tm, tn), lambda j, i: (i, j)),
        ],
        out_specs=pl.BlockSpec((tm, tn), lambda j, i: (i, j)),
        out_shape=jax.ShapeDtypeStruct((t, n), F32),
        scratch_shapes=[pltpu.VMEM((k, tn), BF16)],
        compiler_params=_params(("parallel", "arbitrary")),
        name="matmul_residual",
    )(a, w, res)


def _merge_body(ya_ref, yb_ref, yc_ref, ga_ref, gb_ref, gc_ref, wa_ref, wb_ref, wc_ref, o_ref):
    acc = jax.nn.sigmoid(ga_ref[...]) * _dot(ya_ref[...].astype(BF16), wa_ref[...])
    acc += jax.nn.sigmoid(gb_ref[...]) * _dot(yb_ref[...].astype(BF16), wb_ref[...])
    acc += jax.nn.sigmoid(gc_ref[...]) * _dot(yc_ref[...].astype(BF16), wc_ref[...])
    o_ref[...] = acc.astype(o_ref.dtype)


def _merge(ya, yb, yc, proj, wa, wb, wc, tm, tn):
    t = ya.shape[0]
    nblk = D_MODEL // tn
    g0 = C_GATE // tn
    return pl.pallas_call(
        _merge_body,
        grid=(t // tm, nblk),
        in_specs=[
            pl.BlockSpec((tm, GLA_V), lambda i, j: (i, 0)),
            pl.BlockSpec((tm, CM_WIDTH), lambda i, j: (i, 0)),
            pl.BlockSpec((tm, SSD_INNER), lambda i, j: (i, 0)),
            pl.BlockSpec((tm, tn), lambda i, j: (i, g0 + j)),
            pl.BlockSpec((tm, tn), lambda i, j: (i, g0 + nblk + j)),
            pl.BlockSpec((tm, tn), lambda i, j: (i, g0 + 2 * nblk + j)),
            pl.BlockSpec((GLA_V, tn), lambda i, j: (0, j)),
            pl.BlockSpec((CM_WIDTH, tn), lambda i, j: (0, j)),
            pl.BlockSpec((SSD_INNER, tn), lambda i, j: (0, j)),
        ],
        out_specs=pl.BlockSpec((tm, tn), lambda i, j: (i, j)),
        out_shape=jax.ShapeDtypeStruct((t, D_MODEL), BF16),
        compiler_params=_params(("parallel", "parallel")),
        name="merge",
    )(ya, yb, yc, proj, proj, proj, wa, wb, wc)


def _rms(x, g):
    ms = jnp.mean(x * x, axis=-1, keepdims=True)
    return x * lax.rsqrt(ms + EPS) * g


def _out_proj_body(mix_ref, w_ref, x_ref, g_ref, x1_ref, n_ref):
    x1 = x_ref[...] + _dot(mix_ref[...], w_ref[...])
    x1_ref[...] = x1
    n_ref[...] = _rms(x1, g_ref[...]).astype(n_ref.dtype)


def _out_proj(mix, w, x, g_next, tm):
    t, k = mix.shape
    row_block = lambda width: pl.BlockSpec((tm, width), lambda i: (i, 0))
    return pl.pallas_call(
        _out_proj_body,
        grid=(t // tm,),
        in_specs=[
            row_block(k),
            pl.BlockSpec((k, D_MODEL), lambda i: (0, 0)),
            row_block(D_MODEL),
            pl.BlockSpec((1, D_MODEL), lambda i: (0, 0)),
        ],
        out_specs=[row_block(D_MODEL), row_block(D_MODEL)],
        out_shape=[jax.ShapeDtypeStruct((t, D_MODEL), F32), jax.ShapeDtypeStruct((t, D_MODEL), BF16)],
        compiler_params=_params(("parallel",)),
        name="out_proj",
    )(mix, w, x, g_next)


def _ple_body(x_ref, g_ref, p_ref, wg_ref, wp_ref, gn_ref, *out_refs, final):
    x = x_ref[...]
    n = _rms(x, g_ref[...]).astype(BF16)
    gate = jax.nn.sigmoid(_dot(n, wg_ref[...]))
    emb = _dot(p_ref[...].astype(BF16), wp_ref[...])
    y = x + gate * emb
    if final:
        out_refs[0][...] = _rms(y, gn_ref[...])
    else:
        out_refs[0][...] = y
        out_refs[1][...] = _rms(y, gn_ref[...]).astype(BF16)


def _ple(x, g, p, wg, wp, g_next, final, tm):
    t = x.shape[0]
    row_block = lambda width: pl.BlockSpec((tm, width), lambda i: (i, 0))
    whole = lambda a: pl.BlockSpec(a.shape, lambda i: (0, 0))
    if final:
        out_specs = [row_block(D_MODEL)]
        out_shape = [jax.ShapeDtypeStruct((t, D_MODEL), F32)]
    else:
        out_specs = [row_block(D_MODEL), row_block(D_MODEL)]
        out_shape = [jax.ShapeDtypeStruct((t, D_MODEL), F32), jax.ShapeDtypeStruct((t, D_MODEL), BF16)]
    return pl.pallas_call(
        functools.partial(_ple_body, final=final),
        grid=(t // tm,),
        in_specs=[row_block(D_MODEL), whole(g), row_block(PLE_DIM), whole(wg), whole(wp), whole(g_next)],
        out_specs=out_specs,
        out_shape=out_shape,
        compiler_params=_params(("parallel",)),
        name="ple",
    )(x, g, p, wg, wp, g_next)


def _cmlp_body(u_ref, v_ref, lng_ref, lnb_ref, ws_ref, bst_ref, yb_ref, *rest, seq_len):
    u = jax.nn.gelu(u_ref[...])
    v = jax.nn.gelu(v_ref[...])
    mu = jnp.mean(v, axis=-1, keepdims=True)
    vc = v - mu
    var = jnp.mean(vc * vc, axis=-1, keepdims=True)
    vn = vc * lax.rsqrt(var + EPS) * lng_ref[...] + lnb_ref[...]
    if rest:
        rest[0][...] = vn
    r = lax.broadcasted_iota(jnp.int32, (CM_CHUNK, CM_CHUNK), 0)
    c = lax.broadcasted_iota(jnp.int32, (CM_CHUNK, CM_CHUNK), 1)
    if seq_len >= CM_CHUNK:
        keep = r >= c
    else:
        sh = seq_len.bit_length() - 1
        keep = ((r >> sh) == (c >> sh)) & ((r & (seq_len - 1)) >= (c & (seq_len - 1)))
    for g in range(CM_GROUPS):
        sl = slice(g * CM_GROUP_DIM, (g + 1) * CM_GROUP_DIM)
        w = jnp.where(keep, ws_ref[g], 0.0).astype(BF16)
        mixed = _dot(w, vn[:, sl].astype(BF16)) + bst_ref[:, g:g + 1]
        yb_ref[:, sl] = (u[:, sl] * mixed).astype(yb_ref.dtype)


def _cmlp(proj, ln_g, ln_b, ws_tiled, bs_t, seq_len, emit_v, out_dtype):
    t = proj.shape[0]
    out_shape = [jax.ShapeDtypeStruct((t, CM_WIDTH), out_dtype)]
    out_specs = [pl.BlockSpec((CM_CHUNK, CM_WIDTH), lambda i: (i, 0))]
    if emit_v:
        out_shape.append(jax.ShapeDtypeStruct((t, CM_WIDTH), F32))
        out_specs.append(pl.BlockSpec((CM_CHUNK, CM_WIDTH), lambda i: (i, 0)))
    cu = C_UV // CM_WIDTH
    res = pl.pallas_call(
        functools.partial(_cmlp_body, seq_len=seq_len),
        grid=(t // CM_CHUNK,),
        in_specs=[
            pl.BlockSpec((CM_CHUNK, CM_WIDTH), lambda i: (i, cu)),
            pl.BlockSpec((CM_CHUNK, CM_WIDTH), lambda i: (i, cu + 1)),
            pl.BlockSpec((1, CM_WIDTH), lambda i: (0, 0)),
            pl.BlockSpec((1, CM_WIDTH), lambda i: (0, 0)),
            pl.BlockSpec((CM_GROUPS, CM_CHUNK, CM_CHUNK), lambda i: (0, 0, 0)),
            pl.BlockSpec((CM_CHUNK, CM_GROUPS), lambda i: (0, 0)),
        ],
        out_specs=out_specs,
        out_shape=out_shape,
        compiler_params=_params(("parallel",)),
        name="chunk_mlp",
    )(proj, proj, ln_g, ln_b, ws_tiled, bs_t)
    return (res[0], res[1]) if emit_v else (res[0], None)


def _gla_level_constants(n):
    levels = n.bit_length() - 1
    t = np.arange(n)[:, None]
    j = np.arange(n)[None, :]
    sums = [j <= t, j > t]
    masks = [t == j]
    for l in range(1, levels + 1):
        w, half = 1 << l, 1 << (l - 1)
        start = (t >> l) << l
        upper = ((t >> (l - 1)) & 1) == 1
        a_up = (j >= start + half) & (j <= t)
        a_lo = (j > t) & (j < start + half)
        sums.append(np.where(upper, a_up, a_lo))
        jj = j
        upper_t = ((t >> (l - 1)) & 1) == 1
        lower_s = ((jj >> (l - 1)) & 1) == 0
        masks.append(((t >> l) == (jj >> l)) & upper_t & lower_s)
    return (jnp.asarray(np.stack(sums), BF16), jnp.asarray(np.stack(masks), F32), levels)


def _gla_log_decay(lr_ref, wlr_ref, blr_ref):
    x = _dot(lr_ref[...].astype(BF16), wlr_ref[...]) + blr_ref[...]
    return _log_sigmoid(x) * (1.0 / GLA_TAU)


def _gla_finish(o, g, gn):
    ms = jnp.mean(o * o, axis=-1, keepdims=True)
    return o * lax.rsqrt(ms + EPS) * gn * _silu(g)


def _gla_prompt_body(q_ref, k_ref, v_ref, g_ref, lr_ref, wlr_ref, blr_ref, gn_ref, a_ref, m_ref,
                     ya_ref, s_ref, *, levels):
    rows = q_ref.shape[0]

    @pl.when(pl.program_id(1) == 0)
    def _():
        s_ref[...] = jnp.zeros(s_ref.shape, F32)

    q = q_ref[...] * (GLA_DK ** -0.5)
    k = k_ref[...]
    la = _gla_log_decay(lr_ref, wlr_ref, blr_ref)
    la2 = jnp.concatenate(_split2(la), axis=1)

    def decay_sum(i):
        d = _dot(a_ref[i], la2)
        return d[:, :GLA_QK] + d[:, GLA_QK:]

    b = decay_sum(0)
    qe = (q * jnp.exp(b)).astype(BF16)
    kd = (k * jnp.exp(decay_sum(1))).astype(BF16)
    e_end_t = jnp.exp(jnp.broadcast_to(b[rows - 1:rows, :], (SUBLANES, GLA_QK))).T
    row = lax.broadcasted_iota(jnp.int32, (rows, GLA_QK), 0)
    xs = []
    for l in range(1, levels + 1):
        e = jnp.exp(decay_sum(l + 1))
        upper = ((row >> (l - 1)) & 1) == 1
        xs.append((jnp.where(upper, q, k) * e).astype(BF16))
    qb = q.astype(BF16)
    kb = k.astype(BF16)
    for h in range(GLA_HEADS):
        ks = slice(h * GLA_DK, (h + 1) * GLA_DK)
        vs = slice(h * GLA_DV, (h + 1) * GLA_DV)
        att = m_ref[0] * _nt(qb[:, ks], kb[:, ks])
        for l in range(1, levels + 1):
            x = xs[l - 1][:, ks]
            att += m_ref[l] * _nt(x, x)
        vh = v_ref[:, vs].astype(BF16)
        s = s_ref[0, h]
        o = _dot(qe[:, ks], s.astype(BF16)) + _dot(att.astype(BF16), vh)
        s_ref[0, h] = s * e_end_t[ks, 0:1] + _tn(kd[:, ks], vh)
        ya_ref[:, vs] = _gla_finish(o, g_ref[:, vs], gn_ref[:, vs]).astype(ya_ref.dtype)


def _gla_prompt(proj, small, bsz, length, w_lr, b_lr, gn):
    nchunk = length // CHUNK
    a_mats, masks, levels = _gla_level_constants(CHUNK)
    row = lambda b, c: b * nchunk + c
    return pl.pallas_call(
        functools.partial(_gla_prompt_body, levels=levels),
        grid=(bsz, nchunk),
        in_specs=[
            pl.BlockSpec((CHUNK, GLA_QK), lambda b, c: (row(b, c), C_Q // GLA_QK)),
            pl.BlockSpec((CHUNK, GLA_QK), lambda b, c: (row(b, c), C_K // GLA_QK)),
            pl.BlockSpec((CHUNK, GLA_V), lambda b, c: (row(b, c), C_V // GLA_V)),
            pl.BlockSpec((CHUNK, GLA_V), lambda b, c: (row(b, c), C_G // GLA_V)),
            pl.BlockSpec((CHUNK, LANES), lambda b, c: (row(b, c), SMALL_LR)),
            pl.BlockSpec((LANES, GLA_QK), lambda b, c: (0, 0)),
            pl.BlockSpec((1, GLA_QK), lambda b, c: (0, 0)),
            pl.BlockSpec((1, GLA_V), lambda b, c: (0, 0)),
            pl.BlockSpec(a_mats.shape, lambda b, c: (0, 0, 0)),
            pl.BlockSpec(masks.shape, lambda b, c: (0, 0, 0)),
        ],
        out_specs=[
            pl.BlockSpec((CHUNK, GLA_V), lambda b, c: (row(b, c), 0)),
            pl.BlockSpec((1, GLA_HEADS, GLA_DK, GLA_DV), lambda b, c: (b, 0, 0, 0)),
        ],
        out_shape=[
            jax.ShapeDtypeStruct((bsz * length, GLA_V), BF16),
            jax.ShapeDtypeStruct((bsz, GLA_HEADS, GLA_DK, GLA_DV), F32),
        ],
        compiler_params=_params(("parallel", "arbitrary")),
        name="gla_prompt",
    )(proj, proj, proj, proj, small, w_lr, b_lr, gn, a_mats, masks)


def _roll_rows(x, d):
    return pltpu.roll(x, d, 0) if d else x


def _gla_sample_body(q_ref, k_ref, v_ref, g_ref, lr_ref, wlr_ref, blr_ref, gn_ref, s0_ref,
                     ya_ref, s_ref):
    n = SAMPLE_LEN
    pos_k = lax.broadcasted_iota(jnp.int32, (SAMPLE_ROWS, GLA_QK), 0) & (n - 1)
    row_v = lax.broadcasted_iota(jnp.int32, (SAMPLE_ROWS, GLA_DV), 0)
    q = q_ref[...] * (GLA_DK ** -0.5)
    k = k_ref[...]
    v = v_ref[...]
    la = _gla_log_decay(lr_ref, wlr_ref, blr_ref)
    back = [_roll_rows(la, d) for d in range(n)]
    win = [None, la]
    for d in range(2, n):
        win.append(win[-1] + back[d - 1])
    b = la
    suffix = jnp.zeros_like(la)
    for d in range(1, n):
        b = b + jnp.where(pos_k >= d, back[d], 0.0)
        suffix = suffix + jnp.where(pos_k < n - d, pltpu.roll(la, SAMPLE_ROWS - d, 0), 0.0)
    eb = jnp.exp(b)
    qe = (q * eb).astype(BF16)
    kd = k * jnp.exp(suffix)
    eb_t = eb.T
    nseq = SAMPLE_ROWS // n
    seq_k = lax.broadcasted_iota(jnp.int32, (SAMPLE_ROWS, GLA_QK), 0) >> SAMPLE_SHIFT
    kd_seq = [jnp.where(seq_k == j, kd, 0.0).astype(BF16) for j in range(nseq)]
    pair = []
    for d in range(n):
        p = q * _roll_rows(k, d)
        if d:
            p = p * jnp.exp(win[d])
        pair.append(p)
    vback = [_roll_rows(v, d) for d in range(n)]
    for h in range(GLA_HEADS):
        ks = slice(h * GLA_DK, (h + 1) * GLA_DK)
        vs = slice(h * GLA_DV, (h + 1) * GLA_DV)
        o = jnp.zeros((SAMPLE_ROWS, GLA_DV), F32)
        for d in range(n):
            score = jnp.sum(pair[d][:, ks], axis=-1, keepdims=True)
            o = o + jnp.where((row_v & (n - 1)) >= d, score * vback[d][:, vs], 0.0)
        vh = v[:, vs].astype(BF16)
        for j in range(nseq):
            s = s0_ref[j, h]
            o = o + jnp.where((row_v >> SAMPLE_SHIFT) == j, _dot(qe[:, ks], s.astype(BF16)), 0.0)
            col = n * j + n - 1
            s_ref[j, h] = s * eb_t[ks, col:col + 1] + _tn(kd_seq[j][:, ks], vh)
        ya_ref[:, vs] = _gla_finish(o, g_ref[:, vs], gn_ref[:, vs]).astype(ya_ref.dtype)


def _stacked_state_call(body, prev_out, n_in):
    if prev_out is None:
        return body, [], [], {}
    wrapped = lambda *refs: body(*refs[:n_in], *refs[n_in + 1:])
    return wrapped, [pl.BlockSpec(memory_space=pl.ANY)], [prev_out], {n_in: 1}


def _gla_sample(proj, small, state_all, layer, prev_out, w_lr, b_lr, gn):
    t = proj.shape[0]
    nseq = SAMPLE_ROWS // SAMPLE_LEN
    state_spec = pl.BlockSpec((None, nseq, GLA_HEADS, GLA_DK, GLA_DV), lambda i: (layer, i, 0, 0, 0))
    body, extra_specs, extra_args, aliases = _stacked_state_call(_gla_sample_body, prev_out, 9)
    return pl.pallas_call(
        body,
        grid=(t // SAMPLE_ROWS,),
        in_specs=[
            pl.BlockSpec((SAMPLE_ROWS, GLA_QK), lambda i: (i, C_Q // GLA_QK)),
            pl.BlockSpec((SAMPLE_ROWS, GLA_QK), lambda i: (i, C_K // GLA_QK)),
            pl.BlockSpec((SAMPLE_ROWS, GLA_V), lambda i: (i, C_V // GLA_V)),
            pl.BlockSpec((SAMPLE_ROWS, GLA_V), lambda i: (i, C_G // GLA_V)),
            pl.BlockSpec((SAMPLE_ROWS, LANES), lambda i: (i, SMALL_LR)),
            pl.BlockSpec((LANES, GLA_QK), lambda i: (0, 0)),
            pl.BlockSpec((1, GLA_QK), lambda i: (0, 0)),
            pl.BlockSpec((1, GLA_V), lambda i: (0, 0)),
            state_spec,
        ] + extra_specs,
        out_specs=[pl.BlockSpec((SAMPLE_ROWS, GLA_V), lambda i: (i, 0)), state_spec],
        out_shape=[
            jax.ShapeDtypeStruct((t, GLA_V), F32),
            jax.ShapeDtypeStruct(state_all.shape, F32),
        ],
        input_output_aliases=aliases,
        compiler_params=_params(("parallel",)),
        name="gla_sample",
    )(proj, proj, proj, proj, small, w_lr, b_lr, gn, state_all, *extra_args)


XBC_PART = 2048


def _ssd_gate_norm(y, z, gn):
    yz = y * _silu(z)
    ms = jnp.mean(yz * yz, axis=-1, keepdims=True)
    return yz * lax.rsqrt(ms + EPS) * gn


def _ssd_prompt_body(x1_ref, x2_ref, bc_ref, z1_ref, z2_ref, dt_ref, cw_ref, cb_ref, dtb_ref, alog_ref,
                     dx_ref, gn_ref, e_ref, a_ref, hm_ref, yc_ref, h_ref, tail_scr, ht_scr):
    rows = x1_ref.shape[0]
    step = pl.program_id(1)

    @pl.when(step == 0)
    def _():
        tail_scr[...] = jnp.zeros(tail_scr.shape, F32)
        ht_scr[...] = jnp.zeros(ht_scr.shape, F32)

    row8 = lax.broadcasted_iota(jnp.int32, (SUBLANES, XBC_PART), 0)

    def conv(x_ref, part):
        cs = slice(part * XBC_PART, (part + 1) * XBC_PART)
        x = x_ref[...]
        prev = tail_scr[:, cs]
        acc = cb_ref[:, cs] + x * cw_ref[SSD_CONV - 1:SSD_CONV, cs]
        for d in range(1, SSD_CONV):
            xr = pltpu.roll(x, d, 0)
            first = jnp.where(row8 < d, pltpu.roll(prev, d, 0), xr[:SUBLANES])
            xd = jnp.concatenate([first, xr[SUBLANES:]], axis=0)
            acc = acc + xd * cw_ref[SSD_CONV - 1 - d:SSD_CONV - d, cs]
        tail_scr[:, cs] = x[rows - SUBLANES:, :]
        return _silu(acc)

    xs_halves = [conv(x1_ref, 0), conv(x2_ref, 1)]
    bc = conv(bc_ref, 2)
    half_w = SSD_GROUPS * SSD_STATE
    bmat = bc[:, :half_w]
    cmat = bc[:, half_w:]
    z_halves = [z1_ref, z2_ref]

    dt = _softplus(dt_ref[...] + dtb_ref[...])
    a = dt * (-jnp.exp(alog_ref[...]))
    a2 = jnp.concatenate(_split2(a), axis=1)

    def head_sum(i):
        d = _dot(a_ref[i], a2)
        return d[:, :LANES] + d[:, LANES:]

    cs_in = head_sum(0)
    cs_suf = head_sum(1)
    cs_t = cs_in.T
    cs_hi, cs_lo = _split2(cs_in)
    suf_hi, suf_lo = _split2(cs_suf)
    dt_hi, dt_lo = _split2(dt)
    r_i = lax.broadcasted_iota(jnp.int32, (rows, rows), 0)
    c_i = lax.broadcasted_iota(jnp.int32, (rows, rows), 1)
    causal = r_i >= c_i
    heads_per_group = SSD_HEADS // SSD_GROUPS
    per_half = SSD_GROUPS // 2
    for g in range(SSD_GROUPS):
        gs = slice(g * SSD_GROUP_WIDTH, (g + 1) * SSD_GROUP_WIDTH)
        ls = slice((g % per_half) * SSD_GROUP_WIDTH, (g % per_half + 1) * SSD_GROUP_WIDTH)
        ns = slice(g * SSD_STATE, (g + 1) * SSD_STATE)
        xg = xs_halves[g // per_half][:, ls]
        eg = e_ref[:, gs]
        csx = _dot(cs_hi, eg) + _dot(cs_lo, eg)
        sufx = _dot(suf_hi, eg) + _dot(suf_lo, eg)
        dtx = _dot(dt_hi, eg) + _dot(dt_lo, eg)
        bg = bmat[:, ns].astype(BF16)
        cg = cmat[:, ns].astype(BF16)
        cb = jnp.where(causal, _nt(cg, bg), 0.0)
        xdt = dtx * xg
        xdt_b = xdt.astype(BF16)
        ws = []
        xb = []
        for r in range(heads_per_group):
            h = g * heads_per_group + r
            dm = cs_in[:, h:h + 1] - cs_t[h:h + 1, :]
            ws.append((cb * jnp.exp(jnp.minimum(dm, 0.0))).astype(BF16))
            xb.append(xdt_b * hm_ref[r])
        y = _dot(jnp.concatenate(ws, axis=1), jnp.concatenate(xb, axis=0))
        ht = ht_scr[g]
        y = y + _dot(cg, ht.astype(BF16)) * jnp.exp(csx)
        y = y + dx_ref[:, gs] * xg
        wx = (jnp.exp(sufx) * xdt).astype(BF16)
        ht_scr[g] = ht * jnp.exp(csx[rows - 1:rows, :]) + _tn(bg, wx)
        z = z_halves[g // per_half][:, ls]
        yc_ref[:, gs] = _ssd_gate_norm(y, z, gn_ref[:, gs]).astype(yc_ref.dtype)

    @pl.when(step == pl.num_programs(1) - 1)
    def _():
        for g in range(SSD_GROUPS):
            h_ref[0, g] = ht_scr[g].T


def _tri_constants(n):
    t = np.arange(n)[:, None]
    j = np.arange(n)[None, :]
    return jnp.asarray(np.stack([j <= t, j > t]), BF16)


def _head_lane_masks(rows):
    lane = np.arange(SSD_GROUP_WIDTH)[None, None, :]
    r = np.arange(SSD_HEADS // SSD_GROUPS)[:, None, None]
    return jnp.asarray(np.broadcast_to(lane // SSD_HEADDIM == r, (SSD_HEADS // SSD_GROUPS, rows, SSD_GROUP_WIDTH)), BF16)


def _head_expand_matrix():
    h = np.arange(LANES)[:, None]
    lane = np.arange(SSD_INNER)[None, :]
    return jnp.asarray(h == lane // SSD_HEADDIM, BF16)


def _ssd_prompt(proj, small, bsz, length, conv_w, conv_b, dt_bias, a_log, d_x, gn):
    nchunk = length // CHUNK
    row = lambda b, c: b * nchunk + c
    cx = C_XBC // XBC_PART
    cz = C_Z // XBC_PART
    tri = _tri_constants(CHUNK)
    e = _head_expand_matrix()
    head_masks = _head_lane_masks(CHUNK)
    full = lambda shape: pl.BlockSpec(shape, lambda b, c: (0,) * len(shape))
    return pl.pallas_call(
        _ssd_prompt_body,
        grid=(bsz, nchunk),
        in_specs=[
            pl.BlockSpec((CHUNK, XBC_PART), lambda b, c: (row(b, c), cx)),
            pl.BlockSpec((CHUNK, XBC_PART), lambda b, c: (row(b, c), cx + 1)),
            pl.BlockSpec((CHUNK, XBC_PART), lambda b, c: (row(b, c), cx + 2)),
            pl.BlockSpec((CHUNK, XBC_PART), lambda b, c: (row(b, c), cz)),
            pl.BlockSpec((CHUNK, XBC_PART), lambda b, c: (row(b, c), cz + 1)),
            pl.BlockSpec((CHUNK, LANES), lambda b, c: (row(b, c), SMALL_DT)),
            full((SSD_CONV, SSD_CONV_DIM)),
            full((1, SSD_CONV_DIM)),
            full((1, LANES)),
            full((1, LANES)),
            full((1, SSD_INNER)),
            full((1, SSD_INNER)),
            full((LANES, SSD_INNER)),
            full(tri.shape),
            full(head_masks.shape),
        ],
        out_specs=[
            pl.BlockSpec((CHUNK, SSD_INNER), lambda b, c: (row(b, c), 0)),
            pl.BlockSpec((1, SSD_GROUPS, SSD_GROUP_WIDTH, SSD_STATE), lambda b, c: (b, 0, 0, 0)),
        ],
        out_shape=[
            jax.ShapeDtypeStruct((bsz * length, SSD_INNER), BF16),
            jax.ShapeDtypeStruct((bsz, SSD_GROUPS, SSD_GROUP_WIDTH, SSD_STATE), F32),
        ],
        scratch_shapes=[
            pltpu.VMEM((SUBLANES, SSD_CONV_DIM), F32),
            pltpu.VMEM((SSD_GROUPS, SSD_STATE, SSD_GROUP_WIDTH), F32),
        ],
        compiler_params=_params(("parallel", "arbitrary")),
        name="ssd_prompt",
    )(proj, proj, proj, proj, proj, small, conv_w, conv_b, dt_bias, a_log, d_x, gn, e, tri, head_masks)


def _ssd_sample_body(x1_ref, x2_ref, bc_ref, z1_ref, z2_ref, dt_ref, cw_ref, cb_ref, dtb_ref, alog_ref,
                     dx_ref, gn_ref, e_ref, prev_ref, h0_ref, yc_ref, h_ref):
    n = SAMPLE_LEN
    nseq = SAMPLE_ROWS // n
    pos_p = lax.broadcasted_iota(jnp.int32, (SAMPLE_ROWS, XBC_PART), 0) & (n - 1)

    def conv(x_ref, part):
        cs = slice(part * XBC_PART, (part + 1) * XBC_PART)
        x = x_ref[...]
        prev = prev_ref[0][:, cs]
        acc = cb_ref[:, cs] + x * cw_ref[SSD_CONV - 1:SSD_CONV, cs]
        for d in range(1, SSD_CONV):
            xd = jnp.where(pos_p < d, pltpu.roll(prev, d, 0), pltpu.roll(x, d, 0))
            acc = acc + xd * cw_ref[SSD_CONV - 1 - d:SSD_CONV - d, cs]
        return _silu(acc)

    xs = jnp.concatenate([conv(x1_ref, 0), conv(x2_ref, 1)], axis=1)
    bc = conv(bc_ref, 2)
    half_w = SSD_GROUPS * SSD_STATE
    bmat = bc[:, :half_w]
    cmat = bc[:, half_w:]
    z = jnp.concatenate([z1_ref[...], z2_ref[...]], axis=1)

    dt = _softplus(dt_ref[...] + dtb_ref[...])
    dt_hi, dt_lo = _split2(dt)
    dtx = _dot(dt_hi, e_ref[...]) + _dot(dt_lo, e_ref[...])
    a_hi, a_lo = _split2(-jnp.exp(alog_ref[...]))
    anegx = _dot(jnp.broadcast_to(a_hi, (SUBLANES, LANES)), e_ref[...]) + \
        _dot(jnp.broadcast_to(a_lo, (SUBLANES, LANES)), e_ref[...])
    ax = dtx * anegx
    pos = lax.broadcasted_iota(jnp.int32, (SAMPLE_ROWS, SSD_INNER), 0) & (n - 1)
    seq_g = lax.broadcasted_iota(jnp.int32, (SAMPLE_ROWS, SSD_GROUP_WIDTH), 0) >> SAMPLE_SHIFT
    back = [_roll_rows(ax, d) for d in range(n)]
    win = [None, ax]
    for d in range(2, n):
        win.append(win[-1] + back[d - 1])
    csx = ax
    sufx = jnp.zeros_like(ax)
    for d in range(1, n):
        csx = csx + jnp.where(pos >= d, back[d], 0.0)
        sufx = sufx + jnp.where(pos < n - d, pltpu.roll(ax, SAMPLE_ROWS - d, 0), 0.0)
    ecs = jnp.exp(csx)
    ecs_t = ecs.T
    wx = jnp.exp(sufx) * dtx * xs

    y = dx_ref[...] * xs
    for d in range(n):
        prod = cmat * _roll_rows(bmat, d)
        cbx = jnp.concatenate(
            [jnp.broadcast_to(jnp.sum(prod[:, g * SSD_STATE:(g + 1) * SSD_STATE], axis=-1, keepdims=True),
                              (SAMPLE_ROWS, SSD_GROUP_WIDTH)) for g in range(SSD_GROUPS)], axis=1)
        term = cbx * _roll_rows(dtx, d) * _roll_rows(xs, d)
        if d:
            term = term * jnp.exp(win[d])
        y = y + jnp.where(pos >= d, term, 0.0)

    y_inter = []
    for g in range(SSD_GROUPS):
        gs = slice(g * SSD_GROUP_WIDTH, (g + 1) * SSD_GROUP_WIDTH)
        ns = slice(g * SSD_STATE, (g + 1) * SSD_STATE)
        bg = bmat[:, ns].astype(BF16)
        cg = cmat[:, ns].astype(BF16)
        acc = jnp.zeros((SAMPLE_ROWS, SSD_GROUP_WIDTH), F32)
        for j in range(nseq):
            h0 = h0_ref[j, g]
            acc = acc + jnp.where(seq_g == j, _nt(cg, h0.astype(BF16)), 0.0)
            wxj = jnp.where(seq_g == j, wx[:, gs], 0.0).astype(BF16)
            col = n * j + n - 1
            h_ref[j, g] = h0 * ecs_t[gs, col:col + 1] + _tn(wxj, bg)
        y_inter.append(acc)
    y = y + jnp.concatenate(y_inter, axis=1) * ecs
    for g in range(SSD_GROUPS):
        gs = slice(g * SSD_GROUP_WIDTH, (g + 1) * SSD_GROUP_WIDTH)
        yc_ref[:, gs] = _ssd_gate_norm(y[:, gs], z[:, gs], gn_ref[:, gs]).astype(yc_ref.dtype)


def _ssd_sample(proj, small, conv_prev, state_all, layer, prev_out, conv_w, conv_b, dt_bias, a_log, d_x, gn):
    t = proj.shape[0]
    nseq = SAMPLE_ROWS // SAMPLE_LEN
    cx = C_XBC // XBC_PART
    cz = C_Z // XBC_PART
    e = _head_expand_matrix()
    full = lambda shape: pl.BlockSpec(shape, lambda i: (0,) * len(shape))
    state_spec = pl.BlockSpec((None, nseq, SSD_GROUPS, SSD_GROUP_WIDTH, SSD_STATE), lambda i: (layer, i, 0, 0, 0))
    body, extra_specs, extra_args, aliases = _stacked_state_call(_ssd_sample_body, prev_out, 15)
    return pl.pallas_call(
        body,
        grid=(t // SAMPLE_ROWS,),
        in_specs=[
            pl.BlockSpec((SAMPLE_ROWS, XBC_PART), lambda i: (i, cx)),
            pl.BlockSpec((SAMPLE_ROWS, XBC_PART), lambda i: (i, cx + 1)),
            pl.BlockSpec((SAMPLE_ROWS, XBC_PART), lambda i: (i, cx + 2)),
            pl.BlockSpec((SAMPLE_ROWS, XBC_PART), lambda i: (i, cz)),
            pl.BlockSpec((SAMPLE_ROWS, XBC_PART), lambda i: (i, cz + 1)),
            pl.BlockSpec((SAMPLE_ROWS, LANES), lambda i: (i, SMALL_DT)),
            full((SSD_CONV, SSD_CONV_DIM)),
            full((1, SSD_CONV_DIM)),
            full((1, LANES)),
            full((1, LANES)),
            full((1, SSD_INNER)),
            full((1, SSD_INNER)),
            full((LANES, SSD_INNER)),
            pl.BlockSpec((1, SAMPLE_ROWS, SSD_CONV_DIM), lambda i: (i, 0, 0)),
            state_spec,
        ] + extra_specs,
        out_specs=[pl.BlockSpec((SAMPLE_ROWS, SSD_INNER), lambda i: (i, 0)), state_spec],
        out_shape=[
            jax.ShapeDtypeStruct((t, SSD_INNER), F32),
            jax.ShapeDtypeStruct(state_all.shape, F32),
        ],
        input_output_aliases=aliases,
        compiler_params=_params(("parallel",)),
        name="ssd_sample",
    )(proj, proj, proj, proj, proj, small, conv_w, conv_b, dt_bias, a_log, d_x, gn, e, conv_prev, state_all,
      *extra_args)


def _narrow_w_in_t(w_in_t, layer):
    k = w_in_t.shape[2]
    rows = lambda a, b: lax.slice(w_in_t, (layer, a, 0), (layer + 1, b, k)).reshape(b - a, k)
    lr = rows(NATIVE_UV - GLA_LOWRANK, NATIVE_UV)
    dt = rows(NATIVE_DT, NATIVE_GATE)
    pad = lambda n: jnp.zeros((n, k), w_in_t.dtype)
    return jnp.concatenate([lr, pad(LANES - GLA_LOWRANK), dt, pad(LANES - SSD_HEADS)], axis=0)


def _pad_lanes(v, n):
    return jnp.pad(v, (0, n - v.shape[0])).reshape(1, n)


def _layer_weights(i, w_in_t, g_next, g_mix, w_in, w_gla_lr, b_gla_lr, g_gla_norm, cm_ln_g, cm_ln_b, cm_ws,
                   cm_bs, ssd_conv_w, ssd_conv_b, ssd_dt_bias, ssd_a_log, ssd_d, g_ssd_norm,
                   w_br_gla, w_br_cm, w_br_ssd, w_o, g_ffn, w_ffn_in, w_ffn_out,
                   g_ple, w_ple_gate, w_ple_proj):
    row = lambda v: v[i].reshape(1, -1)
    n_tile = CM_CHUNK // SAMPLE_LEN
    return dict(
        w_in_t=w_in_t,
        w_in_narrow=_narrow_w_in_t(w_in_t, i),
        layer=i,
        g_next=g_next.reshape(1, -1),
        w_lr=jnp.pad(w_gla_lr[i], ((0, LANES - GLA_LOWRANK), (0, 0))).astype(BF16),
        b_lr=row(b_gla_lr),
        g_gla=row(g_gla_norm),
        ln_g=row(cm_ln_g),
        ln_b=row(cm_ln_b),
        ws_prompt=cm_ws[i],
        bs_prompt=cm_bs[i].T,
        ws_sample=jnp.tile(cm_ws[i][:, :SAMPLE_LEN, :SAMPLE_LEN], (1, n_tile, n_tile)),
        bs_sample=jnp.tile(cm_bs[i][:, :SAMPLE_LEN].T, (n_tile, 1)),
        conv_w=ssd_conv_w[i],
        conv_b=row(ssd_conv_b),
        dt_bias=_pad_lanes(ssd_dt_bias[i], LANES),
        a_log=_pad_lanes(ssd_a_log[i], LANES),
        d_x=jnp.repeat(ssd_d[i], SSD_HEADDIM).reshape(1, SSD_INNER),
        g_ssd=row(g_ssd_norm),
        w_br_gla=w_br_gla[i].astype(BF16),
        w_br_cm=w_br_cm[i].astype(BF16),
        w_br_ssd=w_br_ssd[i].astype(BF16),
        w_o=w_o[i].astype(BF16),
        g_ffn=row(g_ffn),
        w_ffn_in=w_ffn_in,
        w_ffn_out=w_ffn_out,
        g_ple=row(g_ple),
        w_ple_gate=w_ple_gate[i].astype(BF16),
        w_ple_proj=w_ple_proj[i].astype(BF16),
    )


def _merge_out(x, proj, ya, yb, yc, w):
    mix = _merge(ya, yb, yc, proj, w["w_br_gla"], w["w_br_cm"], w["w_br_ssd"], 512, 512)
    return _out_proj(mix, w["w_o"], x, w["g_ffn"], 256)


def _ffn_ple(x, h, p, w, final):
    x = _matmul_residual(h, w["w_ffn_out"], w["layer"], x, 512, 512)
    return _ple(x, w["g_ple"], p, w["w_ple_gate"], w["w_ple_proj"], w["g_next"], final, 256)


def _last_conv_rows(proj, bsz, length):
    keep = SSD_CONV - 1
    if length % SUBLANES == 0:
        rows = proj.reshape(bsz, length, proj.shape[-1])[:, length - keep:]
        return rows[:, :, C_XBC:C_XBC + SSD_CONV_DIM]
    xbc = proj[:, C_XBC:C_XBC + SSD_CONV_DIM]
    return xbc.reshape(bsz, length, SSD_CONV_DIM)[:, length - keep:]


def _prompt_mixers(proj, small, bsz, length, w):
    ya, s_gla = _gla_prompt(proj, small, bsz, length, w["w_lr"], w["b_lr"], w["g_gla"])
    yb, _ = _cmlp(proj, w["ln_g"], w["ln_b"], w["ws_prompt"], w["bs_prompt"], CM_CHUNK, False, BF16)
    yc, s_ssm = _ssd_prompt(proj, small, bsz, length, w["conv_w"], w["conv_b"], w["dt_bias"], w["a_log"],
                            w["d_x"], w["g_ssd"])
    s_conv = _last_conv_rows(proj, bsz, length)
    s_ssm = s_ssm.reshape(bsz, SSD_HEADS, SSD_HEADDIM, SSD_STATE)
    return (ya, yb, yc), s_gla, s_ssm, s_conv


def _sample_mixers(proj, small, bsz, gla_all, ssm_all, s_conv, prev_gla, prev_ssm, w):
    n = SAMPLE_LEN
    ya, gla_out = _gla_sample(proj, small, gla_all, w["layer"], prev_gla, w["w_lr"], w["b_lr"], w["g_gla"])
    yb, v_rows = _cmlp(proj, w["ln_g"], w["ln_b"], w["ws_sample"], w["bs_sample"], n, True, F32)
    sc = s_conv.reshape(bsz // 2, 2, SSD_CONV - 1, SSD_CONV_DIM)
    zrow = jnp.zeros((bsz // 2, 1, SSD_CONV_DIM), F32)
    conv_prev = jnp.concatenate([zrow, sc[:, 1], zrow, sc[:, 0]], axis=1)
    yc, ssm_out = _ssd_sample(proj, small, conv_prev, ssm_all, w["layer"], prev_ssm, w["conv_w"], w["conv_b"],
                              w["dt_bias"], w["a_log"], w["d_x"], w["g_ssd"])
    s_conv_new = _last_conv_rows(proj, bsz, n)
    return (ya, yb, yc), gla_out, ssm_out, s_conv_new, v_rows.reshape(bsz, n, CM_WIDTH)


def kernel(x_prompt, x_sample, state_gla, state_ssm, state_conv, p_prompt, p_sample, g_mix, w_in, w_gla_lr, b_gla_lr, g_gla_norm, cm_ln_g, cm_ln_b, cm_ws, cm_bs, ssd_conv_w, ssd_conv_b, ssd_dt_bias, ssd_a_log, ssd_d, g_ssd_norm, w_br_gla, w_br_cm, w_br_ssd, w_o, g_ffn, w_ffn_in, w_ffn_out, g_ple, w_ple_gate, w_ple_proj, g_final):
    weights = (g_mix, w_in, w_gla_lr, b_gla_lr, g_gla_norm, cm_ln_g, cm_ln_b, cm_ws, cm_bs,
               ssd_conv_w, ssd_conv_b, ssd_dt_bias, ssd_a_log, ssd_d, g_ssd_norm,
               w_br_gla, w_br_cm, w_br_ssd, w_o, g_ffn, w_ffn_in, w_ffn_out,
               g_ple, w_ple_gate, w_ple_proj)
    bp, lp, _ = x_prompt.shape
    bs, ls, _ = x_sample.shape
    assert ls == SAMPLE_LEN and lp % CHUNK == 0 and bs % 2 == 0
    xp = x_prompt.reshape(bp * lp, D_MODEL)
    xs = x_sample.reshape(bs * ls, D_MODEL)
    g_first = g_mix[0].reshape(1, D_MODEL)
    np_ = _rms_cast(xp, g_first, 512, BF16)
    ns = _rms_cast(xs, g_first, 512, BF16)
    w_in_t = jnp.swapaxes(w_in, 1, 2)
    gla_p, ssm_p, conv_p = [], [], []
    conv_s, v_s = [], []
    ssm_all = state_ssm.reshape(DEPTH, bs, SSD_GROUPS, SSD_GROUP_WIDTH, SSD_STATE)
    gla_s = ssm_s = None
    for i in range(DEPTH):
        final = i == DEPTH - 1
        g_next = g_final if final else g_mix[i + 1]
        w = _layer_weights(i, w_in_t, g_next, *weights)
        proj_p, proj_s = _proj_matmul(np_, ns, w_in_t, i, 1024)
        small_p = _matmul_small_nt(np_, w["w_in_narrow"], 512)
        small_s = _matmul_small_nt(ns, w["w_in_narrow"], 512)
        ys_p, sg, sm, sc = _prompt_mixers(proj_p, small_p, bp, lp, w)
        gla_p.append(sg)
        ssm_p.append(sm)
        conv_p.append(sc)
        ys_s, gla_s, ssm_s, sc, vr = _sample_mixers(proj_s, small_s, bs, state_gla, ssm_all, state_conv[i],
                                                    gla_s, ssm_s, w)
        conv_s.append(sc)
        v_s.append(vr)
        xp, n_ffn_p = _merge_out(xp, proj_p, *ys_p, w)
        xs, n_ffn_s = _merge_out(xs, proj_s, *ys_s, w)
        h_p, h_s = _ffn_in(n_ffn_p, n_ffn_s, w_ffn_in, i, 1024, 512)
        outs = _ffn_ple(xp, h_p, p_prompt[i].reshape(bp * lp, PLE_DIM), w, final)
        xp, np_ = (outs[0], None) if final else outs
        outs = _ffn_ple(xs, h_s, p_sample[i].reshape(bs * ls, PLE_DIM), w, final)
        xs, ns = (outs[0], None) if final else outs
    y_prompt = xp.reshape(bp, lp, D_MODEL)
    y_sample = xs.reshape(bs, ls, D_MODEL)
    ssm_s = ssm_s.reshape(DEPTH, bs, SSD_HEADS, SSD_HEADDIM, SSD_STATE)
    return (y_prompt, y_sample, jnp.stack(gla_p), gla_s, jnp.stack(ssm_p), ssm_s,
            jnp.stack(conv_p), jnp.stack(conv_s), jnp.stack(v_s))
```

```python
import functools

import numpy as np
import jax
import jax.numpy as jnp
from jax import lax
from jax.experimental import pallas as pl
from jax.experimental.pallas import tpu as pltpu

F32 = jnp.float32
BF16 = jnp.bfloat16

D_MODEL = 2048
DEPTH = 2
GLA_HEADS = 4
GLA_DK = 256
GLA_DV = 512
GLA_QK = 1024
GLA_V = 2048
GLA_LOWRANK = 16
GLA_TAU = 16.0
CM_GROUPS = 8
CM_WIDTH = 2048
CM_GROUP_DIM = 256
CM_CHUNK = 128
SSD_INNER = 4096
SSD_HEADDIM = 64
SSD_HEADS = 64
SSD_GROUPS = 8
SSD_STATE = 128
SSD_CONV = 4
SSD_CONV_DIM = 6144
SSD_GROUP_WIDTH = SSD_INNER // SSD_GROUPS
D_FF = 5632
PLE_DIM = 256
EPS = 1e-6

LANES = 128
SUBLANES = 8
VMEM_LIMIT = 56 * 1024 * 1024

C_Q = 0
C_K = 1024
C_V = 2048
C_G = 4096
C_UV = 6144
C_Z = 10240
C_XBC = 14336
C_GATE = 20480
C_LR = 26624
SMALL_LR = 0
SMALL_DT = 1

CHUNK = 128
SAMPLE_ROWS = 8
SAMPLE_LEN = 4
SAMPLE_SHIFT = 2
HEADDIM_SHIFT = 6


def _params(sem):
    return pltpu.CompilerParams(dimension_semantics=sem, vmem_limit_bytes=VMEM_LIMIT)


def _nt(a, b):
    return lax.dot_general(a, b, (((1,), (1,)), ((), ())), preferred_element_type=F32)


def _tn(a, b):
    return lax.dot_general(a, b, (((0,), (0,)), ((), ())), preferred_element_type=F32)


def _dot(a, b):
    return jnp.dot(a, b, preferred_element_type=F32)


def _split2(x):
    hi = x.astype(BF16)
    lo = (x - hi.astype(F32)).astype(BF16)
    return hi, lo


def _softplus(x):
    return jnp.maximum(x, 0.0) + jnp.log(1.0 + jnp.exp(-jnp.abs(x)))


def _log_sigmoid(x):
    return jnp.minimum(x, 0.0) - jnp.log(1.0 + jnp.exp(-jnp.abs(x)))


def _silu(x):
    return x * jax.nn.sigmoid(x)


def _rms_cast_body(x_ref, g_ref, o_ref):
    x = x_ref[...]
    ms = jnp.mean(x * x, axis=-1, keepdims=True)
    o_ref[...] = (x * lax.rsqrt(ms + EPS) * g_ref[...]).astype(o_ref.dtype)


def _rms_cast(x, g, tm, dtype):
    t, k = x.shape
    return pl.pallas_call(
        _rms_cast_body,
        grid=(t // tm,),
        in_specs=[pl.BlockSpec((tm, k), lambda i: (i, 0)), pl.BlockSpec((1, k), lambda i: (0, 0))],
        out_specs=pl.BlockSpec((tm, k), lambda i: (i, 0)),
        out_shape=jax.ShapeDtypeStruct((t, k), dtype),
        compiler_params=_params(("parallel",)),
        name="rms_cast",
    )(x, g)


NATIVE_UV = 2 * GLA_QK + 2 * GLA_V + GLA_LOWRANK
NATIVE_DT = NATIVE_UV + 2 * CM_WIDTH + SSD_INNER + SSD_CONV_DIM
NATIVE_GATE = NATIVE_DT + SSD_HEADS
PROJ_TN = 1024
W_ROW_CHUNK = 256


def _proj_body(n_ref, w_ref, wn_ref, o_ref, w_scr):
    j = pl.program_id(0)
    tn = w_ref.shape[0]

    def load_weights(shift):
        for r in range(0, tn - shift, W_ROW_CHUNK):
            rows = min(W_ROW_CHUNK, tn - shift - r)
            w_scr[r:r + rows, :] = w_ref[r + shift:r + shift + rows, :].astype(BF16)
        if shift:
            w_scr[tn - shift:tn, :] = wn_ref[0:shift, :].astype(BF16)

    @pl.when(pl.program_id(1) == 0)
    def _():
        j_uv = C_UV // PROJ_TN
        j_gate = C_GATE // PROJ_TN
        pl.when(j < j_uv)(lambda: load_weights(0))
        pl.when((j >= j_uv) & (j < j_gate))(lambda: load_weights(NATIVE_UV - C_UV))
        pl.when(j >= j_gate)(lambda: load_weights(NATIVE_GATE - C_GATE))

    o_ref[...] = _nt(n_ref[...], w_scr[...])


def _proj_matmul(n, w_in_t, layer, tm):
    t, k = n.shape
    tn = PROJ_TN
    return pl.pallas_call(
        _proj_body,
        grid=(C_LR // tn, t // tm),
        in_specs=[
            pl.BlockSpec((tm, k), lambda j, i: (i, 0)),
            pl.BlockSpec((None, tn, k), lambda j, i: (layer, j, 0)),
            pl.BlockSpec((None, LANES, k), lambda j, i: (layer, (j + 1) * (tn // LANES), 0)),
        ],
        out_specs=pl.BlockSpec((tm, tn), lambda j, i: (i, j)),
        out_shape=jax.ShapeDtypeStruct((t, C_LR), F32),
        scratch_shapes=[pltpu.VMEM((tn, k), BF16)],
        compiler_params=_params(("parallel", "arbitrary")),
        name="proj_matmul",
    )(n, w_in_t, w_in_t)


def _mm_nt_body(a_ref, w_ref, o_ref):
    o_ref[...] = _nt(a_ref[...], w_ref[...].astype(BF16))


def _matmul_small_nt(a, w_t, tm):
    t, k = a.shape
    n = w_t.shape[0]
    return pl.pallas_call(
        _mm_nt_body,
        grid=(t // tm,),
        in_specs=[pl.BlockSpec((tm, k), lambda i: (i, 0)), pl.BlockSpec((n, k), lambda i: (0, 0))],
        out_specs=pl.BlockSpec((tm, n), lambda i: (i, 0)),
        out_shape=jax.ShapeDtypeStruct((t, n), F32),
        compiler_params=_params(("parallel",)),
        name="matmul_small",
    )(a, w_t)


def _ffn_in_body(n_ref, wg_ref, wu_ref, o_ref, wg_scr, wu_scr):
    @pl.when(pl.program_id(1) == 0)
    def _():
        for r in range(0, wg_ref.shape[0], W_ROW_CHUNK):
            rs = slice(r, r + W_ROW_CHUNK)
            wg_scr[rs, :] = wg_ref[rs, :].astype(BF16)
            wu_scr[rs, :] = wu_ref[rs, :].astype(BF16)

    n = n_ref[...]
    gate = _dot(n, wg_scr[...])
    up = _dot(n, wu_scr[...])
    o_ref[...] = (_silu(gate) * up).astype(o_ref.dtype)


def _ffn_in(n, w, layer, tm, tn):
    t, k = n.shape
    nblk = D_FF // tn
    return pl.pallas_call(
        _ffn_in_body,
        grid=(nblk, t // tm),
        in_specs=[
            pl.BlockSpec((tm, k), lambda j, i: (i, 0)),
            pl.BlockSpec((None, k, tn), lambda j, i: (layer, 0, j)),
            pl.BlockSpec((None, k, tn), lambda j, i: (layer, 0, j + nblk)),
        ],
        out_specs=pl.BlockSpec((tm, tn), lambda j, i: (i, j)),
        out_shape=jax.ShapeDtypeStruct((t, D_FF), BF16),
        scratch_shapes=[pltpu.VMEM((k, tn), BF16), pltpu.VMEM((k, tn), BF16)],
        compiler_params=_params(("parallel", "arbitrary")),
        name="ffn_in",
    )(n, w, w)


def _mm_res_body(a_ref, w_ref, r_ref, o_ref, w_scr):
    @pl.when(pl.program_id(1) == 0)
    def _():
        for r in range(0, w_ref.shape[0], W_ROW_CHUNK):
            rs = slice(r, r + W_ROW_CHUNK)
            w_scr[rs, :] = w_ref[rs, :].astype(BF16)

    o_ref[...] = r_ref[...] + _dot(a_ref[...], w_scr[...])


def _matmul_residual(a, w, layer, res, tm, tn):
    t, k = a.shape
    n = w.shape[2]
    return pl.pallas_call(
        _mm_res_body,
        grid=(n // tn, t // tm),
        in_specs=[
            pl.BlockSpec((tm, k), lambda j, i: (i, 0)),
            pl.BlockSpec((None, k, tn), lambda j, i: (layer, 0, j)),
            pl.BlockSpec((tm, tn), lambda j, i: (i, j)),
        ],
        out_specs=pl.BlockSpec((tm, tn), lambda j, i: (i, j)),
        out_shape=jax.ShapeDtypeStruct((t, n), F32),
        scratch_shapes=[pltpu.VMEM((k, tn), BF16)],
        compiler_params=_params(("parallel", "arbitrary")),
        name="matmul_residual",
    )(a, w, res)


def _merge_body(ya_ref, yb_ref, yc_ref, ga_ref, gb_ref, gc_ref, wa_ref, wb_ref, wc_ref, o_ref):
    acc = jax.nn.sigmoid(ga_ref[...]) * _dot(ya_ref[...].astype(BF16), wa_ref[...])
    acc += jax.nn.sigmoid(gb_ref[...]) * _dot(yb_ref[...].astype(BF16), wb_ref[...])
    acc += jax.nn.sigmoid(gc_ref[...]) * _dot(yc_ref[...].astype(BF16), wc_ref[...])
    o_ref[...] = acc.astype(o_ref.dtype)


def _resident(shape, index_map):
    return pl.BlockSpec(shape, index_map, pipeline_mode=pl.Buffered(1))


def _merge(ya, yb, yc, proj, wa, wb, wc, tm, tn):
    t = ya.shape[0]
    nblk = D_MODEL // tn
    g0 = C_GATE // tn
    return pl.pallas_call(
        _merge_body,
        grid=(nblk, t // tm),
        in_specs=[
            pl.BlockSpec((tm, GLA_V), lambda j, i: (i, 0)),
            pl.BlockSpec((tm, CM_WIDTH), lambda j, i: (i, 0)),
            pl.BlockSpec((tm, SSD_INNER), lambda j, i: (i, 0)),
            pl.BlockSpec((tm, tn), lambda j, i: (i, g0 + j)),
            pl.BlockSpec((tm, tn), lambda j, i: (i, g0 + nblk + j)),
            pl.BlockSpec((tm, tn), lambda j, i: (i, g0 + 2 * nblk + j)),
            _resident((GLA_V, tn), lambda j, i: (0, j)),
            _resident((CM_WIDTH, tn), lambda j, i: (0, j)),
            _resident((SSD_INNER, tn), lambda j, i: (0, j)),
        ],
        out_specs=pl.BlockSpec((tm, tn), lambda j, i: (i, j)),
        out_shape=jax.ShapeDtypeStruct((t, D_MODEL), BF16),
        compiler_params=_params(("parallel", "arbitrary")),
        name="merge",
    )(ya, yb, yc, proj, proj, proj, wa, wb, wc)


def _rms(x, g):
    ms = jnp.mean(x * x, axis=-1, keepdims=True)
    return x * lax.rsqrt(ms + EPS) * g


def _out_proj_body(mix_ref, w_ref, x_ref, g_ref, x1_ref, n_ref):
    x1 = x_ref[...] + _dot(mix_ref[...], w_ref[...])
    x1_ref[...] = x1
    n_ref[...] = _rms(x1, g_ref[...]).astype(n_ref.dtype)


def _out_proj(mix, w, x, g_next, tm):
    t, k = mix.shape
    row_block = lambda width: pl.BlockSpec((tm, width), lambda i: (i, 0))
    return pl.pallas_call(
        _out_proj_body,
        grid=(t // tm,),
        in_specs=[
            row_block(k),
            _resident((k, D_MODEL), lambda i: (0, 0)),
            row_block(D_MODEL),
            pl.BlockSpec((1, D_MODEL), lambda i: (0, 0)),
        ],
        out_specs=[row_block(D_MODEL), row_block(D_MODEL)],
        out_shape=[jax.ShapeDtypeStruct((t, D_MODEL), F32), jax.ShapeDtypeStruct((t, D_MODEL), BF16)],
        compiler_params=_params(("parallel",)),
        name="out_proj",
    )(mix, w, x, g_next)


def _ple_body(x_ref, g_ref, p_ref, wg_ref, wp_ref, gn_ref, *out_refs, final):
    x = x_ref[...]
    n = _rms(x, g_ref[...]).astype(BF16)
    gate = jax.nn.sigmoid(_dot(n, wg_ref[...]))
    emb = _dot(p_ref[...].astype(BF16), wp_ref[...])
    y = x + gate * emb
    if final:
        out_refs[0][...] = _rms(y, gn_ref[...])
    else:
        out_refs[0][...] = y
        out_refs[1][...] = _rms(y, gn_ref[...]).astype(BF16)


def _ple(x, g, p, wg, wp, g_next, final, tm):
    t = x.shape[0]
    row_block = lambda width: pl.BlockSpec((tm, width), lambda i: (i, 0))
    whole = lambda a: _resident(a.shape, lambda i: (0, 0))
    if final:
        out_specs = [row_block(D_MODEL)]
        out_shape = [jax.ShapeDtypeStruct((t, D_MODEL), F32)]
    else:
        out_specs = [row_block(D_MODEL), row_block(D_MODEL)]
        out_shape = [jax.ShapeDtypeStruct((t, D_MODEL), F32), jax.ShapeDtypeStruct((t, D_MODEL), BF16)]
    return pl.pallas_call(
        functools.partial(_ple_body, final=final),
        grid=(t // tm,),
        in_specs=[row_block(D_MODEL), whole(g), row_block(PLE_DIM), whole(wg), whole(wp), whole(g_next)],
        out_specs=out_specs,
        out_shape=out_shape,
        compiler_params=_params(("parallel",)),
        name="ple",
    )(x, g, p, wg, wp, g_next)


def _cmlp_body(u_ref, v_ref, lng_ref, lnb_ref, ws_ref, bst_ref, yb_ref, *rest, seq_len):
    u = jax.nn.gelu(u_ref[...])
    v = jax.nn.gelu(v_ref[...])
    mu = jnp.mean(v, axis=-1, keepdims=True)
    vc = v - mu
    var = jnp.mean(vc * vc, axis=-1, keepdims=True)
    vn = vc * lax.rsqrt(var + EPS) * lng_ref[...] + lnb_ref[...]
    if rest:
        rest[0][...] = vn
    r = lax.broadcasted_iota(jnp.int32, (CM_CHUNK, CM_CHUNK), 0)
    c = lax.broadcasted_iota(jnp.int32, (CM_CHUNK, CM_CHUNK), 1)
    if seq_len >= CM_CHUNK:
        keep = r >= c
    else:
        sh = seq_len.bit_length() - 1
        keep = ((r >> sh) == (c >> sh)) & ((r & (seq_len - 1)) >= (c & (seq_len - 1)))
    for g in range(CM_GROUPS):
        sl = slice(g * CM_GROUP_DIM, (g + 1) * CM_GROUP_DIM)
        w = jnp.where(keep, ws_ref[g], 0.0).astype(BF16)
        mixed = _dot(w, vn[:, sl].astype(BF16)) + bst_ref[:, g:g + 1]
        yb_ref[:, sl] = (u[:, sl] * mixed).astype(yb_ref.dtype)


def _cmlp(proj, ln_g, ln_b, ws_tiled, bs_t, seq_len, emit_v, out_dtype):
    t = proj.shape[0]
    out_shape = [jax.ShapeDtypeStruct((t, CM_WIDTH), out_dtype)]
    out_specs = [pl.BlockSpec((CM_CHUNK, CM_WIDTH), lambda i: (i, 0))]
    if emit_v:
        out_shape.append(jax.ShapeDtypeStruct((t, CM_WIDTH), F32))
        out_specs.append(pl.BlockSpec((CM_CHUNK, CM_WIDTH), lambda i: (i, 0)))
    cu = C_UV // CM_WIDTH
    res = pl.pallas_call(
        functools.partial(_cmlp_body, seq_len=seq_len),
        grid=(t // CM_CHUNK,),
        in_specs=[
            pl.BlockSpec((CM_CHUNK, CM_WIDTH), lambda i: (i, cu)),
            pl.BlockSpec((CM_CHUNK, CM_WIDTH), lambda i: (i, cu + 1)),
            pl.BlockSpec((1, CM_WIDTH), lambda i: (0, 0)),
            pl.BlockSpec((1, CM_WIDTH), lambda i: (0, 0)),
            pl.BlockSpec((CM_GROUPS, CM_CHUNK, CM_CHUNK), lambda i: (0, 0, 0)),
            pl.BlockSpec((CM_CHUNK, CM_GROUPS), lambda i: (0, 0)),
        ],
        out_specs=out_specs,
        out_shape=out_shape,
        compiler_params=_params(("parallel",)),
        name="chunk_mlp",
    )(proj, proj, ln_g, ln_b, ws_tiled, bs_t)
    return (res[0], res[1]) if emit_v else (res[0], None)


def _gla_level_constants(n):
    levels = n.bit_length() - 1
    t = np.arange(n)[:, None]
    j = np.arange(n)[None, :]
    sums = [j <= t, j > t]
    masks = [t == j]
    for l in range(1, levels + 1):
        w, half = 1 << l, 1 << (l - 1)
        start = (t >> l) << l
        upper = ((t >> (l - 1)) & 1) == 1
        a_up = (j >= start + half) & (j <= t)
        a_lo = (j > t) & (j < start + half)
        sums.append(np.where(upper, a_up, a_lo))
        jj = j
        upper_t = ((t >> (l - 1)) & 1) == 1
        lower_s = ((jj >> (l - 1)) & 1) == 0
        masks.append(((t >> l) == (jj >> l)) & upper_t & lower_s)
    return (jnp.asarray(np.stack(sums), BF16), jnp.asarray(np.stack(masks), F32), levels)


def _gla_log_decay(lr_ref, wlr_ref, blr_ref):
    x = _dot(lr_ref[...].astype(BF16), wlr_ref[...]) + blr_ref[...]
    return _log_sigmoid(x) * (1.0 / GLA_TAU)


def _gla_finish(o, g, gn):
    ms = jnp.mean(o * o, axis=-1, keepdims=True)
    return o * lax.rsqrt(ms + EPS) * gn * _silu(g)


def _gla_prompt_body(q_ref, k_ref, v_ref, g_ref, lr_ref, wlr_ref, blr_ref, gn_ref, a_ref, m_ref,
                     ya_ref, s_ref, *, levels):
    rows = q_ref.shape[0]

    @pl.when(pl.program_id(1) == 0)
    def _():
        s_ref[...] = jnp.zeros(s_ref.shape, F32)

    q = q_ref[...] * (GLA_DK ** -0.5)
    k = k_ref[...]
    la = _gla_log_decay(lr_ref, wlr_ref, blr_ref)
    la2 = jnp.concatenate(_split2(la), axis=1)

    def decay_sum(i):
        d = _dot(a_ref[i], la2)
        return d[:, :GLA_QK] + d[:, GLA_QK:]

    b = decay_sum(0)
    qe = (q * jnp.exp(b)).astype(BF16)
    kd = (k * jnp.exp(decay_sum(1))).astype(BF16)
    e_end_t = jnp.exp(jnp.broadcast_to(b[rows - 1:rows, :], (SUBLANES, GLA_QK))).T
    row = lax.broadcasted_iota(jnp.int32, (rows, GLA_QK), 0)
    xs = []
    for l in range(1, levels + 1):
        e = jnp.exp(decay_sum(l + 1))
        upper = ((row >> (l - 1)) & 1) == 1
        xs.append((jnp.where(upper, q, k) * e).astype(BF16))
    qb = q.astype(BF16)
    kb = k.astype(BF16)
    for h in range(GLA_HEADS):
        ks = slice(h * GLA_DK, (h + 1) * GLA_DK)
        vs = slice(h * GLA_DV, (h + 1) * GLA_DV)
        att = m_ref[0] * _nt(qb[:, ks], kb[:, ks])
        for l in range(1, levels + 1):
            x = xs[l - 1][:, ks]
            att += m_ref[l] * _nt(x, x)
        vh = v_ref[:, vs].astype(BF16)
        s = s_ref[0, h]
        o = _dot(qe[:, ks], s.astype(BF16)) + _dot(att.astype(BF16), vh)
        s_ref[0, h] = s * e_end_t[ks, 0:1] + _tn(kd[:, ks], vh)
        ya_ref[:, vs] = _gla_finish(o, g_ref[:, vs], gn_ref[:, vs]).astype(ya_ref.dtype)


def _gla_prompt(proj, small, bsz, length, w_lr, b_lr, gn):
    nchunk = length // CHUNK
    a_mats, masks, levels = _gla_level_constants(CHUNK)
    row = lambda b, c: b * nchunk + c
    return pl.pallas_call(
        functools.partial(_gla_prompt_body, levels=levels),
        grid=(bsz, nchunk),
        in_specs=[
            pl.BlockSpec((CHUNK, GLA_QK), lambda b, c: (row(b, c), C_Q // GLA_QK)),
            pl.BlockSpec((CHUNK, GLA_QK), lambda b, c: (row(b, c), C_K // GLA_QK)),
            pl.BlockSpec((CHUNK, GLA_V), lambda b, c: (row(b, c), C_V // GLA_V)),
            pl.BlockSpec((CHUNK, GLA_V), lambda b, c: (row(b, c), C_G // GLA_V)),
            pl.BlockSpec((CHUNK, LANES), lambda b, c: (row(b, c), SMALL_LR)),
            pl.BlockSpec((LANES, GLA_QK), lambda b, c: (0, 0)),
            pl.BlockSpec((1, GLA_QK), lambda b, c: (0, 0)),
            pl.BlockSpec((1, GLA_V), lambda b, c: (0, 0)),
            pl.BlockSpec(a_mats.shape, lambda b, c: (0, 0, 0)),
            pl.BlockSpec(masks.shape, lambda b, c: (0, 0, 0)),
        ],
        out_specs=[
            pl.BlockSpec((CHUNK, GLA_V), lambda b, c: (row(b, c), 0)),
            pl.BlockSpec((1, GLA_HEADS, GLA_DK, GLA_DV), lambda b, c: (b, 0, 0, 0)),
        ],
        out_shape=[
            jax.ShapeDtypeStruct((bsz * length, GLA_V), BF16),
            jax.ShapeDtypeStruct((bsz, GLA_HEADS, GLA_DK, GLA_DV), F32),
        ],
        compiler_params=_params(("parallel", "arbitrary")),
        name="gla_prompt",
    )(proj, proj, proj, proj, small, w_lr, b_lr, gn, a_mats, masks)


def _roll_rows(x, d):
    return pltpu.roll(x, d, 0) if d else x


def _gla_sample_body(q_ref, k_ref, v_ref, g_ref, lr_ref, wlr_ref, blr_ref, gn_ref, s0_ref,
                     ya_ref, s_ref):
    n = SAMPLE_LEN
    pos_k = lax.broadcasted_iota(jnp.int32, (SAMPLE_ROWS, GLA_QK), 0) & (n - 1)
    row_v = lax.broadcasted_iota(jnp.int32, (SAMPLE_ROWS, GLA_DV), 0)
    q = q_ref[...] * (GLA_DK ** -0.5)
    k = k_ref[...]
    v = v_ref[...]
    la = _gla_log_decay(lr_ref, wlr_ref, blr_ref)
    back = [_roll_rows(la, d) for d in range(n)]
    win = [None, la]
    for d in range(2, n):
        win.append(win[-1] + back[d - 1])
    b = la
    suffix = jnp.zeros_like(la)
    for d in range(1, n):
        b = b + jnp.where(pos_k >= d, back[d], 0.0)
        suffix = suffix + jnp.where(pos_k < n - d, pltpu.roll(la, SAMPLE_ROWS - d, 0), 0.0)
    eb = jnp.exp(b)
    qe = (q * eb).astype(BF16)
    kd = k * jnp.exp(suffix)
    eb_t = eb.T
    nseq = SAMPLE_ROWS // n
    seq_k = lax.broadcasted_iota(jnp.int32, (SAMPLE_ROWS, GLA_QK), 0) >> SAMPLE_SHIFT
    kd_seq = [jnp.where(seq_k == j, kd, 0.0).astype(BF16) for j in range(nseq)]
    pair = []
    for d in range(n):
        p = q * _roll_rows(k, d)
        if d:
            p = p * jnp.exp(win[d])
        pair.append(p)
    vback = [_roll_rows(v, d) for d in range(n)]
    for h in range(GLA_HEADS):
        ks = slice(h * GLA_DK, (h + 1) * GLA_DK)
        vs = slice(h * GLA_DV, (h + 1) * GLA_DV)
        o = jnp.zeros((SAMPLE_ROWS, GLA_DV), F32)
        for d in range(n):
            score = jnp.sum(pair[d][:, ks], axis=-1, keepdims=True)
            o = o + jnp.where((row_v & (n - 1)) >= d, score * vback[d][:, vs], 0.0)
        vh = v[:, vs].astype(BF16)
        for j in range(nseq):
            s = s0_ref[j, h]
            o = o + jnp.where((row_v >> SAMPLE_SHIFT) == j, _dot(qe[:, ks], s.astype(BF16)), 0.0)
            col = n * j + n - 1
            s_ref[j, h] = s * eb_t[ks, col:col + 1] + _tn(kd_seq[j][:, ks], vh)
        ya_ref[:, vs] = _gla_finish(o, g_ref[:, vs], gn_ref[:, vs]).astype(ya_ref.dtype)


def _stacked_state_call(body, prev_out, n_in):
    if prev_out is None:
        return body, [], [], {}
    wrapped = lambda *refs: body(*refs[:n_in], *refs[n_in + 1:])
    return wrapped, [pl.BlockSpec(memory_space=pl.ANY)], [prev_out], {n_in: 1}


def _gla_sample(proj, small, state_all, layer, prev_out, w_lr, b_lr, gn):
    t = proj.shape[0]
    nseq = SAMPLE_ROWS // SAMPLE_LEN
    state_spec = pl.BlockSpec((None, nseq, GLA_HEADS, GLA_DK, GLA_DV), lambda i: (layer, i, 0, 0, 0))
    body, extra_specs, extra_args, aliases = _stacked_state_call(_gla_sample_body, prev_out, 9)
    return pl.pallas_call(
        body,
        grid=(t // SAMPLE_ROWS,),
        in_specs=[
            pl.BlockSpec((SAMPLE_ROWS, GLA_QK), lambda i: (i, C_Q // GLA_QK)),
            pl.BlockSpec((SAMPLE_ROWS, GLA_QK), lambda i: (i, C_K // GLA_QK)),
            pl.BlockSpec((SAMPLE_ROWS, GLA_V), lambda i: (i, C_V // GLA_V)),
            pl.BlockSpec((SAMPLE_ROWS, GLA_V), lambda i: (i, C_G // GLA_V)),
            pl.BlockSpec((SAMPLE_ROWS, LANES), lambda i: (i, SMALL_LR)),
            pl.BlockSpec((LANES, GLA_QK), lambda i: (0, 0)),
            pl.BlockSpec((1, GLA_QK), lambda i: (0, 0)),
            pl.BlockSpec((1, GLA_V), lambda i: (0, 0)),
            state_spec,
        ] + extra_specs,
        out_specs=[pl.BlockSpec((SAMPLE_ROWS, GLA_V), lambda i: (i, 0)), state_spec],
        out_shape=[
            jax.ShapeDtypeStruct((t, GLA_V), F32),
            jax.ShapeDtypeStruct(state_all.shape, F32),
        ],
        input_output_aliases=aliases,
        compiler_params=_params(("parallel",)),
        name="gla_sample",
    )(proj, proj, proj, proj, small, w_lr, b_lr, gn, state_all, *extra_args)


XBC_PART = 2048


def _ssd_gate_norm(y, z, gn):
    yz = y * _silu(z)
    ms = jnp.mean(yz * yz, axis=-1, keepdims=True)
    return yz * lax.rsqrt(ms + EPS) * gn


def _ssd_prompt_body(x1_ref, x2_ref, bc_ref, z1_ref, z2_ref, dt_ref, cw_ref, cb_ref, dtb_ref, alog_ref,
                     dx_ref, gn_ref, e_ref, a_ref, hm_ref, yc_ref, h_ref, tail_scr, ht_scr):
    rows = x1_ref.shape[0]
    step = pl.program_id(1)

    @pl.when(step == 0)
    def _():
        tail_scr[...] = jnp.zeros(tail_scr.shape, F32)
        ht_scr[...] = jnp.zeros(ht_scr.shape, F32)

    row8 = lax.broadcasted_iota(jnp.int32, (SUBLANES, XBC_PART), 0)

    def conv(x_ref, part):
        cs = slice(part * XBC_PART, (part + 1) * XBC_PART)
        x = x_ref[...]
        prev = tail_scr[:, cs]
        acc = cb_ref[:, cs] + x * cw_ref[SSD_CONV - 1:SSD_CONV, cs]
        for d in range(1, SSD_CONV):
            xr = pltpu.roll(x, d, 0)
            first = jnp.where(row8 < d, pltpu.roll(prev, d, 0), xr[:SUBLANES])
            xd = jnp.concatenate([first, xr[SUBLANES:]], axis=0)
            acc = acc + xd * cw_ref[SSD_CONV - 1 - d:SSD_CONV - d, cs]
        tail_scr[:, cs] = x[rows - SUBLANES:, :]
        return _silu(acc)

    xs_halves = [conv(x1_ref, 0), conv(x2_ref, 1)]
    bc = conv(bc_ref, 2)
    half_w = SSD_GROUPS * SSD_STATE
    bmat = bc[:, :half_w]
    cmat = bc[:, half_w:]
    z_halves = [z1_ref, z2_ref]

    dt = _softplus(dt_ref[...] + dtb_ref[...])
    a = dt * (-jnp.exp(alog_ref[...]))
    a2 = jnp.concatenate(_split2(a), axis=1)

    def head_sum(i):
        d = _dot(a_ref[i], a2)
        return d[:, :LANES] + d[:, LANES:]

    cs_in = head_sum(0)
    cs_suf = head_sum(1)
    cs_t = cs_in.T
    cs_hi, cs_lo = _split2(cs_in)
    suf_hi, suf_lo = _split2(cs_suf)
    dt_hi, dt_lo = _split2(dt)
    r_i = lax.broadcasted_iota(jnp.int32, (rows, rows), 0)
    c_i = lax.broadcasted_iota(jnp.int32, (rows, rows), 1)
    causal = r_i >= c_i
    heads_per_group = SSD_HEADS // SSD_GROUPS
    per_half = SSD_GROUPS // 2
    for g in range(SSD_GROUPS):
        gs = slice(g * SSD_GROUP_WIDTH, (g + 1) * SSD_GROUP_WIDTH)
        ls = slice((g % per_half) * SSD_GROUP_WIDTH, (g % per_half + 1) * SSD_GROUP_WIDTH)
        ns = slice(g * SSD_STATE, (g + 1) * SSD_STATE)
        xg = xs_halves[g // per_half][:, ls]
        eg = e_ref[:, gs]
        csx = _dot(cs_hi, eg) + _dot(cs_lo, eg)
        sufx = _dot(suf_hi, eg) + _dot(suf_lo, eg)
        dtx = _dot(dt_hi, eg) + _dot(dt_lo, eg)
        bg = bmat[:, ns].astype(BF16)
        cg = cmat[:, ns].astype(BF16)
        cb = jnp.where(causal, _nt(cg, bg), 0.0)
        xdt = dtx * xg
        xdt_b = xdt.astype(BF16)
        ws = []
        xb = []
        for r in range(heads_per_group):
            h = g * heads_per_group + r
            dm = cs_in[:, h:h + 1] - cs_t[h:h + 1, :]
            ws.append((cb * jnp.exp(jnp.minimum(dm, 0.0))).astype(BF16))
            xb.append(xdt_b * hm_ref[r])
        y = _dot(jnp.concatenate(ws, axis=1), jnp.concatenate(xb, axis=0))
        ht = ht_scr[g]
        y = y + _dot(cg, ht.astype(BF16)) * jnp.exp(csx)
        y = y + dx_ref[:, gs] * xg
        wx = (jnp.exp(sufx) * xdt).astype(BF16)
        ht_scr[g] = ht * jnp.exp(csx[rows - 1:rows, :]) + _tn(bg, wx)
        z = z_halves[g // per_half][:, ls]
        yc_ref[:, gs] = _ssd_gate_norm(y, z, gn_ref[:, gs]).astype(yc_ref.dtype)

    @pl.when(step == pl.num_programs(1) - 1)
    def _():
        for g in range(SSD_GROUPS):
            h_ref[0, g] = ht_scr[g].T


def _tri_constants(n):
    t = np.arange(n)[:, None]
    j = np.arange(n)[None, :]
    return jnp.asarray(np.stack([j <= t, j > t]), BF16)


def _head_lane_masks(rows):
    lane = np.arange(SSD_GROUP_WIDTH)[None, None, :]
    r = np.arange(SSD_HEADS // SSD_GROUPS)[:, None, None]
    return jnp.asarray(np.broadcast_to(lane // SSD_HEADDIM == r, (SSD_HEADS // SSD_GROUPS, rows, SSD_GROUP_WIDTH)), BF16)


def _head_expand_matrix():
    h = np.arange(LANES)[:, None]
    lane = np.arange(SSD_INNER)[None, :]
    return jnp.asarray(h == lane // SSD_HEADDIM, BF16)


def _ssd_prompt(proj, small, bsz, length, conv_w, conv_b, dt_bias, a_log, d_x, gn):
    nchunk = length // CHUNK
    row = lambda b, c: b * nchunk + c
    cx = C_XBC // XBC_PART
    cz = C_Z // XBC_PART
    tri = _tri_constants(CHUNK)
    e = _head_expand_matrix()
    head_masks = _head_lane_masks(CHUNK)
    full = lambda shape: pl.BlockSpec(shape, lambda b, c: (0,) * len(shape))
    return pl.pallas_call(
        _ssd_prompt_body,
        grid=(bsz, nchunk),
        in_specs=[
            pl.BlockSpec((CHUNK, XBC_PART), lambda b, c: (row(b, c), cx)),
            pl.BlockSpec((CHUNK, XBC_PART), lambda b, c: (row(b, c), cx + 1)),
            pl.BlockSpec((CHUNK, XBC_PART), lambda b, c: (row(b, c), cx + 2)),
            pl.BlockSpec((CHUNK, XBC_PART), lambda b, c: (row(b, c), cz)),
            pl.BlockSpec((CHUNK, XBC_PART), lambda b, c: (row(b, c), cz + 1)),
            pl.BlockSpec((CHUNK, LANES), lambda b, c: (row(b, c), SMALL_DT)),
            full((SSD_CONV, SSD_CONV_DIM)),
            full((1, SSD_CONV_DIM)),
            full((1, LANES)),
            full((1, LANES)),
            full((1, SSD_INNER)),
            full((1, SSD_INNER)),
            full((LANES, SSD_INNER)),
            full(tri.shape),
            full(head_masks.shape),
        ],
        out_specs=[
            pl.BlockSpec((CHUNK, SSD_INNER), lambda b, c: (row(b, c), 0)),
            pl.BlockSpec((1, SSD_GROUPS, SSD_GROUP_WIDTH, SSD_STATE), lambda b, c: (b, 0, 0, 0)),
        ],
        out_shape=[
            jax.ShapeDtypeStruct((bsz * length, SSD_INNER), BF16),
            jax.ShapeDtypeStruct((bsz, SSD_GROUPS, SSD_GROUP_WIDTH, SSD_STATE), F32),
        ],
        scratch_shapes=[
            pltpu.VMEM((SUBLANES, SSD_CONV_DIM), F32),
            pltpu.VMEM((SSD_GROUPS, SSD_STATE, SSD_GROUP_WIDTH), F32),
        ],
        compiler_params=_params(("parallel", "arbitrary")),
        name="ssd_prompt",
    )(proj, proj, proj, proj, proj, small, conv_w, conv_b, dt_bias, a_log, d_x, gn, e, tri, head_masks)


def _ssd_sample_body(x1_ref, x2_ref, bc_ref, z1_ref, z2_ref, dt_ref, cw_ref, cb_ref, dtb_ref, alog_ref,
                     dx_ref, gn_ref, e_ref, prev_ref, h0_ref, yc_ref, h_ref):
    n = SAMPLE_LEN
    nseq = SAMPLE_ROWS // n
    pos_p = lax.broadcasted_iota(jnp.int32, (SAMPLE_ROWS, XBC_PART), 0) & (n - 1)

    def conv(x_ref, part):
        cs = slice(part * XBC_PART, (part + 1) * XBC_PART)
        x = x_ref[...]
        prev = prev_ref[0][:, cs]
        acc = cb_ref[:, cs] + x * cw_ref[SSD_CONV - 1:SSD_CONV, cs]
        for d in range(1, SSD_CONV):
            xd = jnp.where(pos_p < d, pltpu.roll(prev, d, 0), pltpu.roll(x, d, 0))
            acc = acc + xd * cw_ref[SSD_CONV - 1 - d:SSD_CONV - d, cs]
        return _silu(acc)

    xs = jnp.concatenate([conv(x1_ref, 0), conv(x2_ref, 1)], axis=1)
    bc = conv(bc_ref, 2)
    half_w = SSD_GROUPS * SSD_STATE
    bmat = bc[:, :half_w]
    cmat = bc[:, half_w:]
    z = jnp.concatenate([z1_ref[...], z2_ref[...]], axis=1)

    dt = _softplus(dt_ref[...] + dtb_ref[...])
    dt_hi, dt_lo = _split2(dt)
    dtx = _dot(dt_hi, e_ref[...]) + _dot(dt_lo, e_ref[...])
    a_hi, a_lo = _split2(-jnp.exp(alog_ref[...]))
    anegx = _dot(jnp.broadcast_to(a_hi, (SUBLANES, LANES)), e_ref[...]) + \
        _dot(jnp.broadcast_to(a_lo, (SUBLANES, LANES)), e_ref[...])
    ax = dtx * anegx
    pos = lax.broadcasted_iota(jnp.int32, (SAMPLE_ROWS, SSD_INNER), 0) & (n - 1)
    seq_g = lax.broadcasted_iota(jnp.int32, (SAMPLE_ROWS, SSD_GROUP_WIDTH), 0) >> SAMPLE_SHIFT
    back = [_roll_rows(ax, d) for d in range(n)]
    win = [None, ax]
    for d in range(2, n):
        win.append(win[-1] + back[d - 1])
    csx = ax
    sufx = jnp.zeros_like(ax)
    for d in range(1, n):
        csx = csx + jnp.where(pos >= d, back[d], 0.0)
        sufx = sufx + jnp.where(pos < n - d, pltpu.roll(ax, SAMPLE_ROWS - d, 0), 0.0)
    ecs = jnp.exp(csx)
    ecs_t = ecs.T
    wx = jnp.exp(sufx) * dtx * xs

    y = dx_ref[...] * xs
    for d in range(n):
        prod = cmat * _roll_rows(bmat, d)
        cbx = jnp.concatenate(
            [jnp.broadcast_to(jnp.sum(prod[:, g * SSD_STATE:(g + 1) * SSD_STATE], axis=-1, keepdims=True),
                              (SAMPLE_ROWS, SSD_GROUP_WIDTH)) for g in range(SSD_GROUPS)], axis=1)
        term = cbx * _roll_rows(dtx, d) * _roll_rows(xs, d)
        if d:
            term = term * jnp.exp(win[d])
        y = y + jnp.where(pos >= d, term, 0.0)

    y_inter = []
    for g in range(SSD_GROUPS):
        gs = slice(g * SSD_GROUP_WIDTH, (g + 1) * SSD_GROUP_WIDTH)
        ns = slice(g * SSD_STATE, (g + 1) * SSD_STATE)
        bg = bmat[:, ns].astype(BF16)
        cg = cmat[:, ns].astype(BF16)
        acc = jnp.zeros((SAMPLE_ROWS, SSD_GROUP_WIDTH), F32)
        for j in range(nseq):
            h0 = h0_ref[j, g]
            acc = acc + jnp.where(seq_g == j, _nt(cg, h0.astype(BF16)), 0.0)
            wxj = jnp.where(seq_g == j, wx[:, gs], 0.0).astype(BF16)
            col = n * j + n - 1
            h_ref[j, g] = h0 * ecs_t[gs, col:col + 1] + _tn(wxj, bg)
        y_inter.append(acc)
    y = y + jnp.concatenate(y_inter, axis=1) * ecs
    for g in range(SSD_GROUPS):
        gs = slice(g * SSD_GROUP_WIDTH, (g + 1) * SSD_GROUP_WIDTH)
        yc_ref[:, gs] = _ssd_gate_norm(y[:, gs], z[:, gs], gn_ref[:, gs]).astype(yc_ref.dtype)


def _ssd_sample(proj, small, conv_prev, state_all, layer, prev_out, conv_w, conv_b, dt_bias, a_log, d_x, gn):
    t = proj.shape[0]
    nseq = SAMPLE_ROWS // SAMPLE_LEN
    cx = C_XBC // XBC_PART
    cz = C_Z // XBC_PART
    e = _head_expand_matrix()
    full = lambda shape: pl.BlockSpec(shape, lambda i: (0,) * len(shape))
    state_spec = pl.BlockSpec((None, nseq, SSD_GROUPS, SSD_GROUP_WIDTH, SSD_STATE), lambda i: (layer, i, 0, 0, 0))
    body, extra_specs, extra_args, aliases = _stacked_state_call(_ssd_sample_body, prev_out, 15)
    return pl.pallas_call(
        body,
        grid=(t // SAMPLE_ROWS,),
        in_specs=[
            pl.BlockSpec((SAMPLE_ROWS, XBC_PART), lambda i: (i, cx)),
            pl.BlockSpec((SAMPLE_ROWS, XBC_PART), lambda i: (i, cx + 1)),
            pl.BlockSpec((SAMPLE_ROWS, XBC_PART), lambda i: (i, cx + 2)),
            pl.BlockSpec((SAMPLE_ROWS, XBC_PART), lambda i: (i, cz)),
            pl.BlockSpec((SAMPLE_ROWS, XBC_PART), lambda i: (i, cz + 1)),
            pl.BlockSpec((SAMPLE_ROWS, LANES), lambda i: (i, SMALL_DT)),
            full((SSD_CONV, SSD_CONV_DIM)),
            full((1, SSD_CONV_DIM)),
            full((1, LANES)),
            full((1, LANES)),
            full((1, SSD_INNER)),
            full((1, SSD_INNER)),
            full((LANES, SSD_INNER)),
            pl.BlockSpec((1, SAMPLE_ROWS, SSD_CONV_DIM), lambda i: (i, 0, 0)),
            state_spec,
        ] + extra_specs,
        out_specs=[pl.BlockSpec((SAMPLE_ROWS, SSD_INNER), lambda i: (i, 0)), state_spec],
        out_shape=[
            jax.ShapeDtypeStruct((t, SSD_INNER), F32),
            jax.ShapeDtypeStruct(state_all.shape, F32),
        ],
        input_output_aliases=aliases,
        compiler_params=_params(("parallel",)),
        name="ssd_sample",
    )(proj, proj, proj, proj, proj, small, conv_w, conv_b, dt_bias, a_log, d_x, gn, e, conv_prev, state_all,
      *extra_args)


def _narrow_w_in_t(w_in_t, layer):
    k = w_in_t.shape[2]
    rows = lambda a, b: lax.slice(w_in_t, (layer, a, 0), (layer + 1, b, k)).reshape(b - a, k)
    lr = rows(NATIVE_UV - GLA_LOWRANK, NATIVE_UV)
    dt = rows(NATIVE_DT, NATIVE_GATE)
    pad = lambda n: jnp.zeros((n, k), w_in_t.dtype)
    return jnp.concatenate([lr, pad(LANES - GLA_LOWRANK), dt, pad(LANES - SSD_HEADS)], axis=0)


def _pad_lanes(v, n):
    return jnp.pad(v, (0, n - v.shape[0])).reshape(1, n)


def _layer_weights(i, w_in_t, g_next, g_mix, w_in, w_gla_lr, b_gla_lr, g_gla_norm, cm_ln_g, cm_ln_b, cm_ws,
                   cm_bs, ssd_conv_w, ssd_conv_b, ssd_dt_bias, ssd_a_log, ssd_d, g_ssd_norm,
                   w_br_gla, w_br_cm, w_br_ssd, w_o, g_ffn, w_ffn_in, w_ffn_out,
                   g_ple, w_ple_gate, w_ple_proj):
    row = lambda v: v[i].reshape(1, -1)
    n_tile = CM_CHUNK // SAMPLE_LEN
    return dict(
        w_in_t=w_in_t,
        w_in_narrow=_narrow_w_in_t(w_in_t, i),
        layer=i,
        g_next=g_next.reshape(1, -1),
        w_lr=jnp.pad(w_gla_lr[i], ((0, LANES - GLA_LOWRANK), (0, 0))).astype(BF16),
        b_lr=row(b_gla_lr),
        g_gla=row(g_gla_norm),
        ln_g=row(cm_ln_g),
        ln_b=row(cm_ln_b),
        ws_prompt=cm_ws[i],
        bs_prompt=cm_bs[i].T,
        ws_sample=jnp.tile(cm_ws[i][:, :SAMPLE_LEN, :SAMPLE_LEN], (1, n_tile, n_tile)),
        bs_sample=jnp.tile(cm_bs[i][:, :SAMPLE_LEN].T, (n_tile, 1)),
        conv_w=ssd_conv_w[i],
        conv_b=row(ssd_conv_b),
        dt_bias=_pad_lanes(ssd_dt_bias[i], LANES),
        a_log=_pad_lanes(ssd_a_log[i], LANES),
        d_x=jnp.repeat(ssd_d[i], SSD_HEADDIM).reshape(1, SSD_INNER),
        g_ssd=row(g_ssd_norm),
        w_br_gla=w_br_gla[i].astype(BF16),
        w_br_cm=w_br_cm[i].astype(BF16),
        w_br_ssd=w_br_ssd[i].astype(BF16),
        w_o=w_o[i].astype(BF16),
        g_ffn=row(g_ffn),
        w_ffn_in=w_ffn_in,
        w_ffn_out=w_ffn_out,
        g_ple=row(g_ple),
        w_ple_gate=w_ple_gate[i].astype(BF16),
        w_ple_proj=w_ple_proj[i].astype(BF16),
    )


def _merge_out(x, proj, ya, yb, yc, w):
    tm = 512 if ya.dtype == BF16 else 256
    mix = _merge(ya, yb, yc, proj, w["w_br_gla"], w["w_br_cm"], w["w_br_ssd"], tm, 1024)
    return _out_proj(mix, w["w_o"], x, w["g_ffn"], 512)


def _ffn_ple(x, h, p, w, final):
    x = _matmul_residual(h, w["w_ffn_out"], w["layer"], x, 512, 512)
    return _ple(x, w["g_ple"], p, w["w_ple_gate"], w["w_ple_proj"], w["g_next"], final, 512)


def _last_conv_rows(proj, bsz, length):
    keep = SSD_CONV - 1
    if length % SUBLANES == 0:
        rows = proj.reshape(bsz, length, proj.shape[-1])[:, length - keep:]
        return rows[:, :, C_XBC:C_XBC + SSD_CONV_DIM]
    xbc = proj[:, C_XBC:C_XBC + SSD_CONV_DIM]
    return xbc.reshape(bsz, length, SSD_CONV_DIM)[:, length - keep:]


def _prompt_mixers(proj, small, bsz, length, w):
    ya, s_gla = _gla_prompt(proj, small, bsz, length, w["w_lr"], w["b_lr"], w["g_gla"])
    yb, _ = _cmlp(proj, w["ln_g"], w["ln_b"], w["ws_prompt"], w["bs_prompt"], CM_CHUNK, False, BF16)
    yc, s_ssm = _ssd_prompt(proj, small, bsz, length, w["conv_w"], w["conv_b"], w["dt_bias"], w["a_log"],
                            w["d_x"], w["g_ssd"])
    s_conv = _last_conv_rows(proj, bsz, length)
    s_ssm = s_ssm.reshape(bsz, SSD_HEADS, SSD_HEADDIM, SSD_STATE)
    return (ya, yb, yc), s_gla, s_ssm, s_conv


def _sample_mixers(proj, small, bsz, gla_all, ssm_all, s_conv, prev_gla, prev_ssm, w):
    n = SAMPLE_LEN
    ya, gla_out = _gla_sample(proj, small, gla_all, w["layer"], prev_gla, w["w_lr"], w["b_lr"], w["g_gla"])
    yb, v_rows = _cmlp(proj, w["ln_g"], w["ln_b"], w["ws_sample"], w["bs_sample"], n, True, F32)
    sc = s_conv.reshape(bsz // 2, 2, SSD_CONV - 1, SSD_CONV_DIM)
    zrow = jnp.zeros((bsz // 2, 1, SSD_CONV_DIM), F32)
    conv_prev = jnp.concatenate([zrow, sc[:, 1], zrow, sc[:, 0]], axis=1)
    yc, ssm_out = _ssd_sample(proj, small, conv_prev, ssm_all, w["layer"], prev_ssm, w["conv_w"], w["conv_b"],
                              w["dt_bias"], w["a_log"], w["d_x"], w["g_ssd"])
    s_conv_new = _last_conv_rows(proj, bsz, n)
    return (ya, yb, yc), gla_out, ssm_out, s_conv_new, v_rows.reshape(bsz, n, CM_WIDTH)


def kernel(x_prompt, x_sample, state_gla, state_ssm, state_conv, p_prompt, p_sample, g_mix, w_in, w_gla_lr, b_gla_lr, g_gla_norm, cm_ln_g, cm_ln_b, cm_ws, cm_bs, ssd_conv_w, ssd_conv_b, ssd_dt_bias, ssd_a_log, ssd_d, g_ssd_norm, w_br_gla, w_br_cm, w_br_ssd, w_o, g_ffn, w_ffn_in, w_ffn_out, g_ple, w_ple_gate, w_ple_proj, g_final):
    weights = (g_mix, w_in, w_gla_lr, b_gla_lr, g_gla_norm, cm_ln_g, cm_ln_b, cm_ws, cm_bs,
               ssd_conv_w, ssd_conv_b, ssd_dt_bias, ssd_a_log, ssd_d, g_ssd_norm,
               w_br_gla, w_br_cm, w_br_ssd, w_o, g_ffn, w_ffn_in, w_ffn_out,
               g_ple, w_ple_gate, w_ple_proj)
    bp, lp, _ = x_prompt.shape
    bs, ls, _ = x_sample.shape
    assert ls == SAMPLE_LEN and lp % CHUNK == 0 and bs % 2 == 0
    xp = x_prompt.reshape(bp * lp, D_MODEL)
    xs = x_sample.reshape(bs * ls, D_MODEL)
    g_first = g_mix[0].reshape(1, D_MODEL)
    np_ = _rms_cast(xp, g_first, 512, BF16)
    ns = _rms_cast(xs, g_first, 512, BF16)
    w_in_t = jnp.swapaxes(w_in, 1, 2)
    gla_p, ssm_p, conv_p = [], [], []
    conv_s, v_s = [], []
    ssm_all = state_ssm.reshape(DEPTH, bs, SSD_GROUPS, SSD_GROUP_WIDTH, SSD_STATE)
    gla_s = ssm_s = None
    for i in range(DEPTH):
        final = i == DEPTH - 1
        g_next = g_final if final else g_mix[i + 1]
        w = _layer_weights(i, w_in_t, g_next, *weights)
        proj_p = _proj_matmul(np_, w_in_t, i, 1024)
        proj_s = _proj_matmul(ns, w_in_t, i, 512)
        small_p = _matmul_small_nt(np_, w["w_in_narrow"], 512)
        small_s = _matmul_small_nt(ns, w["w_in_narrow"], 512)
        ys_p, sg, sm, sc = _prompt_mixers(proj_p, small_p, bp, lp, w)
        gla_p.append(sg)
        ssm_p.append(sm)
        conv_p.append(sc)
        ys_s, gla_s, ssm_s, sc, vr = _sample_mixers(proj_s, small_s, bs, state_gla, ssm_all, state_conv[i],
                                                    gla_s, ssm_s, w)
        conv_s.append(sc)
        v_s.append(vr)
        xp, n_ffn_p = _merge_out(xp, proj_p, *ys_p, w)
        xs, n_ffn_s = _merge_out(xs, proj_s, *ys_s, w)
        h_p = _ffn_in(n_ffn_p, w_ffn_in, i, 1024, 512)
        h_s = _ffn_in(n_ffn_s, w_ffn_in, i, 512, 512)
        outs = _ffn_ple(xp, h_p, p_prompt[i].reshape(bp * lp, PLE_DIM), w, final)
        xp, np_ = (outs[0], None) if final else outs
        outs = _ffn_ple(xs, h_s, p_sample[i].reshape(bs * ls, PLE_DIM), w, final)
        xs, ns = (outs[0], None) if final else outs
    y_prompt = xp.reshape(bp, lp, D_MODEL)
    y_sample = xs.reshape(bs, ls, D_MODEL)
    ssm_s = ssm_s.reshape(DEPTH, bs, SSD_HEADS, SSD_HEADDIM, SSD_STATE)
    return (y_prompt, y_sample, jnp.stack(gla_p), gla_s, jnp.stack(ssm_p), ssm_s,
            jnp.stack(conv_p), jnp.stack(conv_s), jnp.stack(v_s))
```

```python
import functools

import numpy as np
import jax
import jax.numpy as jnp
from jax import lax
from jax.experimental import pallas as pl
from jax.experimental.pallas import tpu as pltpu

F32 = jnp.float32
BF16 = jnp.bfloat16

D_MODEL = 2048
DEPTH = 2
GLA_HEADS = 4
GLA_DK = 256
GLA_DV = 512
GLA_QK = 1024
GLA_V = 2048
GLA_LOWRANK = 16
GLA_TAU = 16.0
CM_GROUPS = 8
CM_WIDTH = 2048
CM_GROUP_DIM = 256
CM_CHUNK = 128
SSD_INNER = 4096
SSD_HEADDIM = 64
SSD_HEADS = 64
SSD_GROUPS = 8
SSD_STATE = 128
SSD_CONV = 4
SSD_CONV_DIM = 6144
SSD_GROUP_WIDTH = SSD_INNER // SSD_GROUPS
D_FF = 5632
PLE_DIM = 256
EPS = 1e-6

LANES = 128
SUBLANES = 8
VMEM_LIMIT = 56 * 1024 * 1024

C_Q = 0
C_K = 1024
C_V = 2048
C_G = 4096
C_UV = 6144
C_Z = 10240
C_XBC = 14336
C_GATE = 20480
C_LR = 26624
SMALL_LR = 0
SMALL_DT = 1

CHUNK = 128
SAMPLE_ROWS = 8
SAMPLE_LEN = 4
SAMPLE_SHIFT = 2
HEADDIM_SHIFT = 6


def _params(sem):
    return pltpu.CompilerParams(dimension_semantics=sem, vmem_limit_bytes=VMEM_LIMIT)


def _nt(a, b):
    return lax.dot_general(a, b, (((1,), (1,)), ((), ())), preferred_element_type=F32)


def _tn(a, b):
    return lax.dot_general(a, b, (((0,), (0,)), ((), ())), preferred_element_type=F32)


def _dot(a, b):
    return jnp.dot(a, b, preferred_element_type=F32)


def _split2(x):
    hi = x.astype(BF16)
    lo = (x - hi.astype(F32)).astype(BF16)
    return hi, lo


def _softplus(x):
    return jnp.maximum(x, 0.0) + jnp.log(1.0 + jnp.exp(-jnp.abs(x)))


def _log_sigmoid(x):
    return jnp.minimum(x, 0.0) - jnp.log(1.0 + jnp.exp(-jnp.abs(x)))


def _silu(x):
    return x * jax.nn.sigmoid(x)


def _rms_cast_body(x_ref, g_ref, wn_ref, o_ref, small_ref):
    x = x_ref[...]
    ms = jnp.mean(x * x, axis=-1, keepdims=True)
    n = (x * lax.rsqrt(ms + EPS) * g_ref[...]).astype(BF16)
    o_ref[...] = n
    small_ref[...] = _nt(n, wn_ref[...].astype(BF16))


def _rms_cast(x, g, w_narrow, tm):
    t, k = x.shape
    nn = w_narrow.shape[0]
    return pl.pallas_call(
        _rms_cast_body,
        grid=(t // tm,),
        in_specs=[pl.BlockSpec((tm, k), lambda i: (i, 0)), pl.BlockSpec((1, k), lambda i: (0, 0)),
                  pl.BlockSpec((nn, k), lambda i: (0, 0))],
        out_specs=[pl.BlockSpec((tm, k), lambda i: (i, 0)), pl.BlockSpec((tm, nn), lambda i: (i, 0))],
        out_shape=[jax.ShapeDtypeStruct((t, k), BF16), jax.ShapeDtypeStruct((t, nn), F32)],
        compiler_params=_params(("parallel",)),
        name="rms_cast",
    )(x, g, w_narrow)


NATIVE_UV = 2 * GLA_QK + 2 * GLA_V + GLA_LOWRANK
NATIVE_DT = NATIVE_UV + 2 * CM_WIDTH + SSD_INNER + SSD_CONV_DIM
NATIVE_GATE = NATIVE_DT + SSD_HEADS
PROJ_TN = 1024
W_ROW_CHUNK = 256


def _proj_body(n_ref, w_ref, wn_ref, o_ref, w_scr):
    j = pl.program_id(0)
    tn = w_ref.shape[0]

    def load_weights(shift):
        for r in range(0, tn - shift, W_ROW_CHUNK):
            rows = min(W_ROW_CHUNK, tn - shift - r)
            w_scr[r:r + rows, :] = w_ref[r + shift:r + shift + rows, :].astype(BF16)
        if shift:
            w_scr[tn - shift:tn, :] = wn_ref[0:shift, :].astype(BF16)

    @pl.when(pl.program_id(1) == 0)
    def _():
        j_uv = C_UV // PROJ_TN
        j_gate = C_GATE // PROJ_TN
        pl.when(j < j_uv)(lambda: load_weights(0))
        pl.when((j >= j_uv) & (j < j_gate))(lambda: load_weights(NATIVE_UV - C_UV))
        pl.when(j >= j_gate)(lambda: load_weights(NATIVE_GATE - C_GATE))

    o_ref[...] = _nt(n_ref[...], w_scr[...])


def _proj_matmul(n, w_in_t, layer, tm):
    t, k = n.shape
    tn = PROJ_TN
    return pl.pallas_call(
        _proj_body,
        grid=(C_LR // tn, t // tm),
        in_specs=[
            pl.BlockSpec((tm, k), lambda j, i: (i, 0)),
            pl.BlockSpec((None, tn, k), lambda j, i: (layer, j, 0)),
            pl.BlockSpec((None, LANES, k), lambda j, i: (layer, (j + 1) * (tn // LANES), 0)),
        ],
        out_specs=pl.BlockSpec((tm, tn), lambda j, i: (i, j)),
        out_shape=jax.ShapeDtypeStruct((t, C_LR), F32),
        scratch_shapes=[pltpu.VMEM((tn, k), BF16)],
        compiler_params=_params(("parallel", "arbitrary")),
        name="proj_matmul",
    )(n, w_in_t, w_in_t)


def _ffn_in_body(n_ref, wg_ref, wu_ref, o_ref, wg_scr, wu_scr):
    @pl.when(pl.program_id(1) == 0)
    def _():
        for r in range(0, wg_ref.shape[0], W_ROW_CHUNK):
            rs = slice(r, r + W_ROW_CHUNK)
            wg_scr[rs, :] = wg_ref[rs, :].astype(BF16)
            wu_scr[rs, :] = wu_ref[rs, :].astype(BF16)

    n = n_ref[...]
    gate = _dot(n, wg_scr[...])
    up = _dot(n, wu_scr[...])
    o_ref[...] = (_silu(gate) * up).astype(o_ref.dtype)


def _ffn_in(n, w, layer, tm, tn):
    t, k = n.shape
    nblk = D_FF // tn
    return pl.pallas_call(
        _ffn_in_body,
        grid=(nblk, t // tm),
        in_specs=[
            pl.BlockSpec((tm, k), lambda j, i: (i, 0)),
            pl.BlockSpec((None, k, tn), lambda j, i: (layer, 0, j)),
            pl.BlockSpec((None, k, tn), lambda j, i: (layer, 0, j + nblk)),
        ],
        out_specs=pl.BlockSpec((tm, tn), lambda j, i: (i, j)),
        out_shape=jax.ShapeDtypeStruct((t, D_FF), BF16),
        scratch_shapes=[pltpu.VMEM((k, tn), BF16), pltpu.VMEM((k, tn), BF16)],
        compiler_params=_params(("parallel", "arbitrary")),
        name="ffn_in",
    )(n, w, w)


def _mm_res_body(a_ref, w_ref, r_ref, o_ref, w_scr):
    @pl.when(pl.program_id(1) == 0)
    def _():
        for r in range(0, w_ref.shape[0], W_ROW_CHUNK):
            rs = slice(r, r + W_ROW_CHUNK)
            w_scr[rs, :] = w_ref[rs, :].astype(BF16)

    o_ref[...] = r_ref[...] + _dot(a_ref[...], w_scr[...])


def _matmul_residual(a, w, layer, res, tm, tn):
    t, k = a.shape
    n = w.shape[2]
    return pl.pallas_call(
        _mm_res_body,
        grid=(n // tn, t // tm),
        in_specs=[
            pl.BlockSpec((tm, k), lambda j, i: (i, 0)),
            pl.BlockSpec((None, k, tn), lambda j, i: (layer, 0, j)),
            pl.BlockSpec((tm, tn), lambda j, i: (i, j)),
        ],
        out_specs=pl.BlockSpec((tm, tn), lambda j, i: (i, j)),
        out_shape=jax.ShapeDtypeStruct((t, n), F32),
        scratch_shapes=[pltpu.VMEM((k, tn), BF16)],
        compiler_params=_params(("parallel", "arbitrary")),
        name="matmul_residual",
    )(a, w, res)


def _merge_body(ya_ref, yb_ref, yc_ref, ga_ref, gb_ref, gc_ref, wa_ref, wb_ref, wc_ref, o_ref):
    acc = jax.nn.sigmoid(ga_ref[...]) * _dot(ya_ref[...].astype(BF16), wa_ref[...])
    acc += jax.nn.sigmoid(gb_ref[...]) * _dot(yb_ref[...].astype(BF16), wb_ref[...])
    acc += jax.nn.sigmoid(gc_ref[...]) * _dot(yc_ref[...].astype(BF16), wc_ref[...])
    o_ref[...] = acc.astype(o_ref.dtype)


def _resident(shape, index_map):
    return pl.BlockSpec(shape, index_map, pipeline_mode=pl.Buffered(1))


def _merge(ya, yb, yc, proj, wa, wb, wc, tm, tn):
    t = ya.shape[0]
    nblk = D_MODEL // tn
    g0 = C_GATE // tn
    return pl.pallas_call(
        _merge_body,
        grid=(nblk, t // tm),
        in_specs=[
            pl.BlockSpec((tm, GLA_V), lambda j, i: (i, 0)),
            pl.BlockSpec((tm, CM_WIDTH), lambda j, i: (i, 0)),
            pl.BlockSpec((tm, SSD_INNER), lambda j, i: (i, 0)),
            pl.BlockSpec((tm, tn), lambda j, i: (i, g0 + j)),
            pl.BlockSpec((tm, tn), lambda j, i: (i, g0 + nblk + j)),
            pl.BlockSpec((tm, tn), lambda j, i: (i, g0 + 2 * nblk + j)),
            _resident((GLA_V, tn), lambda j, i: (0, j)),
            _resident((CM_WIDTH, tn), lambda j, i: (0, j)),
            _resident((SSD_INNER, tn), lambda j, i: (0, j)),
        ],
        out_specs=pl.BlockSpec((tm, tn), lambda j, i: (i, j)),
        out_shape=jax.ShapeDtypeStruct((t, D_MODEL), BF16),
        compiler_params=_params(("parallel", "arbitrary")),
        name="merge",
    )(ya, yb, yc, proj, proj, proj, wa, wb, wc)


def _rms(x, g):
    ms = jnp.mean(x * x, axis=-1, keepdims=True)
    return x * lax.rsqrt(ms + EPS) * g


def _out_proj_body(mix_ref, w_ref, x_ref, g_ref, x1_ref, n_ref):
    x1 = x_ref[...] + _dot(mix_ref[...], w_ref[...])
    x1_ref[...] = x1
    n_ref[...] = _rms(x1, g_ref[...]).astype(n_ref.dtype)


def _out_proj(mix, w, x, g_next, tm):
    t, k = mix.shape
    row_block = lambda width: pl.BlockSpec((tm, width), lambda i: (i, 0))
    return pl.pallas_call(
        _out_proj_body,
        grid=(t // tm,),
        in_specs=[
            row_block(k),
            _resident((k, D_MODEL), lambda i: (0, 0)),
            row_block(D_MODEL),
            pl.BlockSpec((1, D_MODEL), lambda i: (0, 0)),
        ],
        out_specs=[row_block(D_MODEL), row_block(D_MODEL)],
        out_shape=[jax.ShapeDtypeStruct((t, D_MODEL), F32), jax.ShapeDtypeStruct((t, D_MODEL), BF16)],
        compiler_params=_params(("parallel",)),
        name="out_proj",
    )(mix, w, x, g_next)


def _ple_body(x_ref, g_ref, p_ref, wg_ref, wp_ref, gn_ref, *rest, final):
    x = x_ref[...]
    n = _rms(x, g_ref[...]).astype(BF16)
    gate = jax.nn.sigmoid(_dot(n, wg_ref[...]))
    emb = _dot(p_ref[...].astype(BF16), wp_ref[...])
    y = x + gate * emb
    if final:
        rest[0][...] = _rms(y, gn_ref[...])
    else:
        wn_ref, y_ref, n_ref, small_ref = rest
        y_ref[...] = y
        n_next = _rms(y, gn_ref[...]).astype(BF16)
        n_ref[...] = n_next
        small_ref[...] = _nt(n_next, wn_ref[...].astype(BF16))


def _ple(x, g, p_all, layer, wg, wp, g_next, w_narrow_next, tm):
    t = x.shape[0]
    final = w_narrow_next is None
    row_block = lambda width: pl.BlockSpec((tm, width), lambda i: (i, 0))
    whole = lambda a: _resident(a.shape, lambda i: (0, 0))
    in_specs = [row_block(D_MODEL), whole(g), pl.BlockSpec((None, tm, PLE_DIM), lambda i: (layer, i, 0)),
                whole(wg), whole(wp), whole(g_next)]
    args = [x, g, p_all, wg, wp, g_next]
    if final:
        out_specs = [row_block(D_MODEL)]
        out_shape = [jax.ShapeDtypeStruct((t, D_MODEL), F32)]
    else:
        nn = w_narrow_next.shape[0]
        in_specs.append(whole(w_narrow_next))
        args.append(w_narrow_next)
        out_specs = [row_block(D_MODEL), row_block(D_MODEL), row_block(nn)]
        out_shape = [jax.ShapeDtypeStruct((t, D_MODEL), F32), jax.ShapeDtypeStruct((t, D_MODEL), BF16),
                     jax.ShapeDtypeStruct((t, nn), F32)]
    return pl.pallas_call(
        functools.partial(_ple_body, final=final),
        grid=(t // tm,),
        in_specs=in_specs,
        out_specs=out_specs,
        out_shape=out_shape,
        compiler_params=_params(("parallel",)),
        name="ple",
    )(*args)


CM_STEP_CHUNKS = 2


def _cmlp_body(u_ref, v_ref, lng_ref, lnb_ref, ws_ref, bst_ref, yb_ref, *rest, seq_len):
    r = lax.broadcasted_iota(jnp.int32, (CM_CHUNK, CM_CHUNK), 0)
    c = lax.broadcasted_iota(jnp.int32, (CM_CHUNK, CM_CHUNK), 1)
    if seq_len >= CM_CHUNK:
        keep = r >= c
    else:
        sh = seq_len.bit_length() - 1
        keep = ((r >> sh) == (c >> sh)) & ((r & (seq_len - 1)) >= (c & (seq_len - 1)))
    ws = [jnp.where(keep, ws_ref[g], 0.0).astype(BF16) for g in range(CM_GROUPS)]
    for k in range(CM_STEP_CHUNKS):
        rows = slice(k * CM_CHUNK, (k + 1) * CM_CHUNK)
        u = jax.nn.gelu(u_ref[rows, :])
        v = jax.nn.gelu(v_ref[rows, :])
        mu = jnp.mean(v, axis=-1, keepdims=True)
        vc = v - mu
        var = jnp.mean(vc * vc, axis=-1, keepdims=True)
        vn = vc * lax.rsqrt(var + EPS) * lng_ref[...] + lnb_ref[...]
        if rest:
            rest[0][rows, :] = vn
        for g in range(CM_GROUPS):
            sl = slice(g * CM_GROUP_DIM, (g + 1) * CM_GROUP_DIM)
            mixed = _dot(ws[g], vn[:, sl].astype(BF16)) + bst_ref[:, g:g + 1]
            yb_ref[rows, sl] = (u[:, sl] * mixed).astype(yb_ref.dtype)


def _cmlp(proj, ln_g, ln_b, ws_tiled, bs_t, seq_len, emit_v, out_dtype):
    t = proj.shape[0]
    step_rows = CM_STEP_CHUNKS * CM_CHUNK
    out_shape = [jax.ShapeDtypeStruct((t, CM_WIDTH), out_dtype)]
    out_specs = [pl.BlockSpec((step_rows, CM_WIDTH), lambda i: (i, 0))]
    if emit_v:
        out_shape.append(jax.ShapeDtypeStruct((t, CM_WIDTH), F32))
        out_specs.append(pl.BlockSpec((step_rows, CM_WIDTH), lambda i: (i, 0)))
    cu = C_UV // CM_WIDTH
    res = pl.pallas_call(
        functools.partial(_cmlp_body, seq_len=seq_len),
        grid=(t // step_rows,),
        in_specs=[
            pl.BlockSpec((step_rows, CM_WIDTH), lambda i: (i, cu)),
            pl.BlockSpec((step_rows, CM_WIDTH), lambda i: (i, cu + 1)),
            pl.BlockSpec((1, CM_WIDTH), lambda i: (0, 0)),
            pl.BlockSpec((1, CM_WIDTH), lambda i: (0, 0)),
            pl.BlockSpec((CM_GROUPS, CM_CHUNK, CM_CHUNK), lambda i: (0, 0, 0)),
            pl.BlockSpec((CM_CHUNK, CM_GROUPS), lambda i: (0, 0)),
        ],
        out_specs=out_specs,
        out_shape=out_shape,
        compiler_params=_params(("parallel",)),
        name="chunk_mlp",
    )(proj, proj, ln_g, ln_b, ws_tiled, bs_t)
    return (res[0], res[1]) if emit_v else (res[0], None)


def _gla_level_constants(n):
    levels = n.bit_length() - 1
    t = np.arange(n)[:, None]
    j = np.arange(n)[None, :]
    sums = [j <= t, j > t]
    masks = [t == j]
    for l in range(1, levels + 1):
        w, half = 1 << l, 1 << (l - 1)
        start = (t >> l) << l
        upper = ((t >> (l - 1)) & 1) == 1
        a_up = (j >= start + half) & (j <= t)
        a_lo = (j > t) & (j < start + half)
        sums.append(np.where(upper, a_up, a_lo))
        jj = j
        upper_t = ((t >> (l - 1)) & 1) == 1
        lower_s = ((jj >> (l - 1)) & 1) == 0
        masks.append(((t >> l) == (jj >> l)) & upper_t & lower_s)
    return (jnp.asarray(np.stack(sums), BF16), jnp.asarray(np.stack(masks), F32), levels)


def _gla_log_decay(lr_ref, wlr_ref, blr_ref):
    x = _dot(lr_ref[...].astype(BF16), wlr_ref[...]) + blr_ref[...]
    return _log_sigmoid(x) * (1.0 / GLA_TAU)


def _gla_finish(o, g, gn):
    ms = jnp.mean(o * o, axis=-1, keepdims=True)
    return o * lax.rsqrt(ms + EPS) * gn * _silu(g)


def _gla_prompt_body(q_ref, k_ref, v_ref, g_ref, lr_ref, wlr_ref, blr_ref, gn_ref, a_ref, m_ref,
                     ya_ref, s_ref, *, levels):
    rows = q_ref.shape[0]

    @pl.when(pl.program_id(1) == 0)
    def _():
        s_ref[...] = jnp.zeros(s_ref.shape, F32)

    q = q_ref[...] * (GLA_DK ** -0.5)
    k = k_ref[...]
    la = _gla_log_decay(lr_ref, wlr_ref, blr_ref)
    la2 = jnp.concatenate(_split2(la), axis=1)

    def decay_sum(i):
        d = _dot(a_ref[i], la2)
        return d[:, :GLA_QK] + d[:, GLA_QK:]

    b = decay_sum(0)
    qe = (q * jnp.exp(b)).astype(BF16)
    kd = (k * jnp.exp(decay_sum(1))).astype(BF16)
    e_end_t = jnp.exp(jnp.broadcast_to(b[rows - 1:rows, :], (SUBLANES, GLA_QK))).T
    row = lax.broadcasted_iota(jnp.int32, (rows, GLA_QK), 0)
    xs = []
    for l in range(1, levels + 1):
        e = jnp.exp(decay_sum(l + 1))
        upper = ((row >> (l - 1)) & 1) == 1
        xs.append((jnp.where(upper, q, k) * e).astype(BF16))
    qb = q.astype(BF16)
    kb = k.astype(BF16)
    for h in range(GLA_HEADS):
        ks = slice(h * GLA_DK, (h + 1) * GLA_DK)
        vs = slice(h * GLA_DV, (h + 1) * GLA_DV)
        att = m_ref[0] * _nt(qb[:, ks], kb[:, ks])
        for l in range(1, levels + 1):
            x = xs[l - 1][:, ks]
            att += m_ref[l] * _nt(x, x)
        vh = v_ref[:, vs].astype(BF16)
        s = s_ref[0, h]
        o = _dot(qe[:, ks], s.astype(BF16)) + _dot(att.astype(BF16), vh)
        s_ref[0, h] = s * e_end_t[ks, 0:1] + _tn(kd[:, ks], vh)
        ya_ref[:, vs] = _gla_finish(o, g_ref[:, vs], gn_ref[:, vs]).astype(ya_ref.dtype)


def _gla_prompt(proj, small, bsz, length, w_lr, b_lr, gn):
    nchunk = length // CHUNK
    a_mats, masks, levels = _gla_level_constants(CHUNK)
    row = lambda b, c: b * nchunk + c
    return pl.pallas_call(
        functools.partial(_gla_prompt_body, levels=levels),
        grid=(bsz, nchunk),
        in_specs=[
            pl.BlockSpec((CHUNK, GLA_QK), lambda b, c: (row(b, c), C_Q // GLA_QK)),
            pl.BlockSpec((CHUNK, GLA_QK), lambda b, c: (row(b, c), C_K // GLA_QK)),
            pl.BlockSpec((CHUNK, GLA_V), lambda b, c: (row(b, c), C_V // GLA_V)),
            pl.BlockSpec((CHUNK, GLA_V), lambda b, c: (row(b, c), C_G // GLA_V)),
            pl.BlockSpec((CHUNK, LANES), lambda b, c: (row(b, c), SMALL_LR)),
            pl.BlockSpec((LANES, GLA_QK), lambda b, c: (0, 0)),
            pl.BlockSpec((1, GLA_QK), lambda b, c: (0, 0)),
            pl.BlockSpec((1, GLA_V), lambda b, c: (0, 0)),
            pl.BlockSpec(a_mats.shape, lambda b, c: (0, 0, 0)),
            pl.BlockSpec(masks.shape, lambda b, c: (0, 0, 0)),
        ],
        out_specs=[
            pl.BlockSpec((CHUNK, GLA_V), lambda b, c: (row(b, c), 0)),
            pl.BlockSpec((1, GLA_HEADS, GLA_DK, GLA_DV), lambda b, c: (b, 0, 0, 0)),
        ],
        out_shape=[
            jax.ShapeDtypeStruct((bsz * length, GLA_V), BF16),
            jax.ShapeDtypeStruct((bsz, GLA_HEADS, GLA_DK, GLA_DV), F32),
        ],
        compiler_params=_params(("parallel", "arbitrary")),
        name="gla_prompt",
    )(proj, proj, proj, proj, small, w_lr, b_lr, gn, a_mats, masks)


def _roll_rows(x, d):
    return pltpu.roll(x, d, 0) if d else x


def _gla_sample_body(q_ref, k_ref, v_ref, g_ref, lr_ref, wlr_ref, blr_ref, gn_ref, s0_ref,
                     ya_ref, s_ref):
    n = SAMPLE_LEN
    pos_k = lax.broadcasted_iota(jnp.int32, (SAMPLE_ROWS, GLA_QK), 0) & (n - 1)
    row_v = lax.broadcasted_iota(jnp.int32, (SAMPLE_ROWS, GLA_DV), 0)
    q = q_ref[...] * (GLA_DK ** -0.5)
    k = k_ref[...]
    v = v_ref[...]
    la = _gla_log_decay(lr_ref, wlr_ref, blr_ref)
    back = [_roll_rows(la, d) for d in range(n)]
    win = [None, la]
    for d in range(2, n):
        win.append(win[-1] + back[d - 1])
    b = la
    suffix = jnp.zeros_like(la)
    for d in range(1, n):
        b = b + jnp.where(pos_k >= d, back[d], 0.0)
        suffix = suffix + jnp.where(pos_k < n - d, pltpu.roll(la, SAMPLE_ROWS - d, 0), 0.0)
    eb = jnp.exp(b)
    qe = (q * eb).astype(BF16)
    kd = k * jnp.exp(suffix)
    eb_t = eb.T
    nseq = SAMPLE_ROWS // n
    seq_k = lax.broadcasted_iota(jnp.int32, (SAMPLE_ROWS, GLA_QK), 0) >> SAMPLE_SHIFT
    kd_seq = [jnp.where(seq_k == j, kd, 0.0).astype(BF16) for j in range(nseq)]
    pair = []
    for d in range(n):
        p = q * _roll_rows(k, d)
        if d:
            p = p * jnp.exp(win[d])
        pair.append(p)
    vback = [_roll_rows(v, d) for d in range(n)]
    for h in range(GLA_HEADS):
        ks = slice(h * GLA_DK, (h + 1) * GLA_DK)
        vs = slice(h * GLA_DV, (h + 1) * GLA_DV)
        o = jnp.zeros((SAMPLE_ROWS, GLA_DV), F32)
        for d in range(n):
            score = jnp.sum(pair[d][:, ks], axis=-1, keepdims=True)
            o = o + jnp.where((row_v & (n - 1)) >= d, score * vback[d][:, vs], 0.0)
        vh = v[:, vs].astype(BF16)
        for j in range(nseq):
            s = s0_ref[j, h]
            o = o + jnp.where((row_v >> SAMPLE_SHIFT) == j, _dot(qe[:, ks], s.astype(BF16)), 0.0)
            col = n * j + n - 1
            s_ref[j, h] = s * eb_t[ks, col:col + 1] + _tn(kd_seq[j][:, ks], vh)
        ya_ref[:, vs] = _gla_finish(o, g_ref[:, vs], gn_ref[:, vs]).astype(ya_ref.dtype)


def _stacked_state_call(body, prev_out, n_in):
    if prev_out is None:
        return body, [], [], {}
    wrapped = lambda *refs: body(*refs[:n_in], *refs[n_in + 1:])
    return wrapped, [pl.BlockSpec(memory_space=pl.ANY)], [prev_out], {n_in: 1}


def _gla_sample(proj, small, state_all, layer, prev_out, w_lr, b_lr, gn):
    t = proj.shape[0]
    nseq = SAMPLE_ROWS // SAMPLE_LEN
    state_spec = pl.BlockSpec((None, nseq, GLA_HEADS, GLA_DK, GLA_DV), lambda i: (layer, i, 0, 0, 0))
    body, extra_specs, extra_args, aliases = _stacked_state_call(_gla_sample_body, prev_out, 9)
    return pl.pallas_call(
        body,
        grid=(t // SAMPLE_ROWS,),
        in_specs=[
            pl.BlockSpec((SAMPLE_ROWS, GLA_QK), lambda i: (i, C_Q // GLA_QK)),
            pl.BlockSpec((SAMPLE_ROWS, GLA_QK), lambda i: (i, C_K // GLA_QK)),
            pl.BlockSpec((SAMPLE_ROWS, GLA_V), lambda i: (i, C_V // GLA_V)),
            pl.BlockSpec((SAMPLE_ROWS, GLA_V), lambda i: (i, C_G // GLA_V)),
            pl.BlockSpec((SAMPLE_ROWS, LANES), lambda i: (i, SMALL_LR)),
            pl.BlockSpec((LANES, GLA_QK), lambda i: (0, 0)),
            pl.BlockSpec((1, GLA_QK), lambda i: (0, 0)),
            pl.BlockSpec((1, GLA_V), lambda i: (0, 0)),
            state_spec,
        ] + extra_specs,
        out_specs=[pl.BlockSpec((SAMPLE_ROWS, GLA_V), lambda i: (i, 0)), state_spec],
        out_shape=[
            jax.ShapeDtypeStruct((t, GLA_V), F32),
            jax.ShapeDtypeStruct(state_all.shape, F32),
        ],
        input_output_aliases=aliases,
        compiler_params=_params(("parallel",)),
        name="gla_sample",
    )(proj, proj, proj, proj, small, w_lr, b_lr, gn, state_all, *extra_args)


XBC_PART = 2048


def _ssd_gate_norm(y, z, gn):
    yz = y * _silu(z)
    ms = jnp.mean(yz * yz, axis=-1, keepdims=True)
    return yz * lax.rsqrt(ms + EPS) * gn


def _ssd_prompt_body(x1_ref, x2_ref, bc_ref, z1_ref, z2_ref, dt_ref, cw_ref, cb_ref, dtb_ref, alog_ref,
                     dx_ref, gn_ref, e_ref, a_ref, hm_ref, yc_ref, h_ref, tail_scr, ht_scr):
    rows = x1_ref.shape[0]
    step = pl.program_id(1)

    @pl.when(step == 0)
    def _():
        tail_scr[...] = jnp.zeros(tail_scr.shape, F32)
        ht_scr[...] = jnp.zeros(ht_scr.shape, F32)

    row8 = lax.broadcasted_iota(jnp.int32, (SUBLANES, XBC_PART), 0)

    def conv(x_ref, part):
        cs = slice(part * XBC_PART, (part + 1) * XBC_PART)
        x = x_ref[...]
        prev = tail_scr[:, cs]
        acc = cb_ref[:, cs] + x * cw_ref[SSD_CONV - 1:SSD_CONV, cs]
        for d in range(1, SSD_CONV):
            xr = pltpu.roll(x, d, 0)
            first = jnp.where(row8 < d, pltpu.roll(prev, d, 0), xr[:SUBLANES])
            xd = jnp.concatenate([first, xr[SUBLANES:]], axis=0)
            acc = acc + xd * cw_ref[SSD_CONV - 1 - d:SSD_CONV - d, cs]
        tail_scr[:, cs] = x[rows - SUBLANES:, :]
        return _silu(acc)

    xs_halves = [conv(x1_ref, 0), conv(x2_ref, 1)]
    bc = conv(bc_ref, 2)
    half_w = SSD_GROUPS * SSD_STATE
    bmat = bc[:, :half_w]
    cmat = bc[:, half_w:]
    z_halves = [z1_ref, z2_ref]

    dt = _softplus(dt_ref[...] + dtb_ref[...])
    a = dt * (-jnp.exp(alog_ref[...]))
    a2 = jnp.concatenate(_split2(a), axis=1)

    def head_sum(i):
        d = _dot(a_ref[i], a2)
        return d[:, :LANES] + d[:, LANES:]

    cs_in = head_sum(0)
    cs_suf = head_sum(1)
    cs_t = cs_in.T
    cs_hi, cs_lo = _split2(cs_in)
    suf_hi, suf_lo = _split2(cs_suf)
    dt_hi, dt_lo = _split2(dt)
    r_i = lax.broadcasted_iota(jnp.int32, (rows, rows), 0)
    c_i = lax.broadcasted_iota(jnp.int32, (rows, rows), 1)
    causal = r_i >= c_i
    heads_per_group = SSD_HEADS // SSD_GROUPS
    per_half = SSD_GROUPS // 2
    for g in range(SSD_GROUPS):
        gs = slice(g * SSD_GROUP_WIDTH, (g + 1) * SSD_GROUP_WIDTH)
        ls = slice((g % per_half) * SSD_GROUP_WIDTH, (g % per_half + 1) * SSD_GROUP_WIDTH)
        ns = slice(g * SSD_STATE, (g + 1) * SSD_STATE)
        xg = xs_halves[g // per_half][:, ls]
        eg = e_ref[:, gs]
        csx = _dot(cs_hi, eg) + _dot(cs_lo, eg)
        sufx = _dot(suf_hi, eg) + _dot(suf_lo, eg)
        dtx = _dot(dt_hi, eg) + _dot(dt_lo, eg)
        bg = bmat[:, ns].astype(BF16)
        cg = cmat[:, ns].astype(BF16)
        cb = jnp.where(causal, _nt(cg, bg), 0.0)
        xdt = dtx * xg
        xdt_b = xdt.astype(BF16)
        ws = []
        xb = []
        for r in range(heads_per_group):
            h = g * heads_per_group + r
            dm = cs_in[:, h:h + 1] - cs_t[h:h + 1, :]
            ws.append((cb * jnp.exp(jnp.minimum(dm, 0.0))).astype(BF16))
            xb.append(xdt_b * hm_ref[r])
        y = _dot(jnp.concatenate(ws, axis=1), jnp.concatenate(xb, axis=0))
        ht = ht_scr[g]
        y = y + _dot(cg, ht.astype(BF16)) * jnp.exp(csx)
        y = y + dx_ref[:, gs] * xg
        wx = (jnp.exp(sufx) * xdt).astype(BF16)
        ht_scr[g] = ht * jnp.exp(csx[rows - 1:rows, :]) + _tn(bg, wx)
        z = z_halves[g // per_half][:, ls]
        yc_ref[:, gs] = _ssd_gate_norm(y, z, gn_ref[:, gs]).astype(yc_ref.dtype)

    @pl.when(step == pl.num_programs(1) - 1)
    def _():
        for g in range(SSD_GROUPS):
            h_ref[0, g] = ht_scr[g].T


def _tri_constants(n):
    t = np.arange(n)[:, None]
    j = np.arange(n)[None, :]
    return jnp.asarray(np.stack([j <= t, j > t]), BF16)


def _head_lane_masks(rows):
    lane = np.arange(SSD_GROUP_WIDTH)[None, None, :]
    r = np.arange(SSD_HEADS // SSD_GROUPS)[:, None, None]
    return jnp.asarray(np.broadcast_to(lane // SSD_HEADDIM == r, (SSD_HEADS // SSD_GROUPS, rows, SSD_GROUP_WIDTH)), BF16)


def _head_expand_matrix():
    h = np.arange(LANES)[:, None]
    lane = np.arange(SSD_INNER)[None, :]
    return jnp.asarray(h == lane // SSD_HEADDIM, BF16)


def _ssd_prompt(proj, small, bsz, length, conv_w, conv_b, dt_bias, a_log, d_x, gn):
    nchunk = length // CHUNK
    row = lambda b, c: b * nchunk + c
    cx = C_XBC // XBC_PART
    cz = C_Z // XBC_PART
    tri = _tri_constants(CHUNK)
    e = _head_expand_matrix()
    head_masks = _head_lane_masks(CHUNK)
    full = lambda shape: pl.BlockSpec(shape, lambda b, c: (0,) * len(shape))
    return pl.pallas_call(
        _ssd_prompt_body,
        grid=(bsz, nchunk),
        in_specs=[
            pl.BlockSpec((CHUNK, XBC_PART), lambda b, c: (row(b, c), cx)),
            pl.BlockSpec((CHUNK, XBC_PART), lambda b, c: (row(b, c), cx + 1)),
            pl.BlockSpec((CHUNK, XBC_PART), lambda b, c: (row(b, c), cx + 2)),
            pl.BlockSpec((CHUNK, XBC_PART), lambda b, c: (row(b, c), cz)),
            pl.BlockSpec((CHUNK, XBC_PART), lambda b, c: (row(b, c), cz + 1)),
            pl.BlockSpec((CHUNK, LANES), lambda b, c: (row(b, c), SMALL_DT)),
            full((SSD_CONV, SSD_CONV_DIM)),
            full((1, SSD_CONV_DIM)),
            full((1, LANES)),
            full((1, LANES)),
            full((1, SSD_INNER)),
            full((1, SSD_INNER)),
            full((LANES, SSD_INNER)),
            full(tri.shape),
            full(head_masks.shape),
        ],
        out_specs=[
            pl.BlockSpec((CHUNK, SSD_INNER), lambda b, c: (row(b, c), 0)),
            pl.BlockSpec((1, SSD_GROUPS, SSD_GROUP_WIDTH, SSD_STATE), lambda b, c: (b, 0, 0, 0)),
        ],
        out_shape=[
            jax.ShapeDtypeStruct((bsz * length, SSD_INNER), BF16),
            jax.ShapeDtypeStruct((bsz, SSD_GROUPS, SSD_GROUP_WIDTH, SSD_STATE), F32),
        ],
        scratch_shapes=[
            pltpu.VMEM((SUBLANES, SSD_CONV_DIM), F32),
            pltpu.VMEM((SSD_GROUPS, SSD_STATE, SSD_GROUP_WIDTH), F32),
        ],
        compiler_params=_params(("parallel", "arbitrary")),
        name="ssd_prompt",
    )(proj, proj, proj, proj, proj, small, conv_w, conv_b, dt_bias, a_log, d_x, gn, e, tri, head_masks)


def _ssd_sample_body(x1_ref, x2_ref, bc_ref, z1_ref, z2_ref, dt_ref, cw_ref, cb_ref, dtb_ref, alog_ref,
                     dx_ref, gn_ref, e_ref, prev_ref, h0_ref, yc_ref, h_ref):
    n = SAMPLE_LEN
    nseq = SAMPLE_ROWS // n
    pos_p = lax.broadcasted_iota(jnp.int32, (SAMPLE_ROWS, XBC_PART), 0) & (n - 1)

    def conv(x_ref, part):
        cs = slice(part * XBC_PART, (part + 1) * XBC_PART)
        x = x_ref[...]
        prev = prev_ref[0][:, cs]
        acc = cb_ref[:, cs] + x * cw_ref[SSD_CONV - 1:SSD_CONV, cs]
        for d in range(1, SSD_CONV):
            xd = jnp.where(pos_p < d, pltpu.roll(prev, d, 0), pltpu.roll(x, d, 0))
            acc = acc + xd * cw_ref[SSD_CONV - 1 - d:SSD_CONV - d, cs]
        return _silu(acc)

    xs = jnp.concatenate([conv(x1_ref, 0), conv(x2_ref, 1)], axis=1)
    bc = conv(bc_ref, 2)
    half_w = SSD_GROUPS * SSD_STATE
    bmat = bc[:, :half_w]
    cmat = bc[:, half_w:]
    z = jnp.concatenate([z1_ref[...], z2_ref[...]], axis=1)

    dt = _softplus(dt_ref[...] + dtb_ref[...])
    dt_hi, dt_lo = _split2(dt)
    dtx = _dot(dt_hi, e_ref[...]) + _dot(dt_lo, e_ref[...])
    a_hi, a_lo = _split2(-jnp.exp(alog_ref[...]))
    anegx = _dot(jnp.broadcast_to(a_hi, (SUBLANES, LANES)), e_ref[...]) + \
        _dot(jnp.broadcast_to(a_lo, (SUBLANES, LANES)), e_ref[...])
    ax = dtx * anegx
    pos = lax.broadcasted_iota(jnp.int32, (SAMPLE_ROWS, SSD_INNER), 0) & (n - 1)
    seq_g = lax.broadcasted_iota(jnp.int32, (SAMPLE_ROWS, SSD_GROUP_WIDTH), 0) >> SAMPLE_SHIFT
    back = [_roll_rows(ax, d) for d in range(n)]
    win = [None, ax]
    for d in range(2, n):
        win.append(win[-1] + back[d - 1])
    csx = ax
    sufx = jnp.zeros_like(ax)
    for d in range(1, n):
        csx = csx + jnp.where(pos >= d, back[d], 0.0)
        sufx = sufx + jnp.where(pos < n - d, pltpu.roll(ax, SAMPLE_ROWS - d, 0), 0.0)
    ecs = jnp.exp(csx)
    ecs_t = ecs.T
    wx = jnp.exp(sufx) * dtx * xs

    y = dx_ref[...] * xs
    for d in range(n):
        prod = cmat * _roll_rows(bmat, d)
        cbx = jnp.concatenate(
            [jnp.broadcast_to(jnp.sum(prod[:, g * SSD_STATE:(g + 1) * SSD_STATE], axis=-1, keepdims=True),
                              (SAMPLE_ROWS, SSD_GROUP_WIDTH)) for g in range(SSD_GROUPS)], axis=1)
        term = cbx * _roll_rows(dtx, d) * _roll_rows(xs, d)
        if d:
            term = term * jnp.exp(win[d])
        y = y + jnp.where(pos >= d, term, 0.0)

    y_inter = []
    for g in range(SSD_GROUPS):
        gs = slice(g * SSD_GROUP_WIDTH, (g + 1) * SSD_GROUP_WIDTH)
        ns = slice(g * SSD_STATE, (g + 1) * SSD_STATE)
        bg = bmat[:, ns].astype(BF16)
        cg = cmat[:, ns].astype(BF16)
        acc = jnp.zeros((SAMPLE_ROWS, SSD_GROUP_WIDTH), F32)
        for j in range(nseq):
            h0 = h0_ref[j, g]
            acc = acc + jnp.where(seq_g == j, _nt(cg, h0.astype(BF16)), 0.0)
            wxj = jnp.where(seq_g == j, wx[:, gs], 0.0).astype(BF16)
            col = n * j + n - 1
            h_ref[j, g] = h0 * ecs_t[gs, col:col + 1] + _tn(wxj, bg)
        y_inter.append(acc)
    y = y + jnp.concatenate(y_inter, axis=1) * ecs
    for g in range(SSD_GROUPS):
        gs = slice(g * SSD_GROUP_WIDTH, (g + 1) * SSD_GROUP_WIDTH)
        yc_ref[:, gs] = _ssd_gate_norm(y[:, gs], z[:, gs], gn_ref[:, gs]).astype(yc_ref.dtype)


def _ssd_sample(proj, small, conv_prev, state_all, layer, prev_out, conv_w, conv_b, dt_bias, a_log, d_x, gn):
    t = proj.shape[0]
    nseq = SAMPLE_ROWS // SAMPLE_LEN
    cx = C_XBC // XBC_PART
    cz = C_Z // XBC_PART
    e = _head_expand_matrix()
    full = lambda shape: pl.BlockSpec(shape, lambda i: (0,) * len(shape))
    state_spec = pl.BlockSpec((None, nseq, SSD_GROUPS, SSD_GROUP_WIDTH, SSD_STATE), lambda i: (layer, i, 0, 0, 0))
    body, extra_specs, extra_args, aliases = _stacked_state_call(_ssd_sample_body, prev_out, 15)
    return pl.pallas_call(
        body,
        grid=(t // SAMPLE_ROWS,),
        in_specs=[
            pl.BlockSpec((SAMPLE_ROWS, XBC_PART), lambda i: (i, cx)),
            pl.BlockSpec((SAMPLE_ROWS, XBC_PART), lambda i: (i, cx + 1)),
            pl.BlockSpec((SAMPLE_ROWS, XBC_PART), lambda i: (i, cx + 2)),
            pl.BlockSpec((SAMPLE_ROWS, XBC_PART), lambda i: (i, cz)),
            pl.BlockSpec((SAMPLE_ROWS, XBC_PART), lambda i: (i, cz + 1)),
            pl.BlockSpec((SAMPLE_ROWS, LANES), lambda i: (i, SMALL_DT)),
            full((SSD_CONV, SSD_CONV_DIM)),
            full((1, SSD_CONV_DIM)),
            full((1, LANES)),
            full((1, LANES)),
            full((1, SSD_INNER)),
            full((1, SSD_INNER)),
            full((LANES, SSD_INNER)),
            pl.BlockSpec((1, SAMPLE_ROWS, SSD_CONV_DIM), lambda i: (i, 0, 0)),
            state_spec,
        ] + extra_specs,
        out_specs=[pl.BlockSpec((SAMPLE_ROWS, SSD_INNER), lambda i: (i, 0)), state_spec],
        out_shape=[
            jax.ShapeDtypeStruct((t, SSD_INNER), F32),
            jax.ShapeDtypeStruct(state_all.shape, F32),
        ],
        input_output_aliases=aliases,
        compiler_params=_params(("parallel",)),
        name="ssd_sample",
    )(proj, proj, proj, proj, proj, small, conv_w, conv_b, dt_bias, a_log, d_x, gn, e, conv_prev, state_all,
      *extra_args)


def _narrow_w_in_t(w_in_t, layer):
    k = w_in_t.shape[2]
    rows = lambda a, b: lax.slice(w_in_t, (layer, a, 0), (layer + 1, b, k)).reshape(b - a, k)
    lr = rows(NATIVE_UV - GLA_LOWRANK, NATIVE_UV)
    dt = rows(NATIVE_DT, NATIVE_GATE)
    pad = lambda n: jnp.zeros((n, k), w_in_t.dtype)
    return jnp.concatenate([lr, pad(LANES - GLA_LOWRANK), dt, pad(LANES - SSD_HEADS)], axis=0)


def _pad_lanes(v, n):
    return jnp.pad(v, (0, n - v.shape[0])).reshape(1, n)


def _layer_weights(i, w_narrow_next, g_next, g_mix, w_in, w_gla_lr, b_gla_lr, g_gla_norm, cm_ln_g, cm_ln_b,
                   cm_ws, cm_bs, ssd_conv_w, ssd_conv_b, ssd_dt_bias, ssd_a_log, ssd_d, g_ssd_norm,
                   w_br_gla, w_br_cm, w_br_ssd, w_o, g_ffn, w_ffn_in, w_ffn_out,
                   g_ple, w_ple_gate, w_ple_proj):
    row = lambda v: v[i].reshape(1, -1)
    n_tile = CM_CHUNK // SAMPLE_LEN
    return dict(
        w_narrow_next=w_narrow_next,
        layer=i,
        g_next=g_next.reshape(1, -1),
        w_lr=jnp.pad(w_gla_lr[i], ((0, LANES - GLA_LOWRANK), (0, 0))).astype(BF16),
        b_lr=row(b_gla_lr),
        g_gla=row(g_gla_norm),
        ln_g=row(cm_ln_g),
        ln_b=row(cm_ln_b),
        ws_prompt=cm_ws[i],
        bs_prompt=cm_bs[i].T,
        ws_sample=jnp.tile(cm_ws[i][:, :SAMPLE_LEN, :SAMPLE_LEN], (1, n_tile, n_tile)),
        bs_sample=jnp.tile(cm_bs[i][:, :SAMPLE_LEN].T, (n_tile, 1)),
        conv_w=ssd_conv_w[i],
        conv_b=row(ssd_conv_b),
        dt_bias=_pad_lanes(ssd_dt_bias[i], LANES),
        a_log=_pad_lanes(ssd_a_log[i], LANES),
        d_x=jnp.repeat(ssd_d[i], SSD_HEADDIM).reshape(1, SSD_INNER),
        g_ssd=row(g_ssd_norm),
        w_br_gla=w_br_gla[i].astype(BF16),
        w_br_cm=w_br_cm[i].astype(BF16),
        w_br_ssd=w_br_ssd[i].astype(BF16),
        w_o=w_o[i].astype(BF16),
        g_ffn=row(g_ffn),
        w_ffn_in=w_ffn_in,
        w_ffn_out=w_ffn_out,
        g_ple=row(g_ple),
        w_ple_gate=w_ple_gate[i].astype(BF16),
        w_ple_proj=w_ple_proj[i].astype(BF16),
    )


def _merge_out(x, proj, ya, yb, yc, w):
    tm = 512 if ya.dtype == BF16 else 256
    mix = _merge(ya, yb, yc, proj, w["w_br_gla"], w["w_br_cm"], w["w_br_ssd"], tm, 1024)
    return _out_proj(mix, w["w_o"], x, w["g_ffn"], 512)


def _ffn_ple(x, h, p_all, w):
    x = _matmul_residual(h, w["w_ffn_out"], w["layer"], x, 512, 512)
    return _ple(x, w["g_ple"], p_all, w["layer"], w["w_ple_gate"], w["w_ple_proj"], w["g_next"],
                w["w_narrow_next"], 512)


def _last_conv_rows(proj, bsz, length):
    keep = SSD_CONV - 1
    if length % SUBLANES == 0:
        rows = proj.reshape(bsz, length, proj.shape[-1])[:, length - keep:]
        return rows[:, :, C_XBC:C_XBC + SSD_CONV_DIM]
    xbc = proj[:, C_XBC:C_XBC + SSD_CONV_DIM]
    return xbc.reshape(bsz, length, SSD_CONV_DIM)[:, length - keep:]


def _prompt_mixers(proj, small, bsz, length, w):
    ya, s_gla = _gla_prompt(proj, small, bsz, length, w["w_lr"], w["b_lr"], w["g_gla"])
    yb, _ = _cmlp(proj, w["ln_g"], w["ln_b"], w["ws_prompt"], w["bs_prompt"], CM_CHUNK, False, BF16)
    yc, s_ssm = _ssd_prompt(proj, small, bsz, length, w["conv_w"], w["conv_b"], w["dt_bias"], w["a_log"],
                            w["d_x"], w["g_ssd"])
    s_conv = _last_conv_rows(proj, bsz, length)
    s_ssm = s_ssm.reshape(bsz, SSD_HEADS, SSD_HEADDIM, SSD_STATE)
    return (ya, yb, yc), s_gla, s_ssm, s_conv


def _sample_mixers(proj, small, bsz, gla_all, ssm_all, s_conv, prev_gla, prev_ssm, w):
    n = SAMPLE_LEN
    ya, gla_out = _gla_sample(proj, small, gla_all, w["layer"], prev_gla, w["w_lr"], w["b_lr"], w["g_gla"])
    yb, v_rows = _cmlp(proj, w["ln_g"], w["ln_b"], w["ws_sample"], w["bs_sample"], n, True, F32)
    sc = s_conv.reshape(bsz // 2, 2, SSD_CONV - 1, SSD_CONV_DIM)
    zrow = jnp.zeros((bsz // 2, 1, SSD_CONV_DIM), F32)
    conv_prev = jnp.concatenate([zrow, sc[:, 1], zrow, sc[:, 0]], axis=1)
    yc, ssm_out = _ssd_sample(proj, small, conv_prev, ssm_all, w["layer"], prev_ssm, w["conv_w"], w["conv_b"],
                              w["dt_bias"], w["a_log"], w["d_x"], w["g_ssd"])
    s_conv_new = _last_conv_rows(proj, bsz, n)
    return (ya, yb, yc), gla_out, ssm_out, s_conv_new, v_rows.reshape(bsz, n, CM_WIDTH)


def kernel(x_prompt, x_sample, state_gla, state_ssm, state_conv, p_prompt, p_sample, g_mix, w_in, w_gla_lr, b_gla_lr, g_gla_norm, cm_ln_g, cm_ln_b, cm_ws, cm_bs, ssd_conv_w, ssd_conv_b, ssd_dt_bias, ssd_a_log, ssd_d, g_ssd_norm, w_br_gla, w_br_cm, w_br_ssd, w_o, g_ffn, w_ffn_in, w_ffn_out, g_ple, w_ple_gate, w_ple_proj, g_final):
    weights = (g_mix, w_in, w_gla_lr, b_gla_lr, g_gla_norm, cm_ln_g, cm_ln_b, cm_ws, cm_bs,
               ssd_conv_w, ssd_conv_b, ssd_dt_bias, ssd_a_log, ssd_d, g_ssd_norm,
               w_br_gla, w_br_cm, w_br_ssd, w_o, g_ffn, w_ffn_in, w_ffn_out,
               g_ple, w_ple_gate, w_ple_proj)
    bp, lp, _ = x_prompt.shape
    bs, ls, _ = x_sample.shape
    assert ls == SAMPLE_LEN and lp % CHUNK == 0 and bs % 2 == 0
    xp = x_prompt.reshape(bp * lp, D_MODEL)
    xs = x_sample.reshape(bs * ls, D_MODEL)
    w_in_t = jnp.swapaxes(w_in, 1, 2)
    narrow = [_narrow_w_in_t(w_in_t, i) for i in range(DEPTH)] + [None]
    g_first = g_mix[0].reshape(1, D_MODEL)
    np_, small_p = _rms_cast(xp, g_first, narrow[0], 512)
    ns, small_s = _rms_cast(xs, g_first, narrow[0], 512)
    pp_all = p_prompt.reshape(DEPTH, bp * lp, PLE_DIM)
    ps_all = p_sample.reshape(DEPTH, bs * ls, PLE_DIM)
    gla_p, ssm_p, conv_p = [], [], []
    conv_s, v_s = [], []
    ssm_all = state_ssm.reshape(DEPTH, bs, SSD_GROUPS, SSD_GROUP_WIDTH, SSD_STATE)
    gla_s = ssm_s = None
    for i in range(DEPTH):
        final = i == DEPTH - 1
        g_next = g_final if final else g_mix[i + 1]
        w = _layer_weights(i, narrow[i + 1], g_next, *weights)
        proj_p = _proj_matmul(np_, w_in_t, i, 1024)
        proj_s = _proj_matmul(ns, w_in_t, i, 512)
        ys_p, sg, sm, sc = _prompt_mixers(proj_p, small_p, bp, lp, w)
        gla_p.append(sg)
        ssm_p.append(sm)
        conv_p.append(sc)
        ys_s, gla_s, ssm_s, sc, vr = _sample_mixers(proj_s, small_s, bs, state_gla, ssm_all, state_conv[i],
                                                    gla_s, ssm_s, w)
        conv_s.append(sc)
        v_s.append(vr)
        xp, n_ffn_p = _merge_out(xp, proj_p, *ys_p, w)
        xs, n_ffn_s = _merge_out(xs, proj_s, *ys_s, w)
        h_p = _ffn_in(n_ffn_p, w_ffn_in, i, 1024, 512)
        h_s = _ffn_in(n_ffn_s, w_ffn_in, i, 512, 512)
        outs = _ffn_ple(xp, h_p, pp_all, w)
        xp, np_, small_p = (outs[0], None, None) if final else outs
        outs = _ffn_ple(xs, h_s, ps_all, w)
        xs, ns, small_s = (outs[0], None, None) if final else outs
    y_prompt = xp.reshape(bp, lp, D_MODEL)
    y_sample = xs.reshape(bs, ls, D_MODEL)
    ssm_s = ssm_s.reshape(DEPTH, bs, SSD_HEADS, SSD_HEADDIM, SSD_STATE)
    return (y_prompt, y_sample, jnp.stack(gla_p), gla_s, jnp.stack(ssm_p), ssm_s,
            jnp.stack(conv_p), jnp.stack(conv_s), jnp.stack(v_s))
```

```python
import functools

import numpy as np
import jax
import jax.numpy as jnp
from jax import lax
from jax.experimental import pallas as pl
from jax.experimental.pallas import tpu as pltpu

F32 = jnp.float32
BF16 = jnp.bfloat16

D_MODEL = 2048
DEPTH = 2
GLA_HEADS = 4
GLA_DK = 256
GLA_DV = 512
GLA_QK = 1024
GLA_V = 2048
GLA_LOWRANK = 16
GLA_TAU = 16.0
CM_GROUPS = 8
CM_WIDTH = 2048
CM_GROUP_DIM = 256
CM_CHUNK = 128
SSD_INNER = 4096
SSD_HEADDIM = 64
SSD_HEADS = 64
SSD_GROUPS = 8
SSD_STATE = 128
SSD_CONV = 4
SSD_CONV_DIM = 6144
SSD_GROUP_WIDTH = SSD_INNER // SSD_GROUPS
D_FF = 5632
PLE_DIM = 256
EPS = 1e-6

LANES = 128
SUBLANES = 8
VMEM_LIMIT = 56 * 1024 * 1024

C_Q = 0
C_K = 1024
C_V = 2048
C_G = 4096
C_UV = 6144
C_Z = 10240
C_XBC = 14336
C_GATE = 20480
C_LR = 26624
SMALL_LR = 0
SMALL_DT = 1

CHUNK = 128
SAMPLE_ROWS = 8
SAMPLE_LEN = 4
SAMPLE_SHIFT = 2
HEADDIM_SHIFT = 6


def _params(sem):
    return pltpu.CompilerParams(dimension_semantics=sem, vmem_limit_bytes=VMEM_LIMIT)


def _nt(a, b):
    return lax.dot_general(a, b, (((1,), (1,)), ((), ())), preferred_element_type=F32)


def _tn(a, b):
    return lax.dot_general(a, b, (((0,), (0,)), ((), ())), preferred_element_type=F32)


def _dot(a, b):
    return jnp.dot(a, b, preferred_element_type=F32)


def _split2(x):
    hi = x.astype(BF16)
    lo = (x - hi.astype(F32)).astype(BF16)
    return hi, lo


def _softplus(x):
    return jnp.maximum(x, 0.0) + jnp.log(1.0 + jnp.exp(-jnp.abs(x)))


def _log_sigmoid(x):
    return jnp.minimum(x, 0.0) - jnp.log(1.0 + jnp.exp(-jnp.abs(x)))


def _silu(x):
    return x * jax.nn.sigmoid(x)


def _rms_cast_body(x_ref, g_ref, wn_ref, o_ref, small_ref):
    x = x_ref[...]
    ms = jnp.mean(x * x, axis=-1, keepdims=True)
    n = (x * lax.rsqrt(ms + EPS) * g_ref[...]).astype(BF16)
    o_ref[...] = n
    small_ref[...] = _nt(n, wn_ref[...].astype(BF16))


def _rms_cast(x, g, w_narrow, tm):
    t, k = x.shape
    nn = w_narrow.shape[0]
    return pl.pallas_call(
        _rms_cast_body,
        grid=(t // tm,),
        in_specs=[pl.BlockSpec((tm, k), lambda i: (i, 0)), pl.BlockSpec((1, k), lambda i: (0, 0)),
                  pl.BlockSpec((nn, k), lambda i: (0, 0))],
        out_specs=[pl.BlockSpec((tm, k), lambda i: (i, 0)), pl.BlockSpec((tm, nn), lambda i: (i, 0))],
        out_shape=[jax.ShapeDtypeStruct((t, k), BF16), jax.ShapeDtypeStruct((t, nn), F32)],
        compiler_params=_params(("parallel",)),
        name="rms_cast",
    )(x, g, w_narrow)


NATIVE_UV = 2 * GLA_QK + 2 * GLA_V + GLA_LOWRANK
NATIVE_DT = NATIVE_UV + 2 * CM_WIDTH + SSD_INNER + SSD_CONV_DIM
NATIVE_GATE = NATIVE_DT + SSD_HEADS
PROJ_TN = 1024
W_ROW_CHUNK = 256


def _proj_body(n_ref, w_ref, wn_ref, o_ref, w_scr):
    j = pl.program_id(0)
    tn = w_ref.shape[0]

    def load_weights(shift):
        for r in range(0, tn - shift, W_ROW_CHUNK):
            rows = min(W_ROW_CHUNK, tn - shift - r)
            w_scr[r:r + rows, :] = w_ref[r + shift:r + shift + rows, :].astype(BF16)
        if shift:
            w_scr[tn - shift:tn, :] = wn_ref[0:shift, :].astype(BF16)

    @pl.when(pl.program_id(1) == 0)
    def _():
        j_uv = C_UV // PROJ_TN
        j_gate = C_GATE // PROJ_TN
        pl.when(j < j_uv)(lambda: load_weights(0))
        pl.when((j >= j_uv) & (j < j_gate))(lambda: load_weights(NATIVE_UV - C_UV))
        pl.when(j >= j_gate)(lambda: load_weights(NATIVE_GATE - C_GATE))

    o_ref[...] = _nt(n_ref[...], w_scr[...])


def _proj_matmul(n, w_in_t, layer, tm):
    t, k = n.shape
    tn = PROJ_TN
    return pl.pallas_call(
        _proj_body,
        grid=(C_LR // tn, t // tm),
        in_specs=[
            pl.BlockSpec((tm, k), lambda j, i: (i, 0)),
            pl.BlockSpec((None, tn, k), lambda j, i: (layer, j, 0)),
            pl.BlockSpec((None, LANES, k), lambda j, i: (layer, (j + 1) * (tn // LANES), 0)),
        ],
        out_specs=pl.BlockSpec((tm, tn), lambda j, i: (i, j)),
        out_shape=jax.ShapeDtypeStruct((t, C_LR), F32),
        scratch_shapes=[pltpu.VMEM((tn, k), BF16)],
        compiler_params=_params(("parallel", "arbitrary")),
        name="proj_matmul",
    )(n, w_in_t, w_in_t)


def _ffn_in_body(n_ref, wg_ref, wu_ref, o_ref, wg_scr, wu_scr):
    @pl.when(pl.program_id(1) == 0)
    def _():
        for r in range(0, wg_ref.shape[0], W_ROW_CHUNK):
            rs = slice(r, r + W_ROW_CHUNK)
            wg_scr[rs, :] = wg_ref[rs, :].astype(BF16)
            wu_scr[rs, :] = wu_ref[rs, :].astype(BF16)

    n = n_ref[...]
    gate = _dot(n, wg_scr[...])
    up = _dot(n, wu_scr[...])
    o_ref[...] = (_silu(gate) * up).astype(o_ref.dtype)


def _ffn_in(n, w, layer, tm, tn):
    t, k = n.shape
    nblk = D_FF // tn
    return pl.pallas_call(
        _ffn_in_body,
        grid=(nblk, t // tm),
        in_specs=[
            pl.BlockSpec((tm, k), lambda j, i: (i, 0)),
            pl.BlockSpec((None, k, tn), lambda j, i: (layer, 0, j)),
            pl.BlockSpec((None, k, tn), lambda j, i: (layer, 0, j + nblk)),
        ],
        out_specs=pl.BlockSpec((tm, tn), lambda j, i: (i, j)),
        out_shape=jax.ShapeDtypeStruct((t, D_FF), BF16),
        scratch_shapes=[pltpu.VMEM((k, tn), BF16), pltpu.VMEM((k, tn), BF16)],
        compiler_params=_params(("parallel", "arbitrary")),
        name="ffn_in",
    )(n, w, w)


def _mm_res_body(a_ref, w_ref, r_ref, o_ref, w_scr):
    @pl.when(pl.program_id(1) == 0)
    def _():
        for r in range(0, w_ref.shape[0], W_ROW_CHUNK):
            rs = slice(r, r + W_ROW_CHUNK)
            w_scr[rs, :] = w_ref[rs, :].astype(BF16)

    o_ref[...] = r_ref[...] + _dot(a_ref[...], w_scr[...])


def _matmul_residual(a, w, layer, res, tm, tn):
    t, k = a.shape
    n = w.shape[2]
    return pl.pallas_call(
        _mm_res_body,
        grid=(n // tn, t // tm),
        in_specs=[
            pl.BlockSpec((tm, k), lambda j, i: (i, 0)),
            pl.BlockSpec((None, k, tn), lambda j, i: (layer, 0, j)),
            pl.BlockSpec((tm, tn), lambda j, i: (i, j)),
        ],
        out_specs=pl.BlockSpec((tm, tn), lambda j, i: (i, j)),
        out_shape=jax.ShapeDtypeStruct((t, n), F32),
        scratch_shapes=[pltpu.VMEM((k, tn), BF16)],
        compiler_params=_params(("parallel", "arbitrary")),
        name="matmul_residual",
    )(a, w, res)


def _merge_body(ya_ref, yb_ref, yc_ref, ga_ref, gb_ref, gc_ref, wa_ref, wb_ref, wc_ref, o_ref):
    acc = jax.nn.sigmoid(ga_ref[...]) * _dot(ya_ref[...].astype(BF16), wa_ref[...])
    acc += jax.nn.sigmoid(gb_ref[...]) * _dot(yb_ref[...].astype(BF16), wb_ref[...])
    acc += jax.nn.sigmoid(gc_ref[...]) * _dot(yc_ref[...].astype(BF16), wc_ref[...])
    o_ref[...] = acc.astype(o_ref.dtype)


def _resident(shape, index_map):
    return pl.BlockSpec(shape, index_map, pipeline_mode=pl.Buffered(1))


def _merge(ya, yb, yc, proj, wa, wb, wc, tm, tn):
    t = ya.shape[0]
    nblk = D_MODEL // tn
    g0 = C_GATE // tn
    return pl.pallas_call(
        _merge_body,
        grid=(nblk, t // tm),
        in_specs=[
            pl.BlockSpec((tm, GLA_V), lambda j, i: (i, 0)),
            pl.BlockSpec((tm, CM_WIDTH), lambda j, i: (i, 0)),
            pl.BlockSpec((tm, SSD_INNER), lambda j, i: (i, 0)),
            pl.BlockSpec((tm, tn), lambda j, i: (i, g0 + j)),
            pl.BlockSpec((tm, tn), lambda j, i: (i, g0 + nblk + j)),
            pl.BlockSpec((tm, tn), lambda j, i: (i, g0 + 2 * nblk + j)),
            _resident((GLA_V, tn), lambda j, i: (0, j)),
            _resident((CM_WIDTH, tn), lambda j, i: (0, j)),
            _resident((SSD_INNER, tn), lambda j, i: (0, j)),
        ],
        out_specs=pl.BlockSpec((tm, tn), lambda j, i: (i, j)),
        out_shape=jax.ShapeDtypeStruct((t, D_MODEL), BF16),
        compiler_params=_params(("parallel", "arbitrary")),
        name="merge",
    )(ya, yb, yc, proj, proj, proj, wa, wb, wc)


def _rms(x, g):
    ms = jnp.mean(x * x, axis=-1, keepdims=True)
    return x * lax.rsqrt(ms + EPS) * g


def _out_proj_body(mix_ref, w_ref, x_ref, g_ref, x1_ref, n_ref):
    x1 = x_ref[...] + _dot(mix_ref[...], w_ref[...])
    x1_ref[...] = x1
    n_ref[...] = _rms(x1, g_ref[...]).astype(n_ref.dtype)


def _out_proj(mix, w, x, g_next, tm):
    t, k = mix.shape
    row_block = lambda width: pl.BlockSpec((tm, width), lambda i: (i, 0))
    return pl.pallas_call(
        _out_proj_body,
        grid=(t // tm,),
        in_specs=[
            row_block(k),
            _resident((k, D_MODEL), lambda i: (0, 0)),
            row_block(D_MODEL),
            pl.BlockSpec((1, D_MODEL), lambda i: (0, 0)),
        ],
        out_specs=[row_block(D_MODEL), row_block(D_MODEL)],
        out_shape=[jax.ShapeDtypeStruct((t, D_MODEL), F32), jax.ShapeDtypeStruct((t, D_MODEL), BF16)],
        compiler_params=_params(("parallel",)),
        name="out_proj",
    )(mix, w, x, g_next)


def _ple_body(x_ref, g_ref, p_ref, wg_ref, wp_ref, gn_ref, *rest, final):
    x = x_ref[...]
    n = _rms(x, g_ref[...]).astype(BF16)
    gate = jax.nn.sigmoid(_dot(n, wg_ref[...]))
    emb = _dot(p_ref[...].astype(BF16), wp_ref[...])
    y = x + gate * emb
    if final:
        rest[0][...] = _rms(y, gn_ref[...])
    else:
        wn_ref, y_ref, n_ref, small_ref = rest
        y_ref[...] = y
        n_next = _rms(y, gn_ref[...]).astype(BF16)
        n_ref[...] = n_next
        small_ref[...] = _nt(n_next, wn_ref[...].astype(BF16))


def _ple(x, g, p_all, layer, wg, wp, g_next, w_narrow_next, tm):
    t = x.shape[0]
    final = w_narrow_next is None
    row_block = lambda width: pl.BlockSpec((tm, width), lambda i: (i, 0))
    whole = lambda a: _resident(a.shape, lambda i: (0, 0))
    in_specs = [row_block(D_MODEL), whole(g), pl.BlockSpec((None, tm, PLE_DIM), lambda i: (layer, i, 0)),
                whole(wg), whole(wp), whole(g_next)]
    args = [x, g, p_all, wg, wp, g_next]
    if final:
        out_specs = [row_block(D_MODEL)]
        out_shape = [jax.ShapeDtypeStruct((t, D_MODEL), F32)]
    else:
        nn = w_narrow_next.shape[0]
        in_specs.append(whole(w_narrow_next))
        args.append(w_narrow_next)
        out_specs = [row_block(D_MODEL), row_block(D_MODEL), row_block(nn)]
        out_shape = [jax.ShapeDtypeStruct((t, D_MODEL), F32), jax.ShapeDtypeStruct((t, D_MODEL), BF16),
                     jax.ShapeDtypeStruct((t, nn), F32)]
    return pl.pallas_call(
        functools.partial(_ple_body, final=final),
        grid=(t // tm,),
        in_specs=in_specs,
        out_specs=out_specs,
        out_shape=out_shape,
        compiler_params=_params(("parallel",)),
        name="ple",
    )(*args)


CM_STEP_CHUNKS = 2


def _cmlp_body(u_ref, v_ref, lng_ref, lnb_ref, ws_ref, bst_ref, yb_ref, *rest, seq_len):
    r = lax.broadcasted_iota(jnp.int32, (CM_CHUNK, CM_CHUNK), 0)
    c = lax.broadcasted_iota(jnp.int32, (CM_CHUNK, CM_CHUNK), 1)
    if seq_len >= CM_CHUNK:
        keep = r >= c
    else:
        sh = seq_len.bit_length() - 1
        keep = ((r >> sh) == (c >> sh)) & ((r & (seq_len - 1)) >= (c & (seq_len - 1)))
    ws = [jnp.where(keep, ws_ref[g], 0.0).astype(BF16) for g in range(CM_GROUPS)]
    for k in range(CM_STEP_CHUNKS):
        rows = slice(k * CM_CHUNK, (k + 1) * CM_CHUNK)
        u = jax.nn.gelu(u_ref[rows, :])
        v = jax.nn.gelu(v_ref[rows, :])
        mu = jnp.mean(v, axis=-1, keepdims=True)
        vc = v - mu
        var = jnp.mean(vc * vc, axis=-1, keepdims=True)
        vn = vc * lax.rsqrt(var + EPS) * lng_ref[...] + lnb_ref[...]
        if rest:
            rest[0][rows, :] = vn
        for g in range(CM_GROUPS):
            sl = slice(g * CM_GROUP_DIM, (g + 1) * CM_GROUP_DIM)
            mixed = _dot(ws[g], vn[:, sl].astype(BF16)) + bst_ref[:, g:g + 1]
            yb_ref[rows, sl] = (u[:, sl] * mixed).astype(yb_ref.dtype)


def _cmlp(proj, ln_g, ln_b, ws_tiled, bs_t, seq_len, emit_v, out_dtype):
    t = proj.shape[0]
    step_rows = CM_STEP_CHUNKS * CM_CHUNK
    out_shape = [jax.ShapeDtypeStruct((t, CM_WIDTH), out_dtype)]
    out_specs = [pl.BlockSpec((step_rows, CM_WIDTH), lambda i: (i, 0))]
    if emit_v:
        out_shape.append(jax.ShapeDtypeStruct((t, CM_WIDTH), F32))
        out_specs.append(pl.BlockSpec((step_rows, CM_WIDTH), lambda i: (i, 0)))
    cu = C_UV // CM_WIDTH
    res = pl.pallas_call(
        functools.partial(_cmlp_body, seq_len=seq_len),
        grid=(t // step_rows,),
        in_specs=[
            pl.BlockSpec((step_rows, CM_WIDTH), lambda i: (i, cu)),
            pl.BlockSpec((step_rows, CM_WIDTH), lambda i: (i, cu + 1)),
            pl.BlockSpec((1, CM_WIDTH), lambda i: (0, 0)),
            pl.BlockSpec((1, CM_WIDTH), lambda i: (0, 0)),
            pl.BlockSpec((CM_GROUPS, CM_CHUNK, CM_CHUNK), lambda i: (0, 0, 0)),
            pl.BlockSpec((CM_CHUNK, CM_GROUPS), lambda i: (0, 0)),
        ],
        out_specs=out_specs,
        out_shape=out_shape,
        compiler_params=_params(("parallel",)),
        name="chunk_mlp",
    )(proj, proj, ln_g, ln_b, ws_tiled, bs_t)
    return (res[0], res[1]) if emit_v else (res[0], None)


def _gla_level_constants(n):
    levels = n.bit_length() - 1
    t = np.arange(n)[:, None]
    j = np.arange(n)[None, :]
    sums = [j <= t, j > t]
    masks = [t == j]
    for l in range(1, levels + 1):
        w, half = 1 << l, 1 << (l - 1)
        start = (t >> l) << l
        upper = ((t >> (l - 1)) & 1) == 1
        a_up = (j >= start + half) & (j <= t)
        a_lo = (j > t) & (j < start + half)
        sums.append(np.where(upper, a_up, a_lo))
        jj = j
        upper_t = ((t >> (l - 1)) & 1) == 1
        lower_s = ((jj >> (l - 1)) & 1) == 0
        masks.append(((t >> l) == (jj >> l)) & upper_t & lower_s)
    return (jnp.asarray(np.stack(sums), BF16), jnp.asarray(np.stack(masks), F32), levels)


def _gla_log_decay(lr_ref, wlr_ref, blr_ref):
    x = _dot(lr_ref[...].astype(BF16), wlr_ref[...]) + blr_ref[...]
    return _log_sigmoid(x) * (1.0 / GLA_TAU)


def _gla_finish(o, g, gn):
    ms = jnp.mean(o * o, axis=-1, keepdims=True)
    return o * lax.rsqrt(ms + EPS) * gn * _silu(g)


def _gla_prompt_body(q_ref, k_ref, v_ref, g_ref, lr_ref, wlr_ref, blr_ref, gn_ref, a_ref, m_ref,
                     ya_ref, s_ref, *, levels):
    rows = q_ref.shape[0]

    @pl.when(pl.program_id(1) == 0)
    def _():
        s_ref[...] = jnp.zeros(s_ref.shape, F32)

    q = q_ref[...] * (GLA_DK ** -0.5)
    k = k_ref[...]
    la = _gla_log_decay(lr_ref, wlr_ref, blr_ref)
    la2 = jnp.concatenate(_split2(la), axis=1)

    def decay_sum(i):
        d = _dot(a_ref[i], la2)
        return d[:, :GLA_QK] + d[:, GLA_QK:]

    b = decay_sum(0)
    qe = (q * jnp.exp(b)).astype(BF16)
    kd = (k * jnp.exp(decay_sum(1))).astype(BF16)
    e_end_t = jnp.exp(jnp.broadcast_to(b[rows - 1:rows, :], (SUBLANES, GLA_QK))).T
    row = lax.broadcasted_iota(jnp.int32, (rows, GLA_QK), 0)
    xs = []
    for l in range(1, levels + 1):
        e = jnp.exp(decay_sum(l + 1))
        upper = ((row >> (l - 1)) & 1) == 1
        xs.append((jnp.where(upper, q, k) * e).astype(BF16))
    qb = q.astype(BF16)
    kb = k.astype(BF16)
    for h in range(GLA_HEADS):
        ks = slice(h * GLA_DK, (h + 1) * GLA_DK)
        vs = slice(h * GLA_DV, (h + 1) * GLA_DV)
        att = m_ref[0] * _nt(qb[:, ks], kb[:, ks])
        for l in range(1, levels + 1):
            x = xs[l - 1][:, ks]
            att += m_ref[l] * _nt(x, x)
        vh = v_ref[:, vs].astype(BF16)
        s = s_ref[0, h]
        o = _dot(qe[:, ks], s.astype(BF16)) + _dot(att.astype(BF16), vh)
        s_ref[0, h] = s * e_end_t[ks, 0:1] + _tn(kd[:, ks], vh)
        ya_ref[:, vs] = _gla_finish(o, g_ref[:, vs], gn_ref[:, vs]).astype(ya_ref.dtype)


def _gla_prompt(proj, small, bsz, length, w_lr, b_lr, gn):
    nchunk = length // CHUNK
    a_mats, masks, levels = _gla_level_constants(CHUNK)
    row = lambda b, c: b * nchunk + c
    return pl.pallas_call(
        functools.partial(_gla_prompt_body, levels=levels),
        grid=(bsz, nchunk),
        in_specs=[
            pl.BlockSpec((CHUNK, GLA_QK), lambda b, c: (row(b, c), C_Q // GLA_QK)),
            pl.BlockSpec((CHUNK, GLA_QK), lambda b, c: (row(b, c), C_K // GLA_QK)),
            pl.BlockSpec((CHUNK, GLA_V), lambda b, c: (row(b, c), C_V // GLA_V)),
            pl.BlockSpec((CHUNK, GLA_V), lambda b, c: (row(b, c), C_G // GLA_V)),
            pl.BlockSpec((CHUNK, LANES), lambda b, c: (row(b, c), SMALL_LR)),
            pl.BlockSpec((LANES, GLA_QK), lambda b, c: (0, 0)),
            pl.BlockSpec((1, GLA_QK), lambda b, c: (0, 0)),
            pl.BlockSpec((1, GLA_V), lambda b, c: (0, 0)),
            pl.BlockSpec(a_mats.shape, lambda b, c: (0, 0, 0)),
            pl.BlockSpec(masks.shape, lambda b, c: (0, 0, 0)),
        ],
        out_specs=[
            pl.BlockSpec((CHUNK, GLA_V), lambda b, c: (row(b, c), 0)),
            pl.BlockSpec((1, GLA_HEADS, GLA_DK, GLA_DV), lambda b, c: (b, 0, 0, 0)),
        ],
        out_shape=[
            jax.ShapeDtypeStruct((bsz * length, GLA_V), BF16),
            jax.ShapeDtypeStruct((bsz, GLA_HEADS, GLA_DK, GLA_DV), F32),
        ],
        compiler_params=_params(("parallel", "arbitrary")),
        name="gla_prompt",
    )(proj, proj, proj, proj, small, w_lr, b_lr, gn, a_mats, masks)


def _roll_rows(x, d):
    return pltpu.roll(x, d, 0) if d else x


def _gla_sample_body(q_ref, k_ref, v_ref, g_ref, lr_ref, wlr_ref, blr_ref, gn_ref, s0_ref,
                     ya_ref, s_ref):
    n = SAMPLE_LEN
    pos_k = lax.broadcasted_iota(jnp.int32, (SAMPLE_ROWS, GLA_QK), 0) & (n - 1)
    row_v = lax.broadcasted_iota(jnp.int32, (SAMPLE_ROWS, GLA_DV), 0)
    q = q_ref[...] * (GLA_DK ** -0.5)
    k = k_ref[...]
    v = v_ref[...]
    la = _gla_log_decay(lr_ref, wlr_ref, blr_ref)
    back = [_roll_rows(la, d) for d in range(n)]
    win = [None, la]
    for d in range(2, n):
        win.append(win[-1] + back[d - 1])
    b = la
    suffix = jnp.zeros_like(la)
    for d in range(1, n):
        b = b + jnp.where(pos_k >= d, back[d], 0.0)
        suffix = suffix + jnp.where(pos_k < n - d, pltpu.roll(la, SAMPLE_ROWS - d, 0), 0.0)
    eb = jnp.exp(b)
    qe = (q * eb).astype(BF16)
    kd = k * jnp.exp(suffix)
    eb_t = eb.T
    nseq = SAMPLE_ROWS // n
    seq_k = lax.broadcasted_iota(jnp.int32, (SAMPLE_ROWS, GLA_QK), 0) >> SAMPLE_SHIFT
    kd_seq = [jnp.where(seq_k == j, kd, 0.0).astype(BF16) for j in range(nseq)]
    pair = []
    for d in range(n):
        p = q * _roll_rows(k, d)
        if d:
            p = p * jnp.exp(win[d])
        pair.append(p)
    vback = [_roll_rows(v, d) for d in range(n)]
    for h in range(GLA_HEADS):
        ks = slice(h * GLA_DK, (h + 1) * GLA_DK)
        vs = slice(h * GLA_DV, (h + 1) * GLA_DV)
        o = jnp.zeros((SAMPLE_ROWS, GLA_DV), F32)
        for d in range(n):
            score = jnp.sum(pair[d][:, ks], axis=-1, keepdims=True)
            o = o + jnp.where((row_v & (n - 1)) >= d, score * vback[d][:, vs], 0.0)
        vh = v[:, vs].astype(BF16)
        for j in range(nseq):
            s = s0_ref[j, h]
            o = o + jnp.where((row_v >> SAMPLE_SHIFT) == j, _dot(qe[:, ks], s.astype(BF16)), 0.0)
            col = n * j + n - 1
            s_ref[j, h] = s * eb_t[ks, col:col + 1] + _tn(kd_seq[j][:, ks], vh)
        ya_ref[:, vs] = _gla_finish(o, g_ref[:, vs], gn_ref[:, vs]).astype(ya_ref.dtype)


def _stacked_state_call(body, prev_out, n_in):
    if prev_out is None:
        return body, [], [], {}
    wrapped = lambda *refs: body(*refs[:n_in], *refs[n_in + 1:])
    return wrapped, [pl.BlockSpec(memory_space=pl.ANY)], [prev_out], {n_in: 1}


def _gla_sample(proj, small, state_all, layer, prev_out, w_lr, b_lr, gn):
    t = proj.shape[0]
    nseq = SAMPLE_ROWS // SAMPLE_LEN
    state_spec = pl.BlockSpec((None, nseq, GLA_HEADS, GLA_DK, GLA_DV), lambda i: (layer, i, 0, 0, 0))
    body, extra_specs, extra_args, aliases = _stacked_state_call(_gla_sample_body, prev_out, 9)
    return pl.pallas_call(
        body,
        grid=(t // SAMPLE_ROWS,),
        in_specs=[
            pl.BlockSpec((SAMPLE_ROWS, GLA_QK), lambda i: (i, C_Q // GLA_QK)),
            pl.BlockSpec((SAMPLE_ROWS, GLA_QK), lambda i: (i, C_K // GLA_QK)),
            pl.BlockSpec((SAMPLE_ROWS, GLA_V), lambda i: (i, C_V // GLA_V)),
            pl.BlockSpec((SAMPLE_ROWS, GLA_V), lambda i: (i, C_G // GLA_V)),
            pl.BlockSpec((SAMPLE_ROWS, LANES), lambda i: (i, SMALL_LR)),
            pl.BlockSpec((LANES, GLA_QK), lambda i: (0, 0)),
            pl.BlockSpec((1, GLA_QK), lambda i: (0, 0)),
            pl.BlockSpec((1, GLA_V), lambda i: (0, 0)),
            state_spec,
        ] + extra_specs,
        out_specs=[pl.BlockSpec((SAMPLE_ROWS, GLA_V), lambda i: (i, 0)), state_spec],
        out_shape=[
            jax.ShapeDtypeStruct((t, GLA_V), F32),
            jax.ShapeDtypeStruct(state_all.shape, F32),
        ],
        input_output_aliases=aliases,
        compiler_params=_params(("parallel",)),
        name="gla_sample",
    )(proj, proj, proj, proj, small, w_lr, b_lr, gn, state_all, *extra_args)


XBC_PART = 2048


def _ssd_gate_norm(y, z, gn):
    yz = y * _silu(z)
    ms = jnp.mean(yz * yz, axis=-1, keepdims=True)
    return yz * lax.rsqrt(ms + EPS) * gn


def _ssd_prompt_body(x1_ref, x2_ref, bc_ref, z1_ref, z2_ref, dt_ref, cw_ref, cb_ref, dtb_ref, alog_ref,
                     dx_ref, gn_ref, e_ref, a_ref, hm_ref, yc_ref, h_ref, tail_scr, ht_scr):
    rows = x1_ref.shape[0]
    step = pl.program_id(1)

    @pl.when(step == 0)
    def _():
        tail_scr[...] = jnp.zeros(tail_scr.shape, F32)
        ht_scr[...] = jnp.zeros(ht_scr.shape, F32)

    row8 = lax.broadcasted_iota(jnp.int32, (SUBLANES, XBC_PART), 0)

    def conv(x_ref, part):
        cs = slice(part * XBC_PART, (part + 1) * XBC_PART)
        x = x_ref[...]
        prev = tail_scr[:, cs]
        acc = cb_ref[:, cs] + x * cw_ref[SSD_CONV - 1:SSD_CONV, cs]
        for d in range(1, SSD_CONV):
            xr = pltpu.roll(x, d, 0)
            first = jnp.where(row8 < d, pltpu.roll(prev, d, 0), xr[:SUBLANES])
            xd = jnp.concatenate([first, xr[SUBLANES:]], axis=0)
            acc = acc + xd * cw_ref[SSD_CONV - 1 - d:SSD_CONV - d, cs]
        tail_scr[:, cs] = x[rows - SUBLANES:, :]
        return _silu(acc)

    xs_halves = [conv(x1_ref, 0), conv(x2_ref, 1)]
    bc = conv(bc_ref, 2)
    half_w = SSD_GROUPS * SSD_STATE
    bmat = bc[:, :half_w]
    cmat = bc[:, half_w:]
    z_halves = [z1_ref, z2_ref]

    dt = _softplus(dt_ref[...] + dtb_ref[...])
    a = dt * (-jnp.exp(alog_ref[...]))
    a2 = jnp.concatenate(_split2(a), axis=1)

    def head_sum(i):
        d = _dot(a_ref[i], a2)
        return d[:, :LANES] + d[:, LANES:]

    cs_in = head_sum(0)
    cs_suf = head_sum(1)
    cs_t = cs_in.T
    cs_hi, cs_lo = _split2(cs_in)
    suf_hi, suf_lo = _split2(cs_suf)
    dt_hi, dt_lo = _split2(dt)
    r_i = lax.broadcasted_iota(jnp.int32, (rows, rows), 0)
    c_i = lax.broadcasted_iota(jnp.int32, (rows, rows), 1)
    causal = r_i >= c_i
    heads_per_group = SSD_HEADS // SSD_GROUPS
    per_half = SSD_GROUPS // 2
    for g in range(SSD_GROUPS):
        gs = slice(g * SSD_GROUP_WIDTH, (g + 1) * SSD_GROUP_WIDTH)
        ls = slice((g % per_half) * SSD_GROUP_WIDTH, (g % per_half + 1) * SSD_GROUP_WIDTH)
        ns = slice(g * SSD_STATE, (g + 1) * SSD_STATE)
        xg = xs_halves[g // per_half][:, ls]
        eg = e_ref[:, gs]
        csx = _dot(cs_hi, eg) + _dot(cs_lo, eg)
        sufx = _dot(suf_hi, eg) + _dot(suf_lo, eg)
        dtx = _dot(dt_hi, eg) + _dot(dt_lo, eg)
        bg = bmat[:, ns].astype(BF16)
        cg = cmat[:, ns].astype(BF16)
        cb = jnp.where(causal, _nt(cg, bg), 0.0)
        xdt = dtx * xg
        xdt_b = xdt.astype(BF16)
        ws = []
        xb = []
        for r in range(heads_per_group):
            h = g * heads_per_group + r
            dm = cs_in[:, h:h + 1] - cs_t[h:h + 1, :]
            ws.append((cb * jnp.exp(jnp.minimum(dm, 0.0))).astype(BF16))
            xb.append(xdt_b * hm_ref[r])
        y = _dot(jnp.concatenate(ws, axis=1), jnp.concatenate(xb, axis=0))
        ht = ht_scr[g]
        y = y + _dot(cg, ht.astype(BF16)) * jnp.exp(csx)
        y = y + dx_ref[:, gs] * xg
        wx = (jnp.exp(sufx) * xdt).astype(BF16)
        ht_scr[g] = ht * jnp.exp(csx[rows - 1:rows, :]) + _tn(bg, wx)
        z = z_halves[g // per_half][:, ls]
        yc_ref[:, gs] = _ssd_gate_norm(y, z, gn_ref[:, gs]).astype(yc_ref.dtype)

    @pl.when(step == pl.num_programs(1) - 1)
    def _():
        for g in range(SSD_GROUPS):
            h_ref[0, g] = ht_scr[g].T


def _tri_constants(n):
    t = np.arange(n)[:, None]
    j = np.arange(n)[None, :]
    return jnp.asarray(np.stack([j <= t, j > t]), BF16)


def _head_lane_masks(rows):
    lane = np.arange(SSD_GROUP_WIDTH)[None, None, :]
    r = np.arange(SSD_HEADS // SSD_GROUPS)[:, None, None]
    return jnp.asarray(np.broadcast_to(lane // SSD_HEADDIM == r, (SSD_HEADS // SSD_GROUPS, rows, SSD_GROUP_WIDTH)), BF16)


def _head_expand_matrix():
    h = np.arange(LANES)[:, None]
    lane = np.arange(SSD_INNER)[None, :]
    return jnp.asarray(h == lane // SSD_HEADDIM, BF16)


def _ssd_prompt(proj, small, bsz, length, conv_w, conv_b, dt_bias, a_log, d_x, gn):
    nchunk = length // CHUNK
    row = lambda b, c: b * nchunk + c
    cx = C_XBC // XBC_PART
    cz = C_Z // XBC_PART
    tri = _tri_constants(CHUNK)
    e = _head_expand_matrix()
    head_masks = _head_lane_masks(CHUNK)
    full = lambda shape: pl.BlockSpec(shape, lambda b, c: (0,) * len(shape))
    return pl.pallas_call(
        _ssd_prompt_body,
        grid=(bsz, nchunk),
        in_specs=[
            pl.BlockSpec((CHUNK, XBC_PART), lambda b, c: (row(b, c), cx)),
            pl.BlockSpec((CHUNK, XBC_PART), lambda b, c: (row(b, c), cx + 1)),
            pl.BlockSpec((CHUNK, XBC_PART), lambda b, c: (row(b, c), cx + 2)),
            pl.BlockSpec((CHUNK, XBC_PART), lambda b, c: (row(b, c), cz)),
            pl.BlockSpec((CHUNK, XBC_PART), lambda b, c: (row(b, c), cz + 1)),
            pl.BlockSpec((CHUNK, LANES), lambda b, c: (row(b, c), SMALL_DT)),
            full((SSD_CONV, SSD_CONV_DIM)),
            full((1, SSD_CONV_DIM)),
            full((1, LANES)),
            full((1, LANES)),
            full((1, SSD_INNER)),
            full((1, SSD_INNER)),
            full((LANES, SSD_INNER)),
            full(tri.shape),
            full(head_masks.shape),
        ],
        out_specs=[
            pl.BlockSpec((CHUNK, SSD_INNER), lambda b, c: (row(b, c), 0)),
            pl.BlockSpec((1, SSD_GROUPS, SSD_GROUP_WIDTH, SSD_STATE), lambda b, c: (b, 0, 0, 0)),
        ],
        out_shape=[
            jax.ShapeDtypeStruct((bsz * length, SSD_INNER), BF16),
            jax.ShapeDtypeStruct((bsz, SSD_GROUPS, SSD_GROUP_WIDTH, SSD_STATE), F32),
        ],
        scratch_shapes=[
            pltpu.VMEM((SUBLANES, SSD_CONV_DIM), F32),
            pltpu.VMEM((SSD_GROUPS, SSD_STATE, SSD_GROUP_WIDTH), F32),
        ],
        compiler_params=_params(("parallel", "arbitrary")),
        name="ssd_prompt",
    )(proj, proj, proj, proj, proj, small, conv_w, conv_b, dt_bias, a_log, d_x, gn, e, tri, head_masks)


def _ssd_sample_body(x1_ref, x2_ref, bc_ref, z1_ref, z2_ref, dt_ref, cw_ref, cb_ref, dtb_ref, alog_ref,
                     dx_ref, gn_ref, e_ref, prev_ref, h0_ref, yc_ref, h_ref):
    n = SAMPLE_LEN
    nseq = SAMPLE_ROWS // n
    pos_p = lax.broadcasted_iota(jnp.int32, (SAMPLE_ROWS, XBC_PART), 0) & (n - 1)

    def conv(x_ref, part):
        cs = slice(part * XBC_PART, (part + 1) * XBC_PART)
        x = x_ref[...]
        prev = prev_ref[0][:, cs]
        acc = cb_ref[:, cs] + x * cw_ref[SSD_CONV - 1:SSD_CONV, cs]
        for d in range(1, SSD_CONV):
            xd = jnp.where(pos_p < d, pltpu.roll(prev, d, 0), pltpu.roll(x, d, 0))
            acc = acc + xd * cw_ref[SSD_CONV - 1 - d:SSD_CONV - d, cs]
        return _silu(acc)

    xs = jnp.concatenate([conv(x1_ref, 0), conv(x2_ref, 1)], axis=1)
    bc = conv(bc_ref, 2)
    half_w = SSD_GROUPS * SSD_STATE
    bmat = bc[:, :half_w]
    cmat = bc[:, half_w:]
    z = jnp.concatenate([z1_ref[...], z2_ref[...]], axis=1)

    dt = _softplus(dt_ref[...] + dtb_ref[...])
    dt_hi, dt_lo = _split2(dt)
    dtx = _dot(dt_hi, e_ref[...]) + _dot(dt_lo, e_ref[...])
    a_hi, a_lo = _split2(-jnp.exp(alog_ref[...]))
    anegx = _dot(jnp.broadcast_to(a_hi, (SUBLANES, LANES)), e_ref[...]) + \
        _dot(jnp.broadcast_to(a_lo, (SUBLANES, LANES)), e_ref[...])
    ax = dtx * anegx
    pos = lax.broadcasted_iota(jnp.int32, (SAMPLE_ROWS, SSD_INNER), 0) & (n - 1)
    seq_g = lax.broadcasted_iota(jnp.int32, (SAMPLE_ROWS, SSD_GROUP_WIDTH), 0) >> SAMPLE_SHIFT
    back = [_roll_rows(ax, d) for d in range(n)]
    win = [None, ax]
    for d in range(2, n):
        win.append(win[-1] + back[d - 1])
    csx = ax
    sufx = jnp.zeros_like(ax)
    for d in range(1, n):
        csx = csx + jnp.where(pos >= d, back[d], 0.0)
        sufx = sufx + jnp.where(pos < n - d, pltpu.roll(ax, SAMPLE_ROWS - d, 0), 0.0)
    ecs = jnp.exp(csx)
    ecs_t = ecs.T
    wx = jnp.exp(sufx) * dtx * xs

    y = dx_ref[...] * xs
    for d in range(n):
        prod = cmat * _roll_rows(bmat, d)
        cbx = jnp.concatenate(
            [jnp.broadcast_to(jnp.sum(prod[:, g * SSD_STATE:(g + 1) * SSD_STATE], axis=-1, keepdims=True),
                              (SAMPLE_ROWS, SSD_GROUP_WIDTH)) for g in range(SSD_GROUPS)], axis=1)
        term = cbx * _roll_rows(dtx, d) * _roll_rows(xs, d)
        if d:
            term = term * jnp.exp(win[d])
        y = y + jnp.where(pos >= d, term, 0.0)

    y_inter = []
    seq_n = lax.broadcasted_iota(jnp.int32, (SAMPLE_ROWS, SSD_STATE), 0) >> SAMPLE_SHIFT
    for g in range(SSD_GROUPS):
        gs = slice(g * SSD_GROUP_WIDTH, (g + 1) * SSD_GROUP_WIDTH)
        ns = slice(g * SSD_STATE, (g + 1) * SSD_STATE)
        bg = bmat[:, ns]
        cg = cmat[:, ns].astype(BF16)
        wx_t = wx[:, gs].T.astype(BF16)
        acc = jnp.zeros((SAMPLE_ROWS, SSD_GROUP_WIDTH), F32)
        for j in range(nseq):
            h0 = h0_ref[j, g]
            acc = acc + jnp.where(seq_g == j, _nt(cg, h0.astype(BF16)), 0.0)
            bj = jnp.where(seq_n == j, bg, 0.0).astype(BF16)
            col = n * j + n - 1
            h_ref[j, g] = h0 * ecs_t[gs, col:col + 1] + _dot(wx_t, bj)
        y_inter.append(acc)
    y = y + jnp.concatenate(y_inter, axis=1) * ecs
    for g in range(SSD_GROUPS):
        gs = slice(g * SSD_GROUP_WIDTH, (g + 1) * SSD_GROUP_WIDTH)
        yc_ref[:, gs] = _ssd_gate_norm(y[:, gs], z[:, gs], gn_ref[:, gs]).astype(yc_ref.dtype)


def _ssd_sample(proj, small, conv_prev, state_all, layer, prev_out, conv_w, conv_b, dt_bias, a_log, d_x, gn):
    t = proj.shape[0]
    nseq = SAMPLE_ROWS // SAMPLE_LEN
    cx = C_XBC // XBC_PART
    cz = C_Z // XBC_PART
    e = _head_expand_matrix()
    full = lambda shape: pl.BlockSpec(shape, lambda i: (0,) * len(shape))
    state_spec = pl.BlockSpec((None, nseq, SSD_GROUPS, SSD_GROUP_WIDTH, SSD_STATE), lambda i: (layer, i, 0, 0, 0))
    body, extra_specs, extra_args, aliases = _stacked_state_call(_ssd_sample_body, prev_out, 15)
    return pl.pallas_call(
        body,
        grid=(t // SAMPLE_ROWS,),
        in_specs=[
            pl.BlockSpec((SAMPLE_ROWS, XBC_PART), lambda i: (i, cx)),
            pl.BlockSpec((SAMPLE_ROWS, XBC_PART), lambda i: (i, cx + 1)),
            pl.BlockSpec((SAMPLE_ROWS, XBC_PART), lambda i: (i, cx + 2)),
            pl.BlockSpec((SAMPLE_ROWS, XBC_PART), lambda i: (i, cz)),
            pl.BlockSpec((SAMPLE_ROWS, XBC_PART), lambda i: (i, cz + 1)),
            pl.BlockSpec((SAMPLE_ROWS, LANES), lambda i: (i, SMALL_DT)),
            full((SSD_CONV, SSD_CONV_DIM)),
            full((1, SSD_CONV_DIM)),
            full((1, LANES)),
            full((1, LANES)),
            full((1, SSD_INNER)),
            full((1, SSD_INNER)),
            full((LANES, SSD_INNER)),
            pl.BlockSpec((1, SAMPLE_ROWS, SSD_CONV_DIM), lambda i: (i, 0, 0)),
            state_spec,
        ] + extra_specs,
        out_specs=[pl.BlockSpec((SAMPLE_ROWS, SSD_INNER), lambda i: (i, 0)), state_spec],
        out_shape=[
            jax.ShapeDtypeStruct((t, SSD_INNER), F32),
            jax.ShapeDtypeStruct(state_all.shape, F32),
        ],
        input_output_aliases=aliases,
        compiler_params=_params(("parallel",)),
        name="ssd_sample",
    )(proj, proj, proj, proj, proj, small, conv_w, conv_b, dt_bias, a_log, d_x, gn, e, conv_prev, state_all,
      *extra_args)


def _narrow_w_in_t(w_in_t, layer):
    k = w_in_t.shape[2]
    rows = lambda a, b: lax.slice(w_in_t, (layer, a, 0), (layer + 1, b, k)).reshape(b - a, k)
    lr = rows(NATIVE_UV - GLA_LOWRANK, NATIVE_UV)
    dt = rows(NATIVE_DT, NATIVE_GATE)
    pad = lambda n: jnp.zeros((n, k), w_in_t.dtype)
    return jnp.concatenate([lr, pad(LANES - GLA_LOWRANK), dt, pad(LANES - SSD_HEADS)], axis=0)


def _pad_lanes(v, n):
    return jnp.pad(v, (0, n - v.shape[0])).reshape(1, n)


def _layer_weights(i, w_narrow_next, g_next, g_mix, w_in, w_gla_lr, b_gla_lr, g_gla_norm, cm_ln_g, cm_ln_b,
                   cm_ws, cm_bs, ssd_conv_w, ssd_conv_b, ssd_dt_bias, ssd_a_log, ssd_d, g_ssd_norm,
                   w_br_gla, w_br_cm, w_br_ssd, w_o, g_ffn, w_ffn_in, w_ffn_out,
                   g_ple, w_ple_gate, w_ple_proj):
    row = lambda v: v[i].reshape(1, -1)
    n_tile = CM_CHUNK // SAMPLE_LEN
    return dict(
        w_narrow_next=w_narrow_next,
        layer=i,
        g_next=g_next.reshape(1, -1),
        w_lr=jnp.pad(w_gla_lr[i], ((0, LANES - GLA_LOWRANK), (0, 0))).astype(BF16),
        b_lr=row(b_gla_lr),
        g_gla=row(g_gla_norm),
        ln_g=row(cm_ln_g),
        ln_b=row(cm_ln_b),
        ws_prompt=cm_ws[i],
        bs_prompt=cm_bs[i].T,
        ws_sample=jnp.tile(cm_ws[i][:, :SAMPLE_LEN, :SAMPLE_LEN], (1, n_tile, n_tile)),
        bs_sample=jnp.tile(cm_bs[i][:, :SAMPLE_LEN].T, (n_tile, 1)),
        conv_w=ssd_conv_w[i],
        conv_b=row(ssd_conv_b),
        dt_bias=_pad_lanes(ssd_dt_bias[i], LANES),
        a_log=_pad_lanes(ssd_a_log[i], LANES),
        d_x=jnp.repeat(ssd_d[i], SSD_HEADDIM).reshape(1, SSD_INNER),
        g_ssd=row(g_ssd_norm),
        w_br_gla=w_br_gla[i].astype(BF16),
        w_br_cm=w_br_cm[i].astype(BF16),
        w_br_ssd=w_br_ssd[i].astype(BF16),
        w_o=w_o[i].astype(BF16),
        g_ffn=row(g_ffn),
        w_ffn_in=w_ffn_in,
        w_ffn_out=w_ffn_out,
        g_ple=row(g_ple),
        w_ple_gate=w_ple_gate[i].astype(BF16),
        w_ple_proj=w_ple_proj[i].astype(BF16),
    )


def _merge_out(x, proj, ya, yb, yc, w):
    tm = 512 if ya.dtype == BF16 else 256
    mix = _merge(ya, yb, yc, proj, w["w_br_gla"], w["w_br_cm"], w["w_br_ssd"], tm, 1024)
    return _out_proj(mix, w["w_o"], x, w["g_ffn"], 512)


def _ffn_ple(x, h, p_all, w):
    x = _matmul_residual(h, w["w_ffn_out"], w["layer"], x, 512, 512)
    return _ple(x, w["g_ple"], p_all, w["layer"], w["w_ple_gate"], w["w_ple_proj"], w["g_next"],
                w["w_narrow_next"], 512)


def _last_conv_rows(proj, bsz, length):
    keep = SSD_CONV - 1
    if length % SUBLANES == 0:
        rows = proj.reshape(bsz, length, proj.shape[-1])[:, length - keep:]
        return rows[:, :, C_XBC:C_XBC + SSD_CONV_DIM]
    per_pos = [lax.slice(proj, (r, C_XBC), (proj.shape[0], C_XBC + SSD_CONV_DIM), (length, 1))
               for r in range(length - keep, length)]
    return jnp.swapaxes(jnp.stack(per_pos, axis=0), 0, 1)


def _prompt_mixers(proj, small, bsz, length, w):
    ya, s_gla = _gla_prompt(proj, small, bsz, length, w["w_lr"], w["b_lr"], w["g_gla"])
    yb, _ = _cmlp(proj, w["ln_g"], w["ln_b"], w["ws_prompt"], w["bs_prompt"], CM_CHUNK, False, BF16)
    yc, s_ssm = _ssd_prompt(proj, small, bsz, length, w["conv_w"], w["conv_b"], w["dt_bias"], w["a_log"],
                            w["d_x"], w["g_ssd"])
    s_conv = _last_conv_rows(proj, bsz, length)
    s_ssm = s_ssm.reshape(bsz, SSD_HEADS, SSD_HEADDIM, SSD_STATE)
    return (ya, yb, yc), s_gla, s_ssm, s_conv


def _sample_mixers(proj, small, bsz, gla_all, ssm_all, s_conv, prev_gla, prev_ssm, w):
    n = SAMPLE_LEN
    ya, gla_out = _gla_sample(proj, small, gla_all, w["layer"], prev_gla, w["w_lr"], w["b_lr"], w["g_gla"])
    yb, v_rows = _cmlp(proj, w["ln_g"], w["ln_b"], w["ws_sample"], w["bs_sample"], n, True, F32)
    by_pos = jnp.swapaxes(s_conv, 0, 1)
    zrow = jnp.zeros((bsz // 2, SSD_CONV_DIM), F32)
    odd = [by_pos[r, 1::2] for r in range(SSD_CONV - 1)]
    even = [by_pos[r, 0::2] for r in range(SSD_CONV - 1)]
    conv_prev = jnp.stack([zrow] + odd + [zrow] + even, axis=1)
    yc, ssm_out = _ssd_sample(proj, small, conv_prev, ssm_all, w["layer"], prev_ssm, w["conv_w"], w["conv_b"],
                              w["dt_bias"], w["a_log"], w["d_x"], w["g_ssd"])
    s_conv_new = _last_conv_rows(proj, bsz, n)
    return (ya, yb, yc), gla_out, ssm_out, s_conv_new, v_rows.reshape(bsz, n, CM_WIDTH)


def kernel(x_prompt, x_sample, state_gla, state_ssm, state_conv, p_prompt, p_sample, g_mix, w_in, w_gla_lr, b_gla_lr, g_gla_norm, cm_ln_g, cm_ln_b, cm_ws, cm_bs, ssd_conv_w, ssd_conv_b, ssd_dt_bias, ssd_a_log, ssd_d, g_ssd_norm, w_br_gla, w_br_cm, w_br_ssd, w_o, g_ffn, w_ffn_in, w_ffn_out, g_ple, w_ple_gate, w_ple_proj, g_final):
    weights = (g_mix, w_in, w_gla_lr, b_gla_lr, g_gla_norm, cm_ln_g, cm_ln_b, cm_ws, cm_bs,
               ssd_conv_w, ssd_conv_b, ssd_dt_bias, ssd_a_log, ssd_d, g_ssd_norm,
               w_br_gla, w_br_cm, w_br_ssd, w_o, g_ffn, w_ffn_in, w_ffn_out,
               g_ple, w_ple_gate, w_ple_proj)
    bp, lp, _ = x_prompt.shape
    bs, ls, _ = x_sample.shape
    assert ls == SAMPLE_LEN and lp % CHUNK == 0 and bs % 2 == 0
    xp = x_prompt.reshape(bp * lp, D_MODEL)
    xs = x_sample.reshape(bs * ls, D_MODEL)
    w_in_t = jnp.swapaxes(w_in, 1, 2)
    narrow = [_narrow_w_in_t(w_in_t, i) for i in range(DEPTH)] + [None]
    g_first = g_mix[0].reshape(1, D_MODEL)
    np_, small_p = _rms_cast(xp, g_first, narrow[0], 512)
    ns, small_s = _rms_cast(xs, g_first, narrow[0], 512)
    pp_all = p_prompt.reshape(DEPTH, bp * lp, PLE_DIM)
    ps_all = p_sample.reshape(DEPTH, bs * ls, PLE_DIM)
    gla_p, ssm_p, conv_p = [], [], []
    conv_s, v_s = [], []
    ssm_all = state_ssm.reshape(DEPTH, bs, SSD_GROUPS, SSD_GROUP_WIDTH, SSD_STATE)
    gla_s = ssm_s = None
    for i in range(DEPTH):
        final = i == DEPTH - 1
        g_next = g_final if final else g_mix[i + 1]
        w = _layer_weights(i, narrow[i + 1], g_next, *weights)
        proj_p = _proj_matmul(np_, w_in_t, i, 1024)
        proj_s = _proj_matmul(ns, w_in_t, i, 512)
        ys_p, sg, sm, sc = _prompt_mixers(proj_p, small_p, bp, lp, w)
        gla_p.append(sg)
        ssm_p.append(sm)
        conv_p.append(sc)
        ys_s, gla_s, ssm_s, sc, vr = _sample_mixers(proj_s, small_s, bs, state_gla, ssm_all, state_conv[i],
                                                    gla_s, ssm_s, w)
        conv_s.append(sc)
        v_s.append(vr)
        xp, n_ffn_p = _merge_out(xp, proj_p, *ys_p, w)
        xs, n_ffn_s = _merge_out(xs, proj_s, *ys_s, w)
        h_p = _ffn_in(n_ffn_p, w_ffn_in, i, 1024, 512)
        h_s = _ffn_in(n_ffn_s, w_ffn_in, i, 512, 512)
        outs = _ffn_ple(xp, h_p, pp_all, w)
        xp, np_, small_p = (outs[0], None, None) if final else outs
        outs = _ffn_ple(xs, h_s, ps_all, w)
        xs, ns, small_s = (outs[0], None, None) if final else outs
    y_prompt = xp.reshape(bp, lp, D_MODEL)
    y_sample = xs.reshape(bs, ls, D_MODEL)
    ssm_s = ssm_s.reshape(DEPTH, bs, SSD_HEADS, SSD_HEADDIM, SSD_STATE)
    return (y_prompt, y_sample, jnp.stack(gla_p), gla_s, jnp.stack(ssm_p), ssm_s,
            jnp.stack(conv_p), jnp.stack(conv_s), jnp.stack(v_s))
```

```python
import functools

import numpy as np
import jax
import jax.numpy as jnp
from jax import lax
from jax.experimental import pallas as pl
from jax.experimental.pallas import tpu as pltpu

F32 = jnp.float32
BF16 = jnp.bfloat16

D_MODEL = 2048
DEPTH = 2
GLA_HEADS = 4
GLA_DK = 256
GLA_DV = 512
GLA_QK = 1024
GLA_V = 2048
GLA_LOWRANK = 16
GLA_TAU = 16.0
CM_GROUPS = 8
CM_WIDTH = 2048
CM_GROUP_DIM = 256
CM_CHUNK = 128
SSD_INNER = 4096
SSD_HEADDIM = 64
SSD_HEADS = 64
SSD_GROUPS = 8
SSD_STATE = 128
SSD_CONV = 4
SSD_CONV_DIM = 6144
SSD_GROUP_WIDTH = SSD_INNER // SSD_GROUPS
D_FF = 5632
PLE_DIM = 256
EPS = 1e-6

LANES = 128
SUBLANES = 8
VMEM_LIMIT = 56 * 1024 * 1024

C_Q = 0
C_K = 1024
C_V = 2048
C_G = 4096
C_UV = 6144
C_Z = 10240
C_XBC = 14336
C_GATE = 20480
C_LR = 26624
SMALL_LR = 0
SMALL_DT = 1

CHUNK = 128
SAMPLE_ROWS = 8
SAMPLE_LEN = 4
SAMPLE_SHIFT = 2
HEADDIM_SHIFT = 6


def _params(sem):
    return pltpu.CompilerParams(dimension_semantics=sem, vmem_limit_bytes=VMEM_LIMIT)


def _nt(a, b):
    return lax.dot_general(a, b, (((1,), (1,)), ((), ())), preferred_element_type=F32)


def _tn(a, b):
    return lax.dot_general(a, b, (((0,), (0,)), ((), ())), preferred_element_type=F32)


def _dot(a, b):
    return jnp.dot(a, b, preferred_element_type=F32)


def _split2(x):
    hi = x.astype(BF16)
    lo = (x - hi.astype(F32)).astype(BF16)
    return hi, lo


def _softplus(x):
    return jnp.maximum(x, 0.0) + jnp.log(1.0 + jnp.exp(-jnp.abs(x)))


def _log_sigmoid(x):
    return jnp.minimum(x, 0.0) - jnp.log(1.0 + jnp.exp(-jnp.abs(x)))


def _silu(x):
    return x * jax.nn.sigmoid(x)


def _rms_cast_body(x_ref, g_ref, wn_ref, o_ref, small_ref):
    x = x_ref[...]
    ms = jnp.mean(x * x, axis=-1, keepdims=True)
    n = (x * lax.rsqrt(ms + EPS) * g_ref[...]).astype(BF16)
    o_ref[...] = n
    small_ref[...] = _nt(n, wn_ref[...].astype(BF16))


def _rms_cast(x, g, w_narrow, tm):
    t, k = x.shape
    nn = w_narrow.shape[0]
    return pl.pallas_call(
        _rms_cast_body,
        grid=(t // tm,),
        in_specs=[pl.BlockSpec((tm, k), lambda i: (i, 0)), pl.BlockSpec((1, k), lambda i: (0, 0)),
                  pl.BlockSpec((nn, k), lambda i: (0, 0))],
        out_specs=[pl.BlockSpec((tm, k), lambda i: (i, 0)), pl.BlockSpec((tm, nn), lambda i: (i, 0))],
        out_shape=[jax.ShapeDtypeStruct((t, k), BF16), jax.ShapeDtypeStruct((t, nn), F32)],
        compiler_params=_params(("parallel",)),
        name="rms_cast",
    )(x, g, w_narrow)


NATIVE_UV = 2 * GLA_QK + 2 * GLA_V + GLA_LOWRANK
NATIVE_DT = NATIVE_UV + 2 * CM_WIDTH + SSD_INNER + SSD_CONV_DIM
NATIVE_GATE = NATIVE_DT + SSD_HEADS
PROJ_TN = 1024
W_ROW_CHUNK = 256


def _proj_body(n_ref, w_ref, wn_ref, o_ref, w_scr):
    j = pl.program_id(0)
    tn = w_ref.shape[0]

    def load_weights(shift):
        for r in range(0, tn - shift, W_ROW_CHUNK):
            rows = min(W_ROW_CHUNK, tn - shift - r)
            w_scr[r:r + rows, :] = w_ref[r + shift:r + shift + rows, :].astype(BF16)
        if shift:
            w_scr[tn - shift:tn, :] = wn_ref[0:shift, :].astype(BF16)

    @pl.when(pl.program_id(1) == 0)
    def _():
        j_uv = C_UV // PROJ_TN
        j_gate = C_GATE // PROJ_TN
        pl.when(j < j_uv)(lambda: load_weights(0))
        pl.when((j >= j_uv) & (j < j_gate))(lambda: load_weights(NATIVE_UV - C_UV))
        pl.when(j >= j_gate)(lambda: load_weights(NATIVE_GATE - C_GATE))

    o_ref[...] = _nt(n_ref[...], w_scr[...])


def _proj_matmul(n, w_in_t, layer, tm):
    t, k = n.shape
    tn = PROJ_TN
    return pl.pallas_call(
        _proj_body,
        grid=(C_LR // tn, t // tm),
        in_specs=[
            pl.BlockSpec((tm, k), lambda j, i: (i, 0)),
            pl.BlockSpec((None, tn, k), lambda j, i: (layer, j, 0)),
            pl.BlockSpec((None, LANES, k), lambda j, i: (layer, (j + 1) * (tn // LANES), 0)),
        ],
        out_specs=pl.BlockSpec((tm, tn), lambda j, i: (i, j)),
        out_shape=jax.ShapeDtypeStruct((t, C_LR), F32),
        scratch_shapes=[pltpu.VMEM((tn, k), BF16)],
        compiler_params=_params(("parallel", "arbitrary")),
        name="proj_matmul",
    )(n, w_in_t, w_in_t)


def _ffn_in_body(n_ref, wg_ref, wu_ref, o_ref, wg_scr, wu_scr):
    @pl.when(pl.program_id(1) == 0)
    def _():
        for r in range(0, wg_ref.shape[0], W_ROW_CHUNK):
            rs = slice(r, r + W_ROW_CHUNK)
            wg_scr[rs, :] = wg_ref[rs, :].astype(BF16)
            wu_scr[rs, :] = wu_ref[rs, :].astype(BF16)

    n = n_ref[...]
    gate = _dot(n, wg_scr[...])
    up = _dot(n, wu_scr[...])
    o_ref[...] = (_silu(gate) * up).astype(o_ref.dtype)


def _ffn_in(n, w, layer, tm, tn):
    t, k = n.shape
    nblk = D_FF // tn
    return pl.pallas_call(
        _ffn_in_body,
        grid=(nblk, t // tm),
        in_specs=[
            pl.BlockSpec((tm, k), lambda j, i: (i, 0)),
            pl.BlockSpec((None, k, tn), lambda j, i: (layer, 0, j)),
            pl.BlockSpec((None, k, tn), lambda j, i: (layer, 0, j + nblk)),
        ],
        out_specs=pl.BlockSpec((tm, tn), lambda j, i: (i, j)),
        out_shape=jax.ShapeDtypeStruct((t, D_FF), BF16),
        scratch_shapes=[pltpu.VMEM((k, tn), BF16), pltpu.VMEM((k, tn), BF16)],
        compiler_params=_params(("parallel", "arbitrary")),
        name="ffn_in",
    )(n, w, w)


def _mm_res_body(a_ref, w_ref, r_ref, o_ref, w_scr):
    @pl.when(pl.program_id(1) == 0)
    def _():
        for r in range(0, w_ref.shape[0], W_ROW_CHUNK):
            rs = slice(r, r + W_ROW_CHUNK)
            w_scr[rs, :] = w_ref[rs, :].astype(BF16)

    o_ref[...] = r_ref[...] + _dot(a_ref[...], w_scr[...])


def _matmul_residual(a, w, layer, res, tm, tn):
    t, k = a.shape
    n = w.shape[2]
    return pl.pallas_call(
        _mm_res_body,
        grid=(n // tn, t // tm),
        in_specs=[
            pl.BlockSpec((tm, k), lambda j, i: (i, 0)),
            pl.BlockSpec((None, k, tn), lambda j, i: (layer, 0, j)),
            pl.BlockSpec((tm, tn), lambda j, i: (i, j)),
        ],
        out_specs=pl.BlockSpec((tm, tn), lambda j, i: (i, j)),
        out_shape=jax.ShapeDtypeStruct((t, n), F32),
        scratch_shapes=[pltpu.VMEM((k, tn), BF16)],
        compiler_params=_params(("parallel", "arbitrary")),
        name="matmul_residual",
    )(a, w, res)


def _merge_body(ya_ref, yb_ref, yc_ref, ga_ref, gb_ref, gc_ref, wa_ref, wb_ref, wc_ref, o_ref):
    acc = jax.nn.sigmoid(ga_ref[...]) * _dot(ya_ref[...].astype(BF16), wa_ref[...])
    acc += jax.nn.sigmoid(gb_ref[...]) * _dot(yb_ref[...].astype(BF16), wb_ref[...])
    acc += jax.nn.sigmoid(gc_ref[...]) * _dot(yc_ref[...].astype(BF16), wc_ref[...])
    o_ref[...] = acc.astype(o_ref.dtype)


def _resident(shape, index_map):
    return pl.BlockSpec(shape, index_map, pipeline_mode=pl.Buffered(1))


def _merge(ya, yb, yc, proj, wa, wb, wc, tm, tn):
    t = ya.shape[0]
    nblk = D_MODEL // tn
    g0 = C_GATE // tn
    return pl.pallas_call(
        _merge_body,
        grid=(nblk, t // tm),
        in_specs=[
            pl.BlockSpec((tm, GLA_V), lambda j, i: (i, 0)),
            pl.BlockSpec((tm, CM_WIDTH), lambda j, i: (i, 0)),
            pl.BlockSpec((tm, SSD_INNER), lambda j, i: (i, 0)),
            pl.BlockSpec((tm, tn), lambda j, i: (i, g0 + j)),
            pl.BlockSpec((tm, tn), lambda j, i: (i, g0 + nblk + j)),
            pl.BlockSpec((tm, tn), lambda j, i: (i, g0 + 2 * nblk + j)),
            _resident((GLA_V, tn), lambda j, i: (0, j)),
            _resident((CM_WIDTH, tn), lambda j, i: (0, j)),
            _resident((SSD_INNER, tn), lambda j, i: (0, j)),
        ],
        out_specs=pl.BlockSpec((tm, tn), lambda j, i: (i, j)),
        out_shape=jax.ShapeDtypeStruct((t, D_MODEL), BF16),
        compiler_params=_params(("parallel", "arbitrary")),
        name="merge",
    )(ya, yb, yc, proj, proj, proj, wa, wb, wc)


def _rms(x, g):
    ms = jnp.mean(x * x, axis=-1, keepdims=True)
    return x * lax.rsqrt(ms + EPS) * g


def _out_proj_body(mix_ref, w_ref, x_ref, g_ref, x1_ref, n_ref):
    x1 = x_ref[...] + _dot(mix_ref[...], w_ref[...])
    x1_ref[...] = x1
    n_ref[...] = _rms(x1, g_ref[...]).astype(n_ref.dtype)


def _out_proj(mix, w, x, g_next, tm):
    t, k = mix.shape
    row_block = lambda width: pl.BlockSpec((tm, width), lambda i: (i, 0))
    return pl.pallas_call(
        _out_proj_body,
        grid=(t // tm,),
        in_specs=[
            row_block(k),
            _resident((k, D_MODEL), lambda i: (0, 0)),
            row_block(D_MODEL),
            pl.BlockSpec((1, D_MODEL), lambda i: (0, 0)),
        ],
        out_specs=[row_block(D_MODEL), row_block(D_MODEL)],
        out_shape=[jax.ShapeDtypeStruct((t, D_MODEL), F32), jax.ShapeDtypeStruct((t, D_MODEL), BF16)],
        compiler_params=_params(("parallel",)),
        name="out_proj",
    )(mix, w, x, g_next)


def _ple_body(x_ref, g_ref, p_ref, wg_ref, wp_ref, gn_ref, *rest, final):
    x = x_ref[...]
    n = _rms(x, g_ref[...]).astype(BF16)
    gate = jax.nn.sigmoid(_dot(n, wg_ref[...]))
    emb = _dot(p_ref[...].astype(BF16), wp_ref[...])
    y = x + gate * emb
    if final:
        rest[0][...] = _rms(y, gn_ref[...])
    else:
        wn_ref, y_ref, n_ref, small_ref = rest
        y_ref[...] = y
        n_next = _rms(y, gn_ref[...]).astype(BF16)
        n_ref[...] = n_next
        small_ref[...] = _nt(n_next, wn_ref[...].astype(BF16))


def _ple(x, g, p_all, layer, wg, wp, g_next, w_narrow_next, tm):
    t = x.shape[0]
    final = w_narrow_next is None
    row_block = lambda width: pl.BlockSpec((tm, width), lambda i: (i, 0))
    whole = lambda a: _resident(a.shape, lambda i: (0, 0))
    in_specs = [row_block(D_MODEL), whole(g), pl.BlockSpec((None, tm, PLE_DIM), lambda i: (layer, i, 0)),
                whole(wg), whole(wp), whole(g_next)]
    args = [x, g, p_all, wg, wp, g_next]
    if final:
        out_specs = [row_block(D_MODEL)]
        out_shape = [jax.ShapeDtypeStruct((t, D_MODEL), F32)]
    else:
        nn = w_narrow_next.shape[0]
        in_specs.append(whole(w_narrow_next))
        args.append(w_narrow_next)
        out_specs = [row_block(D_MODEL), row_block(D_MODEL), row_block(nn)]
        out_shape = [jax.ShapeDtypeStruct((t, D_MODEL), F32), jax.ShapeDtypeStruct((t, D_MODEL), BF16),
                     jax.ShapeDtypeStruct((t, nn), F32)]
    return pl.pallas_call(
        functools.partial(_ple_body, final=final),
        grid=(t // tm,),
        in_specs=in_specs,
        out_specs=out_specs,
        out_shape=out_shape,
        compiler_params=_params(("parallel",)),
        name="ple",
    )(*args)


CM_STEP_CHUNKS = 2


def _cmlp_body(u_ref, v_ref, lng_ref, lnb_ref, ws_ref, bst_ref, yb_ref, *rest, seq_len):
    r = lax.broadcasted_iota(jnp.int32, (CM_CHUNK, CM_CHUNK), 0)
    c = lax.broadcasted_iota(jnp.int32, (CM_CHUNK, CM_CHUNK), 1)
    if seq_len >= CM_CHUNK:
        keep = r >= c
    else:
        sh = seq_len.bit_length() - 1
        keep = ((r >> sh) == (c >> sh)) & ((r & (seq_len - 1)) >= (c & (seq_len - 1)))
    ws = [jnp.where(keep, ws_ref[g], 0.0).astype(BF16) for g in range(CM_GROUPS)]
    for k in range(CM_STEP_CHUNKS):
        rows = slice(k * CM_CHUNK, (k + 1) * CM_CHUNK)
        u = jax.nn.gelu(u_ref[rows, :])
        v = jax.nn.gelu(v_ref[rows, :])
        mu = jnp.mean(v, axis=-1, keepdims=True)
        vc = v - mu
        var = jnp.mean(vc * vc, axis=-1, keepdims=True)
        vn = vc * lax.rsqrt(var + EPS) * lng_ref[...] + lnb_ref[...]
        if rest:
            rest[0][rows, :] = vn
        for g in range(CM_GROUPS):
            sl = slice(g * CM_GROUP_DIM, (g + 1) * CM_GROUP_DIM)
            mixed = _dot(ws[g], vn[:, sl].astype(BF16)) + bst_ref[:, g:g + 1]
            yb_ref[rows, sl] = (u[:, sl] * mixed).astype(yb_ref.dtype)


def _cmlp(proj, ln_g, ln_b, ws_tiled, bs_t, seq_len, emit_v, out_dtype):
    t = proj.shape[0]
    step_rows = CM_STEP_CHUNKS * CM_CHUNK
    out_shape = [jax.ShapeDtypeStruct((t, CM_WIDTH), out_dtype)]
    out_specs = [pl.BlockSpec((step_rows, CM_WIDTH), lambda i: (i, 0))]
    if emit_v:
        out_shape.append(jax.ShapeDtypeStruct((t, CM_WIDTH), F32))
        out_specs.append(pl.BlockSpec((step_rows, CM_WIDTH), lambda i: (i, 0)))
    cu = C_UV // CM_WIDTH
    res = pl.pallas_call(
        functools.partial(_cmlp_body, seq_len=seq_len),
        grid=(t // step_rows,),
        in_specs=[
            pl.BlockSpec((step_rows, CM_WIDTH), lambda i: (i, cu)),
            pl.BlockSpec((step_rows, CM_WIDTH), lambda i: (i, cu + 1)),
            pl.BlockSpec((1, CM_WIDTH), lambda i: (0, 0)),
            pl.BlockSpec((1, CM_WIDTH), lambda i: (0, 0)),
            pl.BlockSpec((CM_GROUPS, CM_CHUNK, CM_CHUNK), lambda i: (0, 0, 0)),
            pl.BlockSpec((CM_CHUNK, CM_GROUPS), lambda i: (0, 0)),
        ],
        out_specs=out_specs,
        out_shape=out_shape,
        compiler_params=_params(("parallel",)),
        name="chunk_mlp",
    )(proj, proj, ln_g, ln_b, ws_tiled, bs_t)
    return (res[0], res[1]) if emit_v else (res[0], None)


def _gla_level_constants(n):
    levels = n.bit_length() - 1
    t = np.arange(n)[:, None]
    j = np.arange(n)[None, :]
    sums = [j <= t, j > t]
    masks = [t == j]
    for l in range(1, levels + 1):
        w, half = 1 << l, 1 << (l - 1)
        start = (t >> l) << l
        upper = ((t >> (l - 1)) & 1) == 1
        a_up = (j >= start + half) & (j <= t)
        a_lo = (j > t) & (j < start + half)
        sums.append(np.where(upper, a_up, a_lo))
        jj = j
        upper_t = ((t >> (l - 1)) & 1) == 1
        lower_s = ((jj >> (l - 1)) & 1) == 0
        masks.append(((t >> l) == (jj >> l)) & upper_t & lower_s)
    return (jnp.asarray(np.stack(sums), BF16), jnp.asarray(np.stack(masks), F32), levels)


def _gla_log_decay(lr_ref, wlr_ref, blr_ref):
    x = _dot(lr_ref[...].astype(BF16), wlr_ref[...]) + blr_ref[...]
    return _log_sigmoid(x) * (1.0 / GLA_TAU)


def _gla_finish(o, g, gn):
    ms = jnp.mean(o * o, axis=-1, keepdims=True)
    return o * lax.rsqrt(ms + EPS) * gn * _silu(g)


def _gla_prompt_body(q_ref, k_ref, v_ref, g_ref, lr_ref, wlr_ref, blr_ref, gn_ref, a_ref, m_ref,
                     ya_ref, s_ref, *, levels):
    rows = q_ref.shape[0]

    @pl.when(pl.program_id(1) == 0)
    def _():
        s_ref[...] = jnp.zeros(s_ref.shape, F32)

    q = q_ref[...] * (GLA_DK ** -0.5)
    k = k_ref[...]
    la = _gla_log_decay(lr_ref, wlr_ref, blr_ref)
    la2 = jnp.concatenate(_split2(la), axis=1)

    def decay_sum(i):
        d = _dot(a_ref[i], la2)
        return d[:, :GLA_QK] + d[:, GLA_QK:]

    b = decay_sum(0)
    qe = (q * jnp.exp(b)).astype(BF16)
    kd = (k * jnp.exp(decay_sum(1))).astype(BF16)
    e_end_t = jnp.exp(jnp.broadcast_to(b[rows - 1:rows, :], (SUBLANES, GLA_QK))).T
    row = lax.broadcasted_iota(jnp.int32, (rows, GLA_QK), 0)
    xs = []
    for l in range(1, levels + 1):
        e = jnp.exp(decay_sum(l + 1))
        upper = ((row >> (l - 1)) & 1) == 1
        xs.append((jnp.where(upper, q, k) * e).astype(BF16))
    qb = q.astype(BF16)
    kb = k.astype(BF16)
    for h in range(GLA_HEADS):
        ks = slice(h * GLA_DK, (h + 1) * GLA_DK)
        vs = slice(h * GLA_DV, (h + 1) * GLA_DV)
        att = m_ref[0] * _nt(qb[:, ks], kb[:, ks])
        for l in range(1, levels + 1):
            x = xs[l - 1][:, ks]
            att += m_ref[l] * _nt(x, x)
        vh = v_ref[:, vs].astype(BF16)
        s = s_ref[0, h]
        o = _dot(qe[:, ks], s.astype(BF16)) + _dot(att.astype(BF16), vh)
        s_ref[0, h] = s * e_end_t[ks, 0:1] + _tn(kd[:, ks], vh)
        ya_ref[:, vs] = _gla_finish(o, g_ref[:, vs], gn_ref[:, vs]).astype(ya_ref.dtype)


def _gla_prompt(proj, small, bsz, length, w_lr, b_lr, gn):
    nchunk = length // CHUNK
    a_mats, masks, levels = _gla_level_constants(CHUNK)
    row = lambda b, c: b * nchunk + c
    return pl.pallas_call(
        functools.partial(_gla_prompt_body, levels=levels),
        grid=(bsz, nchunk),
        in_specs=[
            pl.BlockSpec((CHUNK, GLA_QK), lambda b, c: (row(b, c), C_Q // GLA_QK)),
            pl.BlockSpec((CHUNK, GLA_QK), lambda b, c: (row(b, c), C_K // GLA_QK)),
            pl.BlockSpec((CHUNK, GLA_V), lambda b, c: (row(b, c), C_V // GLA_V)),
            pl.BlockSpec((CHUNK, GLA_V), lambda b, c: (row(b, c), C_G // GLA_V)),
            pl.BlockSpec((CHUNK, LANES), lambda b, c: (row(b, c), SMALL_LR)),
            pl.BlockSpec((LANES, GLA_QK), lambda b, c: (0, 0)),
            pl.BlockSpec((1, GLA_QK), lambda b, c: (0, 0)),
            pl.BlockSpec((1, GLA_V), lambda b, c: (0, 0)),
            pl.BlockSpec(a_mats.shape, lambda b, c: (0, 0, 0)),
            pl.BlockSpec(masks.shape, lambda b, c: (0, 0, 0)),
        ],
        out_specs=[
            pl.BlockSpec((CHUNK, GLA_V), lambda b, c: (row(b, c), 0)),
            pl.BlockSpec((1, GLA_HEADS, GLA_DK, GLA_DV), lambda b, c: (b, 0, 0, 0)),
        ],
        out_shape=[
            jax.ShapeDtypeStruct((bsz * length, GLA_V), BF16),
            jax.ShapeDtypeStruct((bsz, GLA_HEADS, GLA_DK, GLA_DV), F32),
        ],
        compiler_params=_params(("parallel", "arbitrary")),
        name="gla_prompt",
    )(proj, proj, proj, proj, small, w_lr, b_lr, gn, a_mats, masks)


def _roll_rows(x, d):
    return pltpu.roll(x, d, 0) if d else x


def _gla_sample_body(q_ref, k_ref, v_ref, g_ref, lr_ref, wlr_ref, blr_ref, gn_ref, s0_ref,
                     ya_ref, s_ref):
    n = SAMPLE_LEN
    pos_k = lax.broadcasted_iota(jnp.int32, (SAMPLE_ROWS, GLA_QK), 0) & (n - 1)
    row_v = lax.broadcasted_iota(jnp.int32, (SAMPLE_ROWS, GLA_DV), 0)
    q = q_ref[...] * (GLA_DK ** -0.5)
    k = k_ref[...]
    v = v_ref[...]
    la = _gla_log_decay(lr_ref, wlr_ref, blr_ref)
    back = [_roll_rows(la, d) for d in range(n)]
    win = [None, la]
    for d in range(2, n):
        win.append(win[-1] + back[d - 1])
    b = la
    suffix = jnp.zeros_like(la)
    for d in range(1, n):
        b = b + jnp.where(pos_k >= d, back[d], 0.0)
        suffix = suffix + jnp.where(pos_k < n - d, pltpu.roll(la, SAMPLE_ROWS - d, 0), 0.0)
    eb = jnp.exp(b)
    qe = (q * eb).astype(BF16)
    kd = k * jnp.exp(suffix)
    eb_t = eb.T
    nseq = SAMPLE_ROWS // n
    seq_k = lax.broadcasted_iota(jnp.int32, (SAMPLE_ROWS, GLA_QK), 0) >> SAMPLE_SHIFT
    kd_seq = [jnp.where(seq_k == j, kd, 0.0).astype(BF16) for j in range(nseq)]
    pair = []
    for d in range(n):
        p = q * _roll_rows(k, d)
        if d:
            p = p * jnp.exp(win[d])
        pair.append(p)
    vback = [_roll_rows(v, d) for d in range(n)]
    for h in range(GLA_HEADS):
        ks = slice(h * GLA_DK, (h + 1) * GLA_DK)
        vs = slice(h * GLA_DV, (h + 1) * GLA_DV)
        o = jnp.zeros((SAMPLE_ROWS, GLA_DV), F32)
        for d in range(n):
            score = jnp.sum(pair[d][:, ks], axis=-1, keepdims=True)
            o = o + jnp.where((row_v & (n - 1)) >= d, score * vback[d][:, vs], 0.0)
        vh = v[:, vs].astype(BF16)
        for j in range(nseq):
            s = s0_ref[j, h]
            o = o + jnp.where((row_v >> SAMPLE_SHIFT) == j, _dot(qe[:, ks], s.astype(BF16)), 0.0)
            col = n * j + n - 1
            s_ref[j, h] = s * eb_t[ks, col:col + 1] + _tn(kd_seq[j][:, ks], vh)
        ya_ref[:, vs] = _gla_finish(o, g_ref[:, vs], gn_ref[:, vs]).astype(ya_ref.dtype)


def _stacked_state_call(body, prev_out, n_in):
    if prev_out is None:
        return body, [], [], {}
    wrapped = lambda *refs: body(*refs[:n_in], *refs[n_in + 1:])
    return wrapped, [pl.BlockSpec(memory_space=pl.ANY)], [prev_out], {n_in: 1}


def _gla_sample(proj, small, state_all, layer, prev_out, w_lr, b_lr, gn):
    t = proj.shape[0]
    nseq = SAMPLE_ROWS // SAMPLE_LEN
    state_spec = pl.BlockSpec((None, nseq, GLA_HEADS, GLA_DK, GLA_DV), lambda i: (layer, i, 0, 0, 0))
    body, extra_specs, extra_args, aliases = _stacked_state_call(_gla_sample_body, prev_out, 9)
    return pl.pallas_call(
        body,
        grid=(t // SAMPLE_ROWS,),
        in_specs=[
            pl.BlockSpec((SAMPLE_ROWS, GLA_QK), lambda i: (i, C_Q // GLA_QK)),
            pl.BlockSpec((SAMPLE_ROWS, GLA_QK), lambda i: (i, C_K // GLA_QK)),
            pl.BlockSpec((SAMPLE_ROWS, GLA_V), lambda i: (i, C_V // GLA_V)),
            pl.BlockSpec((SAMPLE_ROWS, GLA_V), lambda i: (i, C_G // GLA_V)),
            pl.BlockSpec((SAMPLE_ROWS, LANES), lambda i: (i, SMALL_LR)),
            pl.BlockSpec((LANES, GLA_QK), lambda i: (0, 0)),
            pl.BlockSpec((1, GLA_QK), lambda i: (0, 0)),
            pl.BlockSpec((1, GLA_V), lambda i: (0, 0)),
            state_spec,
        ] + extra_specs,
        out_specs=[pl.BlockSpec((SAMPLE_ROWS, GLA_V), lambda i: (i, 0)), state_spec],
        out_shape=[
            jax.ShapeDtypeStruct((t, GLA_V), F32),
            jax.ShapeDtypeStruct(state_all.shape, F32),
        ],
        input_output_aliases=aliases,
        compiler_params=_params(("parallel",)),
        name="gla_sample",
    )(proj, proj, proj, proj, small, w_lr, b_lr, gn, state_all, *extra_args)


XBC_PART = 2048


def _ssd_gate_norm(y, z, gn):
    yz = y * _silu(z)
    ms = jnp.mean(yz * yz, axis=-1, keepdims=True)
    return yz * lax.rsqrt(ms + EPS) * gn


def _ssd_prompt_body(x1_ref, x2_ref, bc_ref, z1_ref, z2_ref, dt_ref, cw_ref, cb_ref, dtb_ref, alog_ref,
                     dx_ref, gn_ref, e_ref, a_ref, hm_ref, yc_ref, h_ref, tail_scr, ht_scr):
    rows = x1_ref.shape[0]
    step = pl.program_id(1)

    @pl.when(step == 0)
    def _():
        tail_scr[...] = jnp.zeros(tail_scr.shape, F32)
        ht_scr[...] = jnp.zeros(ht_scr.shape, F32)

    row8 = lax.broadcasted_iota(jnp.int32, (SUBLANES, XBC_PART), 0)

    def conv(x_ref, part):
        cs = slice(part * XBC_PART, (part + 1) * XBC_PART)
        x = x_ref[...]
        prev = tail_scr[:, cs]
        acc = cb_ref[:, cs] + x * cw_ref[SSD_CONV - 1:SSD_CONV, cs]
        for d in range(1, SSD_CONV):
            xr = pltpu.roll(x, d, 0)
            first = jnp.where(row8 < d, pltpu.roll(prev, d, 0), xr[:SUBLANES])
            xd = jnp.concatenate([first, xr[SUBLANES:]], axis=0)
            acc = acc + xd * cw_ref[SSD_CONV - 1 - d:SSD_CONV - d, cs]
        tail_scr[:, cs] = x[rows - SUBLANES:, :]
        return _silu(acc)

    xs_halves = [conv(x1_ref, 0), conv(x2_ref, 1)]
    bc = conv(bc_ref, 2)
    half_w = SSD_GROUPS * SSD_STATE
    bmat = bc[:, :half_w]
    cmat = bc[:, half_w:]
    z_halves = [z1_ref, z2_ref]

    dt = _softplus(dt_ref[...] + dtb_ref[...])
    a = dt * (-jnp.exp(alog_ref[...]))
    a2 = jnp.concatenate(_split2(a), axis=1)

    def head_sum(i):
        d = _dot(a_ref[i], a2)
        return d[:, :LANES] + d[:, LANES:]

    cs_in = head_sum(0)
    cs_suf = head_sum(1)
    cs_t = cs_in.T
    cs_hi, cs_lo = _split2(cs_in)
    suf_hi, suf_lo = _split2(cs_suf)
    dt_hi, dt_lo = _split2(dt)
    r_i = lax.broadcasted_iota(jnp.int32, (rows, rows), 0)
    c_i = lax.broadcasted_iota(jnp.int32, (rows, rows), 1)
    causal = r_i >= c_i
    heads_per_group = SSD_HEADS // SSD_GROUPS
    per_half = SSD_GROUPS // 2
    for g in range(SSD_GROUPS):
        gs = slice(g * SSD_GROUP_WIDTH, (g + 1) * SSD_GROUP_WIDTH)
        ls = slice((g % per_half) * SSD_GROUP_WIDTH, (g % per_half + 1) * SSD_GROUP_WIDTH)
        ns = slice(g * SSD_STATE, (g + 1) * SSD_STATE)
        xg = xs_halves[g // per_half][:, ls]
        eg = e_ref[:, gs]
        csx = _dot(cs_hi, eg) + _dot(cs_lo, eg)
        sufx = _dot(suf_hi, eg) + _dot(suf_lo, eg)
        dtx = _dot(dt_hi, eg) + _dot(dt_lo, eg)
        bg = bmat[:, ns].astype(BF16)
        cg = cmat[:, ns].astype(BF16)
        cb = jnp.where(causal, _nt(cg, bg), 0.0)
        xdt = dtx * xg
        xdt_b = xdt.astype(BF16)
        ws = []
        xb = []
        for r in range(heads_per_group):
            h = g * heads_per_group + r
            dm = cs_in[:, h:h + 1] - cs_t[h:h + 1, :]
            ws.append((cb * jnp.exp(jnp.minimum(dm, 0.0))).astype(BF16))
            xb.append(xdt_b * hm_ref[r])
        y = _dot(jnp.concatenate(ws, axis=1), jnp.concatenate(xb, axis=0))
        ht = ht_scr[g]
        y = y + _dot(cg, ht.astype(BF16)) * jnp.exp(csx)
        y = y + dx_ref[:, gs] * xg
        wx = (jnp.exp(sufx) * xdt).astype(BF16)
        ht_scr[g] = ht * jnp.exp(csx[rows - 1:rows, :]) + _tn(bg, wx)
        z = z_halves[g // per_half][:, ls]
        yc_ref[:, gs] = _ssd_gate_norm(y, z, gn_ref[:, gs]).astype(yc_ref.dtype)

    @pl.when(step == pl.num_programs(1) - 1)
    def _():
        for g in range(SSD_GROUPS):
            h_ref[0, g] = ht_scr[g].T


def _tri_constants(n):
    t = np.arange(n)[:, None]
    j = np.arange(n)[None, :]
    return jnp.asarray(np.stack([j <= t, j > t]), BF16)


def _head_lane_masks(rows):
    lane = np.arange(SSD_GROUP_WIDTH)[None, None, :]
    r = np.arange(SSD_HEADS // SSD_GROUPS)[:, None, None]
    return jnp.asarray(np.broadcast_to(lane // SSD_HEADDIM == r, (SSD_HEADS // SSD_GROUPS, rows, SSD_GROUP_WIDTH)), BF16)


def _head_expand_matrix():
    h = np.arange(LANES)[:, None]
    lane = np.arange(SSD_INNER)[None, :]
    return jnp.asarray(h == lane // SSD_HEADDIM, BF16)


def _ssd_prompt(proj, small, bsz, length, conv_w, conv_b, dt_bias, a_log, d_x, gn):
    nchunk = length // CHUNK
    row = lambda b, c: b * nchunk + c
    cx = C_XBC // XBC_PART
    cz = C_Z // XBC_PART
    tri = _tri_constants(CHUNK)
    e = _head_expand_matrix()
    head_masks = _head_lane_masks(CHUNK)
    full = lambda shape: pl.BlockSpec(shape, lambda b, c: (0,) * len(shape))
    return pl.pallas_call(
        _ssd_prompt_body,
        grid=(bsz, nchunk),
        in_specs=[
            pl.BlockSpec((CHUNK, XBC_PART), lambda b, c: (row(b, c), cx)),
            pl.BlockSpec((CHUNK, XBC_PART), lambda b, c: (row(b, c), cx + 1)),
            pl.BlockSpec((CHUNK, XBC_PART), lambda b, c: (row(b, c), cx + 2)),
            pl.BlockSpec((CHUNK, XBC_PART), lambda b, c: (row(b, c), cz)),
            pl.BlockSpec((CHUNK, XBC_PART), lambda b, c: (row(b, c), cz + 1)),
            pl.BlockSpec((CHUNK, LANES), lambda b, c: (row(b, c), SMALL_DT)),
            full((SSD_CONV, SSD_CONV_DIM)),
            full((1, SSD_CONV_DIM)),
            full((1, LANES)),
            full((1, LANES)),
            full((1, SSD_INNER)),
            full((1, SSD_INNER)),
            full((LANES, SSD_INNER)),
            full(tri.shape),
            full(head_masks.shape),
        ],
        out_specs=[
            pl.BlockSpec((CHUNK, SSD_INNER), lambda b, c: (row(b, c), 0)),
            pl.BlockSpec((1, SSD_GROUPS, SSD_GROUP_WIDTH, SSD_STATE), lambda b, c: (b, 0, 0, 0)),
        ],
        out_shape=[
            jax.ShapeDtypeStruct((bsz * length, SSD_INNER), BF16),
            jax.ShapeDtypeStruct((bsz, SSD_GROUPS, SSD_GROUP_WIDTH, SSD_STATE), F32),
        ],
        scratch_shapes=[
            pltpu.VMEM((SUBLANES, SSD_CONV_DIM), F32),
            pltpu.VMEM((SSD_GROUPS, SSD_STATE, SSD_GROUP_WIDTH), F32),
        ],
        compiler_params=_params(("parallel", "arbitrary")),
        name="ssd_prompt",
    )(proj, proj, proj, proj, proj, small, conv_w, conv_b, dt_bias, a_log, d_x, gn, e, tri, head_masks)


def _ssd_sample_body(x1_ref, x2_ref, bc_ref, z1_ref, z2_ref, dt_ref, cw_ref, cb_ref, dtb_ref, alog_ref,
                     dx_ref, gn_ref, e_ref, prev_ref, h0_ref, yc_ref, h_ref):
    n = SAMPLE_LEN
    nseq = SAMPLE_ROWS // n
    pos_p = lax.broadcasted_iota(jnp.int32, (SAMPLE_ROWS, XBC_PART), 0) & (n - 1)

    def conv(x_ref, part):
        cs = slice(part * XBC_PART, (part + 1) * XBC_PART)
        x = x_ref[...]
        prev = prev_ref[0][:, cs]
        acc = cb_ref[:, cs] + x * cw_ref[SSD_CONV - 1:SSD_CONV, cs]
        for d in range(1, SSD_CONV):
            xd = jnp.where(pos_p < d, pltpu.roll(prev, d, 0), pltpu.roll(x, d, 0))
            acc = acc + xd * cw_ref[SSD_CONV - 1 - d:SSD_CONV - d, cs]
        return _silu(acc)

    xs = jnp.concatenate([conv(x1_ref, 0), conv(x2_ref, 1)], axis=1)
    bc = conv(bc_ref, 2)
    half_w = SSD_GROUPS * SSD_STATE
    bmat = bc[:, :half_w]
    cmat = bc[:, half_w:]
    z = jnp.concatenate([z1_ref[...], z2_ref[...]], axis=1)

    dt = _softplus(dt_ref[...] + dtb_ref[...])
    dt_hi, dt_lo = _split2(dt)
    dtx = _dot(dt_hi, e_ref[...]) + _dot(dt_lo, e_ref[...])
    a_hi, a_lo = _split2(-jnp.exp(alog_ref[...]))
    anegx = _dot(jnp.broadcast_to(a_hi, (SUBLANES, LANES)), e_ref[...]) + \
        _dot(jnp.broadcast_to(a_lo, (SUBLANES, LANES)), e_ref[...])
    ax = dtx * anegx
    pos = lax.broadcasted_iota(jnp.int32, (SAMPLE_ROWS, SSD_INNER), 0) & (n - 1)
    seq_g = lax.broadcasted_iota(jnp.int32, (SAMPLE_ROWS, SSD_GROUP_WIDTH), 0) >> SAMPLE_SHIFT
    back = [_roll_rows(ax, d) for d in range(n)]
    win = [None, ax]
    for d in range(2, n):
        win.append(win[-1] + back[d - 1])
    csx = ax
    sufx = jnp.zeros_like(ax)
    for d in range(1, n):
        csx = csx + jnp.where(pos >= d, back[d], 0.0)
        sufx = sufx + jnp.where(pos < n - d, pltpu.roll(ax, SAMPLE_ROWS - d, 0), 0.0)
    ecs = jnp.exp(csx)
    ecs_t = ecs.T
    wx = jnp.exp(sufx) * dtx * xs

    y = dx_ref[...] * xs
    for d in range(n):
        prod = cmat * _roll_rows(bmat, d)
        cbx = jnp.concatenate(
            [jnp.broadcast_to(jnp.sum(prod[:, g * SSD_STATE:(g + 1) * SSD_STATE], axis=-1, keepdims=True),
                              (SAMPLE_ROWS, SSD_GROUP_WIDTH)) for g in range(SSD_GROUPS)], axis=1)
        term = cbx * _roll_rows(dtx, d) * _roll_rows(xs, d)
        if d:
            term = term * jnp.exp(win[d])
        y = y + jnp.where(pos >= d, term, 0.0)

    y_inter = []
    seq_n = lax.broadcasted_iota(jnp.int32, (SAMPLE_ROWS, SSD_STATE), 0) >> SAMPLE_SHIFT
    for g in range(SSD_GROUPS):
        gs = slice(g * SSD_GROUP_WIDTH, (g + 1) * SSD_GROUP_WIDTH)
        ns = slice(g * SSD_STATE, (g + 1) * SSD_STATE)
        bg = bmat[:, ns]
        cg = cmat[:, ns].astype(BF16)
        wx_t = wx[:, gs].T.astype(BF16)
        acc = jnp.zeros((SAMPLE_ROWS, SSD_GROUP_WIDTH), F32)
        for j in range(nseq):
            h0 = h0_ref[j, g]
            acc = acc + jnp.where(seq_g == j, _nt(cg, h0.astype(BF16)), 0.0)
            bj = jnp.where(seq_n == j, bg, 0.0).astype(BF16)
            col = n * j + n - 1
            h_ref[j, g] = h0 * ecs_t[gs, col:col + 1] + _dot(wx_t, bj)
        y_inter.append(acc)
    y = y + jnp.concatenate(y_inter, axis=1) * ecs
    for g in range(SSD_GROUPS):
        gs = slice(g * SSD_GROUP_WIDTH, (g + 1) * SSD_GROUP_WIDTH)
        yc_ref[:, gs] = _ssd_gate_norm(y[:, gs], z[:, gs], gn_ref[:, gs]).astype(yc_ref.dtype)


def _ssd_sample(proj, small, conv_prev, state_all, layer, prev_out, conv_w, conv_b, dt_bias, a_log, d_x, gn):
    t = proj.shape[0]
    nseq = SAMPLE_ROWS // SAMPLE_LEN
    cx = C_XBC // XBC_PART
    cz = C_Z // XBC_PART
    e = _head_expand_matrix()
    full = lambda shape: pl.BlockSpec(shape, lambda i: (0,) * len(shape))
    state_spec = pl.BlockSpec((None, nseq, SSD_GROUPS, SSD_GROUP_WIDTH, SSD_STATE), lambda i: (layer, i, 0, 0, 0))
    body, extra_specs, extra_args, aliases = _stacked_state_call(_ssd_sample_body, prev_out, 15)
    return pl.pallas_call(
        body,
        grid=(t // SAMPLE_ROWS,),
        in_specs=[
            pl.BlockSpec((SAMPLE_ROWS, XBC_PART), lambda i: (i, cx)),
            pl.BlockSpec((SAMPLE_ROWS, XBC_PART), lambda i: (i, cx + 1)),
            pl.BlockSpec((SAMPLE_ROWS, XBC_PART), lambda i: (i, cx + 2)),
            pl.BlockSpec((SAMPLE_ROWS, XBC_PART), lambda i: (i, cz)),
            pl.BlockSpec((SAMPLE_ROWS, XBC_PART), lambda i: (i, cz + 1)),
            pl.BlockSpec((SAMPLE_ROWS, LANES), lambda i: (i, SMALL_DT)),
            full((SSD_CONV, SSD_CONV_DIM)),
            full((1, SSD_CONV_DIM)),
            full((1, LANES)),
            full((1, LANES)),
            full((1, SSD_INNER)),
            full((1, SSD_INNER)),
            full((LANES, SSD_INNER)),
            pl.BlockSpec((1, SAMPLE_ROWS, SSD_CONV_DIM), lambda i: (i, 0, 0)),
            state_spec,
        ] + extra_specs,
        out_specs=[pl.BlockSpec((SAMPLE_ROWS, SSD_INNER), lambda i: (i, 0)), state_spec],
        out_shape=[
            jax.ShapeDtypeStruct((t, SSD_INNER), F32),
            jax.ShapeDtypeStruct(state_all.shape, F32),
        ],
        input_output_aliases=aliases,
        compiler_params=_params(("parallel",)),
        name="ssd_sample",
    )(proj, proj, proj, proj, proj, small, conv_w, conv_b, dt_bias, a_log, d_x, gn, e, conv_prev, state_all,
      *extra_args)


def _narrow_w_in_t(w_in_t, layer):
    k = w_in_t.shape[2]
    rows = lambda a, b: lax.slice(w_in_t, (layer, a, 0), (layer + 1, b, k)).reshape(b - a, k)
    lr = rows(NATIVE_UV - GLA_LOWRANK, NATIVE_UV)
    dt = rows(NATIVE_DT, NATIVE_GATE)
    pad = lambda n: jnp.zeros((n, k), w_in_t.dtype)
    return jnp.concatenate([lr, pad(LANES - GLA_LOWRANK), dt, pad(LANES - SSD_HEADS)], axis=0)


def _pad_lanes(v, n):
    return jnp.pad(v, (0, n - v.shape[0])).reshape(1, n)


def _layer_weights(i, w_narrow_next, g_next, g_mix, w_in, w_gla_lr, b_gla_lr, g_gla_norm, cm_ln_g, cm_ln_b,
                   cm_ws, cm_bs, ssd_conv_w, ssd_conv_b, ssd_dt_bias, ssd_a_log, ssd_d, g_ssd_norm,
                   w_br_gla, w_br_cm, w_br_ssd, w_o, g_ffn, w_ffn_in, w_ffn_out,
                   g_ple, w_ple_gate, w_ple_proj):
    row = lambda v: v[i].reshape(1, -1)
    n_tile = CM_CHUNK // SAMPLE_LEN
    return dict(
        w_narrow_next=w_narrow_next,
        layer=i,
        g_next=g_next.reshape(1, -1),
        w_lr=jnp.pad(w_gla_lr[i], ((0, LANES - GLA_LOWRANK), (0, 0))).astype(BF16),
        b_lr=row(b_gla_lr),
        g_gla=row(g_gla_norm),
        ln_g=row(cm_ln_g),
        ln_b=row(cm_ln_b),
        ws_prompt=cm_ws[i],
        bs_prompt=cm_bs[i].T,
        ws_sample=jnp.tile(cm_ws[i][:, :SAMPLE_LEN, :SAMPLE_LEN], (1, n_tile, n_tile)),
        bs_sample=jnp.tile(cm_bs[i][:, :SAMPLE_LEN].T, (n_tile, 1)),
        conv_w=ssd_conv_w[i],
        conv_b=row(ssd_conv_b),
        dt_bias=_pad_lanes(ssd_dt_bias[i], LANES),
        a_log=_pad_lanes(ssd_a_log[i], LANES),
        d_x=jnp.repeat(ssd_d[i], SSD_HEADDIM).reshape(1, SSD_INNER),
        g_ssd=row(g_ssd_norm),
        w_br_gla=w_br_gla[i].astype(BF16),
        w_br_cm=w_br_cm[i].astype(BF16),
        w_br_ssd=w_br_ssd[i].astype(BF16),
        w_o=w_o[i].astype(BF16),
        g_ffn=row(g_ffn),
        w_ffn_in=w_ffn_in,
        w_ffn_out=w_ffn_out,
        g_ple=row(g_ple),
        w_ple_gate=w_ple_gate[i].astype(BF16),
        w_ple_proj=w_ple_proj[i].astype(BF16),
    )


def _merge_out(x, proj, ya, yb, yc, w):
    tm = 512 if ya.dtype == BF16 else 256
    mix = _merge(ya, yb, yc, proj, w["w_br_gla"], w["w_br_cm"], w["w_br_ssd"], tm, 1024)
    return _out_proj(mix, w["w_o"], x, w["g_ffn"], 512)


def _ffn_ple(x, h, p_all, w):
    x = _matmul_residual(h, w["w_ffn_out"], w["layer"], x, 512, 512)
    return _ple(x, w["g_ple"], p_all, w["layer"], w["w_ple_gate"], w["w_ple_proj"], w["g_next"],
                w["w_narrow_next"], 512)


def _last_conv_rows(proj, bsz, length):
    keep = SSD_CONV - 1
    if length % SUBLANES == 0:
        rows = proj.reshape(bsz, length, proj.shape[-1])[:, length - keep:]
        return rows[:, :, C_XBC:C_XBC + SSD_CONV_DIM]
    xbc = proj[:, C_XBC:C_XBC + SSD_CONV_DIM]
    return xbc.reshape(bsz, length, SSD_CONV_DIM)[:, length - keep:]


def _prompt_mixers(proj, small, bsz, length, w):
    ya, s_gla = _gla_prompt(proj, small, bsz, length, w["w_lr"], w["b_lr"], w["g_gla"])
    yb, _ = _cmlp(proj, w["ln_g"], w["ln_b"], w["ws_prompt"], w["bs_prompt"], CM_CHUNK, False, BF16)
    yc, s_ssm = _ssd_prompt(proj, small, bsz, length, w["conv_w"], w["conv_b"], w["dt_bias"], w["a_log"],
                            w["d_x"], w["g_ssd"])
    s_conv = _last_conv_rows(proj, bsz, length)
    s_ssm = s_ssm.reshape(bsz, SSD_HEADS, SSD_HEADDIM, SSD_STATE)
    return (ya, yb, yc), s_gla, s_ssm, s_conv


def _sample_mixers(proj, small, bsz, gla_all, ssm_all, s_conv, prev_gla, prev_ssm, w):
    n = SAMPLE_LEN
    ya, gla_out = _gla_sample(proj, small, gla_all, w["layer"], prev_gla, w["w_lr"], w["b_lr"], w["g_gla"])
    yb, v_rows = _cmlp(proj, w["ln_g"], w["ln_b"], w["ws_sample"], w["bs_sample"], n, True, F32)
    sc = s_conv.reshape(bsz // 2, 2, SSD_CONV - 1, SSD_CONV_DIM)
    zrow = jnp.zeros((bsz // 2, 1, SSD_CONV_DIM), F32)
    conv_prev = jnp.concatenate([zrow, sc[:, 1], zrow, sc[:, 0]], axis=1)
    yc, ssm_out = _ssd_sample(proj, small, conv_prev, ssm_all, w["layer"], prev_ssm, w["conv_w"], w["conv_b"],
                              w["dt_bias"], w["a_log"], w["d_x"], w["g_ssd"])
    s_conv_new = _last_conv_rows(proj, bsz, n)
    return (ya, yb, yc), gla_out, ssm_out, s_conv_new, v_rows.reshape(bsz, n, CM_WIDTH)


def kernel(x_prompt, x_sample, state_gla, state_ssm, state_conv, p_prompt, p_sample, g_mix, w_in, w_gla_lr, b_gla_lr, g_gla_norm, cm_ln_g, cm_ln_b, cm_ws, cm_bs, ssd_conv_w, ssd_conv_b, ssd_dt_bias, ssd_a_log, ssd_d, g_ssd_norm, w_br_gla, w_br_cm, w_br_ssd, w_o, g_ffn, w_ffn_in, w_ffn_out, g_ple, w_ple_gate, w_ple_proj, g_final):
    weights = (g_mix, w_in, w_gla_lr, b_gla_lr, g_gla_norm, cm_ln_g, cm_ln_b, cm_ws, cm_bs,
               ssd_conv_w, ssd_conv_b, ssd_dt_bias, ssd_a_log, ssd_d, g_ssd_norm,
               w_br_gla, w_br_cm, w_br_ssd, w_o, g_ffn, w_ffn_in, w_ffn_out,
               g_ple, w_ple_gate, w_ple_proj)
    bp, lp, _ = x_prompt.shape
    bs, ls, _ = x_sample.shape
    assert ls == SAMPLE_LEN and lp % CHUNK == 0 and bs % 2 == 0
    xp = x_prompt.reshape(bp * lp, D_MODEL)
    xs = x_sample.reshape(bs * ls, D_MODEL)
    w_in_t = jnp.swapaxes(w_in, 1, 2)
    narrow = [_narrow_w_in_t(w_in_t, i) for i in range(DEPTH)] + [None]
    g_first = g_mix[0].reshape(1, D_MODEL)
    np_, small_p = _rms_cast(xp, g_first, narrow[0], 512)
    ns, small_s = _rms_cast(xs, g_first, narrow[0], 512)
    pp_all = p_prompt.reshape(DEPTH, bp * lp, PLE_DIM)
    ps_all = p_sample.reshape(DEPTH, bs * ls, PLE_DIM)
    gla_p, ssm_p, conv_p = [], [], []
    conv_s, v_s = [], []
    ssm_all = state_ssm.reshape(DEPTH, bs, SSD_GROUPS, SSD_GROUP_WIDTH, SSD_STATE)
    gla_s = ssm_s = None
    for i in range(DEPTH):
        final = i == DEPTH - 1
        g_next = g_final if final else g_mix[i + 1]
        w = _layer_weights(i, narrow[i + 1], g_next, *weights)
        proj_p = _proj_matmul(np_, w_in_t, i, 1024)
        proj_s = _proj_matmul(ns, w_in_t, i, 512)
        ys_p, sg, sm, sc = _prompt_mixers(proj_p, small_p, bp, lp, w)
        gla_p.append(sg)
        ssm_p.append(sm)
        conv_p.append(sc)
        ys_s, gla_s, ssm_s, sc, vr = _sample_mixers(proj_s, small_s, bs, state_gla, ssm_all, state_conv[i],
                                                    gla_s, ssm_s, w)
        conv_s.append(sc)
        v_s.append(vr)
        xp, n_ffn_p = _merge_out(xp, proj_p, *ys_p, w)
        xs, n_ffn_s = _merge_out(xs, proj_s, *ys_s, w)
        h_p = _ffn_in(n_ffn_p, w_ffn_in, i, 1024, 512)
        h_s = _ffn_in(n_ffn_s, w_ffn_in, i, 512, 512)
        outs = _ffn_ple(xp, h_p, pp_all, w)
        xp, np_, small_p = (outs[0], None, None) if final else outs
        outs = _ffn_ple(xs, h_s, ps_all, w)
        xs, ns, small_s = (outs[0], None, None) if final else outs
    y_prompt = xp.reshape(bp, lp, D_MODEL)
    y_sample = xs.reshape(bs, ls, D_MODEL)
    ssm_s = ssm_s.reshape(DEPTH, bs, SSD_HEADS, SSD_HEADDIM, SSD_STATE)
    return (y_prompt, y_sample, jnp.stack(gla_p), gla_s, jnp.stack(ssm_p), ssm_s,
            jnp.stack(conv_p), jnp.stack(conv_s), jnp.stack(v_s))
```
